```python
import math
import jax
import jax.numpy as jnp
from jax import lax
import numpy as np

D_MODEL = 1024
BATCH = 8
SEQ = 2048
DEPTH = 2

GROUP_WIDTH = D_MODEL // 4
D_MIX = 4 * GROUP_WIDTH

GDN_HEADS = 4
GDN_DK = 64
GDN_DV = GROUP_WIDTH // GDN_HEADS
GDN_CONV = 4
GDN_CHUNK = 64
GDN_QKV = GDN_HEADS * (2 * GDN_DK + GDN_DV)

MLA_HEADS = 4
MLA_NOPE = 64
MLA_ROPE = 32
MLA_V = GROUP_WIDTH // MLA_HEADS
MLA_Q_RANK = 192
MLA_KV_RANK = 128
ROPE_BASE = 10000.0
ATTN_BLOCK = 128

HGRN_HEADS = 4
HGRN_DK = 64
HGRN_DV = GROUP_WIDTH // HGRN_HEADS
HGRN_CHUNK = 16

DSA_HEADS = 4
DSA_DH = GROUP_WIDTH // DSA_HEADS
DSA_BRANCHES = ((128, 1), (512, 4), (2048, 16))
DSA_BLOCK = 128

D_FF = 4 * D_MODEL
DEEPNORM_ALPHA = (2 * DEPTH) ** 0.25
DEEPNORM_BETA = (8 * DEPTH) ** -0.25
NORM_EPS = 1e-6
MASK_VALUE = -1e30

IN_SIZES = (
    GDN_QKV, GDN_HEADS, GDN_HEADS, GDN_HEADS * GDN_DV,
    MLA_Q_RANK, MLA_KV_RANK, MLA_ROPE,
    HGRN_HEADS * HGRN_DK, HGRN_HEADS * HGRN_DK,
    HGRN_HEADS * HGRN_DV, HGRN_HEADS * HGRN_DV,
    3 * DSA_HEADS * DSA_DH,
)
D_IN = sum(IN_SIZES)

kernel_name = 'hybrid_parallel_heads_deepnorm'


def _layer_norm(x, g, b):
    xf = x.astype(jnp.float32)
    mu = jnp.mean(xf, axis=-1, keepdims=True)
    var = jnp.mean(jnp.square(xf - mu), axis=-1, keepdims=True)
    return ((xf - mu) * lax.rsqrt(var + NORM_EPS) * g + b).astype(x.dtype)


def _rms_norm(x, g):
    xf = x.astype(jnp.float32)
    return (xf * lax.rsqrt(jnp.mean(xf * xf, axis=-1, keepdims=True) + NORM_EPS) * g).astype(x.dtype)


def _l2norm(x):
    return x * lax.rsqrt(jnp.sum(x * x, axis=-1, keepdims=True) + NORM_EPS)


def _masked_exp(mask, log_ratio):
    return jnp.where(mask, jnp.exp(jnp.where(mask, log_ratio, 0.0)), 0.0)


def _rope(x, pos):
    half = x.shape[-1] // 2
    inv_freq = ROPE_BASE ** (-jnp.arange(half, dtype=jnp.float32) / half)
    ang = pos[:, None] * inv_freq[None, :]
    cos = jnp.cos(ang)[None, :, None, :].astype(x.dtype)
    sin = jnp.sin(ang)[None, :, None, :].astype(x.dtype)
    x1, x2 = x[..., :half], x[..., half:]
    return jnp.concatenate([x1 * cos - x2 * sin, x1 * sin + x2 * cos], axis=-1)


def _causal_depthwise_conv_silu(x, w):
    k_width, ch = w.shape
    y = lax.conv_general_dilated(
        x, w[:, None, :].astype(x.dtype), window_strides=(1,), padding=((k_width - 1, 0),),
        dimension_numbers=('NWC', 'WIO', 'NWC'), feature_group_count=ch)
    return jax.nn.silu(y)


def _gated_delta_rule(q, k, v, g, beta):
    b_, s_, h, dk = q.shape
    dv = v.shape[-1]
    c = GDN_CHUNK
    n = s_ // c
    q = q.reshape(b_, n, c, h, dk)
    k = k.reshape(b_, n, c, h, dk)
    v = v.reshape(b_, n, c, h, dv)
    beta = beta.reshape(b_, n, c, h)
    g = jnp.cumsum(g.reshape(b_, n, c, h), axis=2)
    causal = jnp.tril(jnp.ones((c, c), bool))
    strict = jnp.tril(jnp.ones((c, c), bool), -1)
    g_h = jnp.moveaxis(g, 2, 3)
    decay = _masked_exp(causal, g_h[..., :, None] - g_h[..., None, :])
    k_beta = k * beta[..., None]
    lower = jnp.where(strict, jnp.einsum('bnihd,bnjhd->bnhij', k_beta, k) * decay, 0.0)
    eye = jnp.eye(c, dtype=q.dtype)
    t_inv = lax.linalg.triangular_solve(lower + eye, jnp.broadcast_to(eye, lower.shape),
                                        left_side=True, lower=True, unit_diagonal=True)
    u = jnp.einsum('bnhij,bnjhd->bnihd', t_inv, v * beta[..., None])
    w = jnp.einsum('bnhij,bnjhd->bnihd', t_inv, k_beta * jnp.exp(g)[..., None])
    qk = jnp.einsum('bnihd,bnjhd->bnhij', q, k) * decay
    q_dec = q * jnp.exp(g)[..., None]
    g_last = g[:, :, -1]
    k_dec = k * jnp.exp(g_last[:, :, None] - g)[..., None]

    def step(state, xs):
        u_n, w_n, q_n, k_n, qk_n, gl_n = xs
        v_new = u_n - jnp.einsum('bchk,bhkv->bchv', w_n, state)
        o_n = jnp.einsum('bchk,bhkv->bchv', q_n, state) + jnp.einsum('bhij,bjhv->bihv', qk_n, v_new)
        state = state * jnp.exp(gl_n)[..., None, None] + jnp.einsum('bchk,bchv->bhkv', k_n, v_new)
        return state, o_n

    xs = tuple(jnp.moveaxis(t, 1, 0) for t in (u, w, q_dec, k_dec, qk, g_last))
    state0 = jnp.zeros((b_, h, dk, dv), q.dtype)
    _, o = lax.scan(step, state0, xs)
    return jnp.moveaxis(o, 0, 1).reshape(b_, s_, h, dv)


def _gated_deltanet(qkv, a, b, z, conv_w, a_log, dt_bias, norm_g):
    b_, s_, _ = qkv.shape
    qkv = _causal_depthwise_conv_silu(qkv, conv_w).astype(jnp.float32)
    q, k, v = jnp.split(qkv, [GDN_HEADS * GDN_DK, 2 * GDN_HEADS * GDN_DK], axis=-1)
    q = _l2norm(q.reshape(b_, s_, GDN_HEADS, GDN_DK)) * (GDN_DK ** -0.5)
    k = _l2norm(k.reshape(b_, s_, GDN_HEADS, GDN_DK))
    v = v.reshape(b_, s_, GDN_HEADS, GDN_DV)
    beta = jax.nn.sigmoid(b.astype(jnp.float32))
    g = -jnp.exp(a_log.astype(jnp.float32)) * jax.nn.softplus(a.astype(jnp.float32) + dt_bias.astype(jnp.float32))
    o = _gated_delta_rule(q, k, v, g, beta)
    o = _rms_norm(o, norm_g) * jax.nn.silu(z.reshape(b_, s_, GDN_HEADS, GDN_DV).astype(jnp.float32))
    return o.reshape(b_, s_, GDN_HEADS * GDN_DV).astype(z.dtype)


def _causal_block_attention(q, k, v):
    b_, s_, h, dqk = q.shape
    nblk = s_ // ATTN_BLOCK
    scale = dqk ** -0.5
    q_blocks = jnp.moveaxis(q.reshape(b_, nblk, ATTN_BLOCK, h, dqk), 1, 0)
    k_pos = jnp.arange(s_)

    def attend(args):
        qb, n = args
        s = jnp.einsum('bqhd,bkhd->bhqk', qb, k).astype(jnp.float32) * scale
        q_pos = n * ATTN_BLOCK + jnp.arange(ATTN_BLOCK)
        s = jnp.where(k_pos[None, :] <= q_pos[:, None], s, MASK_VALUE)
        p = jax.nn.softmax(s, axis=-1)
        return jnp.einsum('bhqk,bkhd->bqhd', p.astype(v.dtype), v)

    o = lax.map(attend, (q_blocks, jnp.arange(nblk)))
    return jnp.moveaxis(o, 0, 1).reshape(b_, s_, h, v.shape[-1])


def _mla(c_q, c_kv, k_rope, q_norm_g, kv_norm_g, w_uq, w_ukv, pos):
    b_, s_, _ = c_q.shape
    q = (_rms_norm(c_q, q_norm_g) @ w_uq).reshape(b_, s_, MLA_HEADS, MLA_NOPE + MLA_ROPE)
    kv = (_rms_norm(c_kv, kv_norm_g) @ w_ukv).reshape(b_, s_, MLA_HEADS, MLA_NOPE + MLA_V)
    q = jnp.concatenate([q[..., :MLA_NOPE], _rope(q[..., MLA_NOPE:], pos)], axis=-1)
    k_r = jnp.broadcast_to(_rope(k_rope[:, :, None, :], pos), (b_, s_, MLA_HEADS, MLA_ROPE))
    k = jnp.concatenate([kv[..., :MLA_NOPE], k_r], axis=-1)
    v = kv[..., MLA_NOPE:]
    o = _causal_block_attention(q, k, v)
    return o.reshape(b_, s_, MLA_HEADS * MLA_V)


def _gla_chunked(q, k, v, log_f):
    b_, s_, h, dk = q.shape
    dv = v.shape[-1]
    c = HGRN_CHUNK
    n = s_ // c
    q = q.reshape(b_, n, c, h, dk)
    k = k.reshape(b_, n, c, h, dk)
    v = v.reshape(b_, n, c, h, dv)
    cum = jnp.cumsum(log_f.reshape(b_, n, c, h, dk), axis=2)
    causal = jnp.tril(jnp.ones((c, c), bool))[:, :, None, None]
    decay = _masked_exp(causal, cum[:, :, :, None] - cum[:, :, None, :])
    scores = jnp.sum(q[:, :, :, None] * k[:, :, None, :] * decay, axis=-1)
    o_intra = jnp.einsum('bntsh,bnshv->bnthv', scores, v)
    cum_last = cum[:, :, -1]
    k_dec = k * jnp.exp(cum_last[:, :, None] - cum)

    def step(state, xs):
        k_n, v_n, gl_n = xs
        new = state * jnp.exp(gl_n)[..., None] + jnp.einsum('bchk,bchv->bhkv', k_n, v_n)
        return new, state

    xs = tuple(jnp.moveaxis(t, 1, 0) for t in (k_dec, v, cum_last))
    _, s_prev = lax.scan(step, jnp.zeros((b_, h, dk, dv), q.dtype), xs)
    o_inter = jnp.einsum('bnchk,nbhkv->bnchv', q * jnp.exp(cum), s_prev)
    return (o_intra + o_inter).reshape(b_, s_, h, dv)


def _hgrn2(q, f, i, g, lb, norm_g):
    b_, s_, _ = q.shape
    qf = q.reshape(b_, s_, HGRN_HEADS, HGRN_DK).astype(jnp.float32)
    fl = f.reshape(b_, s_, HGRN_HEADS, HGRN_DK).astype(jnp.float32)
    lb = lb.reshape(HGRN_HEADS, HGRN_DK)
    log_f = jnp.log(lb + (1.0 - lb) * jax.nn.sigmoid(fl))
    k = (1.0 - lb) * jax.nn.sigmoid(-fl)
    v = i.reshape(b_, s_, HGRN_HEADS, HGRN_DV).astype(jnp.float32)
    o = _gla_chunked(qf, k, v, log_f)
    o = _rms_norm(o, norm_g) * jax.nn.silu(g.reshape(b_, s_, HGRN_HEADS, HGRN_DV).astype(jnp.float32))
    return o.reshape(b_, s_, HGRN_HEADS * HGRN_DV).astype(q.dtype)


def _alibi_slopes(n):
    return jnp.asarray([2.0 ** (-8.0 * (j + 1) / n) for j in range(n)], dtype=jnp.float32)


def _dilated_branch(q, k, v, slopes, window, dilation):
    b_, s_, h, dh = q.shape
    length = s_ // dilation
    steps_max = window // dilation
    bb = b_ * dilation

    def sub(t):
        return jnp.moveaxis(t.reshape(b_, length, dilation, h, dh), 2, 1).reshape(bb, length, h, dh)

    nblk = -(-length // DSA_BLOCK)
    lp = nblk * DSA_BLOCK
    pad = ((0, 0), (0, lp - length), (0, 0), (0, 0))
    qs, ks, vs = (jnp.pad(sub(t), pad).reshape(bb, nblk, DSA_BLOCK, h, dh) for t in (q, k, v))
    k_cat = jnp.concatenate([jnp.concatenate([jnp.zeros_like(ks[:, :1]), ks[:, :-1]], axis=1), ks], axis=2)
    v_cat = jnp.concatenate([jnp.concatenate([jnp.zeros_like(vs[:, :1]), vs[:, :-1]], axis=1), vs], axis=2)
    qi = jnp.arange(DSA_BLOCK)[:, None]
    ki = jnp.arange(2 * DSA_BLOCK)[None, :]
    steps = DSA_BLOCK + qi - ki
    blk = jnp.arange(nblk)[:, None, None]
    valid = (steps >= 0) & (steps <= steps_max) & (blk * DSA_BLOCK + ki - DSA_BLOCK >= 0)
    alibi = -slopes[:, None, None] * (steps * dilation).astype(jnp.float32)
    s = jnp.einsum('bnqhd,bnkhd->bnhqk', qs, k_cat).astype(jnp.float32) * (dh ** -0.5) + alibi[None, None]
    s = jnp.where(valid[None, :, None], s, MASK_VALUE)
    lse = jax.nn.logsumexp(s, axis=-1)
    p = jnp.exp(s - lse[..., None])
    o = jnp.einsum('bnhqk,bnkhd->bnqhd', p.astype(v.dtype), v_cat)
    o = o.reshape(bb, lp, h, dh)[:, :length].reshape(b_, dilation, length, h, dh)
    o = jnp.moveaxis(o, 1, 2).reshape(b_, s_, h, dh)
    lse = jnp.swapaxes(lse, 2, 3).reshape(bb, lp, h)[:, :length].reshape(b_, dilation, length, h)
    lse = jnp.moveaxis(lse, 1, 2).reshape(b_, s_, h)
    return o, lse


def _dilated_attention(qkv):
    b_, s_, _ = qkv.shape
    q, k, v = (t.reshape(b_, s_, DSA_HEADS, DSA_DH) for t in jnp.split(qkv, 3, axis=-1))
    slopes = _alibi_slopes(DSA_HEADS)
    outs, lses = [], []
    for window, dilation in DSA_BRANCHES:
        o, lse = _dilated_branch(q, k, v, slopes, window, dilation)
        outs.append(o)
        lses.append(lse)
    weights = jax.nn.softmax(jnp.stack(lses, axis=0), axis=0)
    o = jnp.sum(weights[..., None] * jnp.stack(outs, axis=0).astype(jnp.float32), axis=0)
    return o.reshape(b_, s_, DSA_HEADS * DSA_DH).astype(qkv.dtype)


def _sq_relu_mlp(x, w1, w2):
    return jnp.square(jax.nn.relu(x @ w1)) @ w2


def setup_inputs(seed: int = 0) -> dict:
    key = jax.random.key(seed)
    ks = jax.random.split(key, 20)
    f32 = jnp.float32

    def normal(k, shape, scale):
        return jax.random.normal(k, shape, f32) * scale

    def gain(k, shape):
        return 1.0 + normal(k, shape, 0.02)

    dt = jnp.exp(jax.random.uniform(ks[4], (DEPTH, GDN_HEADS), f32, math.log(1e-3), math.log(1e-1)))
    return {
        'x': normal(ks[0], (BATCH, SEQ, D_MODEL), 1.0),
        'w_in': normal(ks[1], (DEPTH, D_MODEL, D_IN), D_MODEL ** -0.5),
        'gdn_conv_w': normal(ks[2], (DEPTH, GDN_CONV, GDN_QKV), GDN_CONV ** -0.5),
        'gdn_a_log': jnp.log(jax.random.uniform(ks[3], (DEPTH, GDN_HEADS), f32, 1.0, 16.0)),
        'gdn_dt_bias': dt + jnp.log(-jnp.expm1(-dt)),
        'gdn_norm_g': gain(ks[5], (DEPTH, GDN_DV)),
        'mla_q_norm_g': gain(ks[6], (DEPTH, MLA_Q_RANK)),
        'mla_kv_norm_g': gain(ks[7], (DEPTH, MLA_KV_RANK)),
        'mla_w_uq': normal(ks[8], (DEPTH, MLA_Q_RANK, MLA_HEADS * (MLA_NOPE + MLA_ROPE)), MLA_Q_RANK ** -0.5),
        'mla_w_ukv': normal(ks[9], (DEPTH, MLA_KV_RANK, MLA_HEADS * (MLA_NOPE + MLA_V)), MLA_KV_RANK ** -0.5),
        'hgrn_lb_logits': normal(ks[10], (DEPTH, HGRN_HEADS * HGRN_DK), 0.5),
        'hgrn_norm_g': gain(ks[11], (DEPTH, HGRN_DV)),
        'w_out': normal(ks[12], (DEPTH, D_MIX, D_MODEL), D_MIX ** -0.5 * DEEPNORM_BETA),
        'ln1_g': gain(ks[13], (DEPTH, D_MODEL)),
        'ln1_b': normal(ks[14], (DEPTH, D_MODEL), 0.02),
        'w_ff1': normal(ks[15], (DEPTH, D_MODEL, D_FF), D_MODEL ** -0.5),
        'w_ff2': normal(ks[16], (DEPTH, D_FF, D_MODEL), D_FF ** -0.5 * DEEPNORM_BETA),
        'ln2_g': gain(ks[17], (DEPTH, D_MODEL)),
        'ln2_b': normal(ks[18], (DEPTH, D_MODEL), 0.02),
    }


def reference(x, w_in, gdn_conv_w, gdn_a_log, gdn_dt_bias, gdn_norm_g, mla_q_norm_g, mla_kv_norm_g,
              mla_w_uq, mla_w_ukv, hgrn_lb_logits, hgrn_norm_g, w_out, ln1_g, ln1_b, w_ff1, w_ff2,
              ln2_g, ln2_b):
    pos = jnp.arange(x.shape[1], dtype=jnp.float32)
    p_lb = jax.nn.softmax(hgrn_lb_logits.astype(jnp.float32), axis=0)
    lower_bounds = jnp.cumsum(p_lb, axis=0) - p_lb[:1]
    split_points = np.cumsum(IN_SIZES)[:-1].tolist()
    for l in range(DEPTH):
        h = x @ w_in[l]
        (a_qkv, a_a, a_b, a_z, b_cq, b_ckv, b_kr,
         c_q, c_f, c_i, c_g, d_qkv) = jnp.split(h, split_points, axis=-1)
        o_a = _gated_deltanet(a_qkv, a_a, a_b, a_z, gdn_conv_w[l], gdn_a_log[l], gdn_dt_bias[l], gdn_norm_g[l])
        o_b = _mla(b_cq, b_ckv, b_kr, mla_q_norm_g[l], mla_kv_norm_g[l], mla_w_uq[l], mla_w_ukv[l], pos)
        o_c = _hgrn2(c_q, c_f, c_i, c_g, lower_bounds[l], hgrn_norm_g[l])
        o_d = _dilated_attention(d_qkv)
        mixed = jnp.concatenate([o_a, o_b.astype(x.dtype), o_c, o_d], axis=-1) @ w_out[l]
        x = _layer_norm(DEEPNORM_ALPHA * x + mixed, ln1_g[l], ln1_b[l])
        x = _layer_norm(DEEPNORM_ALPHA * x + _sq_relu_mlp(x, w_ff1[l], w_ff2[l]), ln2_g[l], ln2_b[l])
    return x
```

```python
import functools
import math

import numpy as np
import jax
import jax.numpy as jnp
from jax import lax
from jax.experimental import pallas as pl
from jax.experimental.pallas import tpu as pltpu

F32 = jnp.float32
BF16 = jnp.bfloat16

D_MODEL = 1024
DEPTH = 2
GROUP_WIDTH = 256
N_HEADS = 4
HEAD_DIM = 64
GDN_CONV = 4
GDN_CHUNK = 64
MLA_NOPE = 64
MLA_ROPE = 32
MLA_Q_RANK = 192
MLA_KV_RANK = 128
ROPE_BASE = 10000.0
HGRN_CHUNK = 16
DSA_BRANCHES = ((128, 1), (512, 4), (2048, 16))
DSA_BLOCK = 128
ALIBI_SLOPES = tuple(2.0 ** (-8.0 * (j + 1) / N_HEADS) for j in range(N_HEADS))
D_FF = 4 * D_MODEL
DEEPNORM_ALPHA = (2 * DEPTH) ** 0.25
NORM_EPS = 1e-6
MASK_VALUE = -1e30

LANES = 128
SUBLANES = 8
VMEM_LIMIT_BYTES = 56 * 1024 * 1024

C_GDN_QKV = 0
C_GDN_A = 768
C_GDN_B = 1024
C_GDN_Z = 1280
C_MLA_CQ = 1536
C_MLA_CKV = 1792
C_MLA_KR = 1920
C_MLA_KRS = 2048
C_HGRN = 2304
C_DSA = 3328
C_TOTAL = 4096
IN_SIZES = (768, 4, 4, 256, 192, 128, 32, 256, 256, 256, 256, 768)
ROW_TILE = 512


def _bf(x):
    return x.astype(BF16)


def _dot(a, b):
    return jnp.dot(a, b, preferred_element_type=F32)


def _dot_nt(a, b):
    return lax.dot_general(a, b, (((1,), (1,)), ((), ())), preferred_element_type=F32)


def _dot_tn(a, b):
    return lax.dot_general(a, b, (((0,), (0,)), ((), ())), preferred_element_type=F32)


def _split3(x):
    hi = _bf(x)
    r1 = x - hi.astype(F32)
    mid = _bf(r1)
    lo = _bf(r1 - mid.astype(F32))
    return hi, mid, lo


def _dot_sel_r(x, sel):
    hi, mid, lo = _split3(x)
    return _dot(hi, sel) + _dot(mid, sel) + _dot(lo, sel)


def _dot_sel_l(sel, x):
    hi, mid, lo = _split3(x)
    return _dot(sel, hi) + _dot(sel, mid) + _dot(sel, lo)


def _sigmoid(x):
    return 1.0 / (1.0 + jnp.exp(-x))


def _silu(x):
    return x * _sigmoid(x)


def _softplus(x):
    return jnp.maximum(x, 0.0) + jnp.log(1.0 + jnp.exp(-jnp.abs(x)))


def _iota2(shape, axis):
    return lax.broadcasted_iota(jnp.int32, shape, axis)


def _head_block_ones(n):
    r = _iota2((n, n), 0) >> 6
    c = _iota2((n, n), 1) >> 6
    return jnp.where(r == c, 1.0, 0.0).astype(BF16)


def _group_sum(x, ones_bd):
    return _dot_sel_r(x, ones_bd)


def _layer_norm_rows(y, g, b):
    mu = jnp.mean(y, axis=-1, keepdims=True)
    d = y - mu
    var = jnp.mean(d * d, axis=-1, keepdims=True)
    return d * lax.rsqrt(var + NORM_EPS) * g + b


def _params(*sem):
    return pltpu.CompilerParams(dimension_semantics=sem, vmem_limit_bytes=VMEM_LIMIT_BYTES)


def _inproj_kernel(x_ref, w_ref, o_ref):
    xb = _bf(x_ref[...])
    for c in range(C_TOTAL // 512):
        o_ref[:, c * 512:(c + 1) * 512] = _dot(xb, w_ref[:, c * 512:(c + 1) * 512])


def _inproj(x2d, w):
    n = x2d.shape[0]
    return pl.pallas_call(
        _inproj_kernel,
        grid=(n // ROW_TILE,),
        in_specs=[pl.BlockSpec((ROW_TILE, D_MODEL), lambda i: (i, 0)),
                  pl.BlockSpec((D_MODEL, C_TOTAL), lambda i: (0, 0))],
        out_specs=pl.BlockSpec((ROW_TILE, C_TOTAL), lambda i: (i, 0)),
        out_shape=jax.ShapeDtypeStruct((n, C_TOTAL), F32),
        compiler_params=_params("parallel"),
        name="inproj",
    )(x2d, w)


def _gdn_kernel(qkv_ref, a_ref, b_ref, z_ref, convw_ref, avec_ref, dtb_ref, ng_ref, o_ref, s_ref):
    seq = qkv_ref.shape[0]
    c = GDN_CHUNK
    w = GROUP_WIDTH
    s_ref[...] = jnp.zeros_like(s_ref)

    row = _iota2((w, w), 0)
    col = _iota2((w, w), 1)
    same_head = (row >> 6) == (col >> 6)
    ir = row & 63
    ic = col & 63
    causal = same_head & (ic <= ir)
    strict = same_head & (ic < ir)
    eye = jnp.where(row == col, 1.0, 0.0)
    ones_bd = jnp.where(same_head, 1.0, 0.0).astype(BF16)
    ones_full = jnp.ones((w, w), BF16)
    tril_c = jnp.where(_iota2((c, c), 1) <= _iota2((c, c), 0), 1.0, 0.0).astype(BF16)
    lane_head = _iota2((c, w), 1) >> 6

    def stack_heads(t):
        return jnp.concatenate([jnp.where(lane_head == h, t, 0.0) for h in range(N_HEADS)], axis=0)

    def unstack_heads(t):
        return t[0:c] + t[c:2 * c] + t[2 * c:3 * c] + t[3 * c:4 * c]

    convw = convw_ref[...]
    avec = avec_ref[...]
    dtb = dtb_ref[...]
    ng = ng_ref[...]

    def body(n, carry):
        r0 = pl.multiple_of(n * c, c)
        cur = qkv_ref[pl.ds(r0, c), :]
        prev = qkv_ref[pl.ds(pl.multiple_of(jnp.maximum(r0 - SUBLANES, 0), SUBLANES), SUBLANES), :]
        prev = jnp.where(n > 0, prev, 0.0)
        win = jnp.concatenate([prev, cur], axis=0)
        y = cur * convw[GDN_CONV - 1:GDN_CONV, :]
        for j in range(1, GDN_CONV):
            y = y + pltpu.roll(win, j, 0)[SUBLANES:, :] * convw[GDN_CONV - 1 - j:GDN_CONV - j, :]
        y = _silu(y)
        q = y[:, 0:w]
        k = y[:, w:2 * w]
        v = y[:, 2 * w:3 * w]
        q = q * lax.rsqrt(_group_sum(q * q, ones_bd) + NORM_EPS) * (HEAD_DIM ** -0.5)
        k = k * lax.rsqrt(_group_sum(k * k, ones_bd) + NORM_EPS)
        beta = _sigmoid(b_ref[pl.ds(r0, c), :])
        gstep = avec * _softplus(a_ref[pl.ds(r0, c), :] + dtb)
        g = _dot_sel_l(tril_c, gstep)
        eg = jnp.exp(g)
        g_last = g[c - 1:c, :]
        kb = k * beta
        gc = _dot_sel_r(stack_heads(g), ones_full) * (1.0 / HEAD_DIM)
        gr = gc.T
        decay = jnp.where(causal, jnp.exp(jnp.where(causal, gc - gr, 0.0)), 0.0)
        k_st = _bf(stack_heads(k))
        lhs = _bf(jnp.concatenate([stack_heads(kb), stack_heads(q)], axis=0))
        prod = _dot_nt(lhs, k_st)
        lower = jnp.where(strict, prod[0:w] * decay, 0.0)
        qk = prod[w:2 * w] * decay
        m = -lower
        t = eye + m
        mb = _bf(m)
        m = _dot(mb, mb)
        for it in range(5):
            mb = _bf(m)
            if it < 4:
                both = _dot(mb, _bf(jnp.concatenate([m, t], axis=1)))
                t = t + both[:, w:2 * w]
                m = both[:, 0:w]
            else:
                t = t + _dot(mb, _bf(t))
        rhs = _bf(jnp.concatenate([stack_heads(v * beta), stack_heads(kb * eg)], axis=1))
        uw = _dot(_bf(t), rhs)
        u = unstack_heads(uw[:, 0:w])
        wmat = unstack_heads(uw[:, w:2 * w])
        s_bd = s_ref[...]
        sb = _bf(s_bd)
        ws_qs = _dot(_bf(jnp.concatenate([wmat, q * eg], axis=0)), sb)
        v_new = u - ws_qs[0:c]
        o = ws_qs[c:2 * c] + unstack_heads(_dot(_bf(qk), _bf(stack_heads(v_new))))
        k_dec = k * jnp.exp(g_last - g)
        upd = _dot_tn(_bf(k_dec), _bf(v_new))
        s_ref[...] = s_bd * jnp.exp(g_last) + jnp.where(same_head, upd, 0.0)
        ms = _group_sum(o * o, ones_bd) * (1.0 / HEAD_DIM)
        o = o * lax.rsqrt(ms + NORM_EPS) * ng * _silu(z_ref[pl.ds(r0, c), :])
        o_ref[pl.ds(r0, c), :] = o
        return carry

    lax.fori_loop(0, seq // c, body, 0)


def _gdn(h, batch, seq, convw, avec, dtb, ng):
    blk = lambda width, cblk: pl.BlockSpec((seq, width), lambda b: (b, cblk))
    full = lambda a: pl.BlockSpec(a.shape, lambda b: (0, 0))
    return pl.pallas_call(
        _gdn_kernel,
        grid=(batch,),
        in_specs=[blk(768, C_GDN_QKV // 768), blk(256, C_GDN_A // 256), blk(256, C_GDN_B // 256),
                  blk(256, C_GDN_Z // 256), full(convw), full(avec), full(dtb), full(ng)],
        out_specs=pl.BlockSpec((seq, GROUP_WIDTH), lambda b: (b, 0)),
        out_shape=jax.ShapeDtypeStruct((batch * seq, GROUP_WIDTH), F32),
        scratch_shapes=[pltpu.VMEM((GROUP_WIDTH, GROUP_WIDTH), F32)],
        compiler_params=_params("parallel"),
        name="gdn",
    )(h, h, h, h, convw, avec, dtb, ng)


def _mla_prep_kernel(cq_ref, ckv_ref, kr_ref, krs_ref, cos_ref, sin_ref, qg_ref, kvg_ref,
                     wq_ref, wqs_ref, wk_ref, wv_ref, q_ref, k_ref, v_ref):
    cq = cq_ref[...]
    nq = cq * lax.rsqrt(jnp.sum(cq * cq, axis=-1, keepdims=True) * (1.0 / MLA_Q_RANK) + NORM_EPS) * qg_ref[...]
    nqb = _bf(nq)
    cos1 = cos_ref[...]
    sin1 = sin_ref[...]
    cos4 = jnp.concatenate([cos1] * N_HEADS, axis=1)
    sin4 = jnp.concatenate([sin1] * N_HEADS, axis=1)
    scale = (MLA_NOPE + MLA_ROPE) ** -0.5
    q = (_dot(nqb, wq_ref[...]) * cos4 + _dot(nqb, wqs_ref[...]) * sin4) * scale
    q_ref[...] = _bf(q)
    ckv = ckv_ref[...]
    nkv = ckv * lax.rsqrt(jnp.mean(ckv * ckv, axis=-1, keepdims=True) + NORM_EPS) * kvg_ref[...]
    nkvb = _bf(nkv)
    kr = kr_ref[...] * cos1 + krs_ref[...] * sin1
    k = _dot(nkvb, wk_ref[...]) + jnp.concatenate([kr] * N_HEADS, axis=1)
    k_ref[...] = _bf(k)
    v_ref[...] = _bf(_dot(nkvb, wv_ref[...]))


def _mla_prep(h, seq, cos_t, sin_t, qg, kvg, wq, wqs, wk, wv):
    n = h.shape[0]
    tm = ROW_TILE
    pos_blocks = seq // tm
    blk = lambda width, cblk: pl.BlockSpec((tm, width), lambda i: (i, cblk))
    full = lambda a: pl.BlockSpec(a.shape, lambda i: (0, 0))
    tab = pl.BlockSpec((tm, LANES), lambda i: (i % pos_blocks, 0))
    return pl.pallas_call(
        _mla_prep_kernel,
        grid=(n // tm,),
        in_specs=[blk(256, C_MLA_CQ // 256), blk(128, C_MLA_CKV // 128), blk(128, C_MLA_KR // 128),
                  blk(128, C_MLA_KRS // 128), tab, tab, full(qg), full(kvg),
                  full(wq), full(wqs), full(wk), full(wv)],
        out_specs=[pl.BlockSpec((tm, 512), lambda i: (i, 0)), pl.BlockSpec((tm, 512), lambda i: (i, 0)),
                   pl.BlockSpec((tm, 256), lambda i: (i, 0))],
        out_shape=[jax.ShapeDtypeStruct((n, 512), BF16), jax.ShapeDtypeStruct((n, 512), BF16),
                   jax.ShapeDtypeStruct((n, 256), BF16)],
        compiler_params=_params("parallel"),
        name="mla_prep",
    )(h, h, h, h, cos_t, sin_t, qg, kvg, wq, wqs, wk, wv)


MLA_Q_BLOCK = 256


def _mla_attn_kernel(q_ref, k_ref, v_ref, o_ref):
    seq = q_ref.shape[0]
    tq = MLA_Q_BLOCK
    ri = _iota2((tq, tq), 0)
    ci = _iota2((tq, tq), 1)
    diag_ok = ci <= ri
    lane = _iota2((tq, LANES), 1)
    for qi in range(seq // tq):
        q0 = qi * tq
        v_off = v_ref[0:q0, :] if qi else None
        v_diag = v_ref[q0:q0 + tq, :]
        outs = []
        for hh in range(2):
            qh = q_ref[q0:q0 + tq, hh * LANES:(hh + 1) * LANES]
            s_diag = _dot_nt(qh, k_ref[q0:q0 + tq, hh * LANES:(hh + 1) * LANES])
            s_diag = jnp.where(diag_ok, s_diag, MASK_VALUE)
            m = jnp.max(s_diag, axis=-1, keepdims=True)
            if qi:
                s_off = _dot_nt(qh, k_ref[0:q0, hh * LANES:(hh + 1) * LANES])
                m = jnp.maximum(m, jnp.max(s_off, axis=-1, keepdims=True))
            p_diag = jnp.exp(s_diag - m)
            l = jnp.sum(p_diag, axis=-1, keepdims=True)
            acc = _dot(_bf(p_diag), v_diag)
            if qi:
                p_off = jnp.exp(s_off - m)
                l = l + jnp.sum(p_off, axis=-1, keepdims=True)
                acc = acc + _dot(_bf(p_off), v_off)
            outs.append(acc / l)
        o_ref[q0:q0 + tq, :] = jnp.where(lane < HEAD_DIM, outs[0], outs[1])


def _mla_attn(q, k, v, batch, seq):
    return pl.pallas_call(
        _mla_attn_kernel,
        grid=(batch, 2),
        in_specs=[pl.BlockSpec((seq, 256), lambda b, p: (b, p)),
                  pl.BlockSpec((seq, 256), lambda b, p: (b, p)),
                  pl.BlockSpec((seq, LANES), lambda b, p: (b, p))],
        out_specs=pl.BlockSpec((seq, LANES), lambda b, p: (b, p)),
        out_shape=jax.ShapeDtypeStruct((batch * seq, GROUP_WIDTH), F32),
        compiler_params=_params("parallel", "parallel"),
        name="mla_attn",
    )(q, k, v)


HGRN_ROWS = 128


def _hgrn_kernel(q_ref, f_ref, i_ref, g_ref, lb_ref, ng_ref, o_ref, st_ref):
    seq = q_ref.shape[0]
    w = GROUP_WIDTH
    rr = HGRN_ROWS
    cc = HGRN_CHUNK
    st_ref[...] = jnp.zeros_like(st_ref)
    row = _iota2((w, w), 0)
    col = _iota2((w, w), 1)
    same_head = (row >> 6) == (col >> 6)
    ones_bd = jnp.where(same_head, 1.0, 0.0).astype(BF16)
    tr = _iota2((rr, rr), 0)
    tc = _iota2((rr, rr), 1)
    tril_chunks = jnp.where(((tr >> 4) == (tc >> 4)) & (tc <= tr), 1.0, 0.0).astype(BF16)
    pos_in_chunk = _iota2((rr, w), 0) & (cc - 1)
    lb = lb_ref[...]
    ng = ng_ref[...]

    def body(n, carry):
        r0 = pl.multiple_of(n * rr, rr)
        q = q_ref[pl.ds(r0, rr), :]
        fl = f_ref[pl.ds(r0, rr), :]
        v = i_ref[pl.ds(r0, rr), :]
        log_f = jnp.log(lb + (1.0 - lb) * _sigmoid(fl))
        kk = (1.0 - lb) * _sigmoid(-fl)
        cum = _dot_sel_l(tril_chunks, log_f)
        o = (_dot(_bf(q * kk), ones_bd)) * v
        for d in range(1, cc):
            ok = pos_in_chunk >= d
            x = q * pltpu.roll(kk, d, 0) * jnp.exp(jnp.where(ok, cum - pltpu.roll(cum, d, 0), 0.0))
            x = jnp.where(ok, x, 0.0)
            o = o + _dot(_bf(x), ones_bd) * pltpu.roll(v, d, 0)
        q_dec = q * jnp.exp(cum)
        parts = []
        for j in range(rr // cc):
            sl = slice(j * cc, (j + 1) * cc)
            st = st_ref[...]
            parts.append(_dot_nt(_bf(q_dec[sl]), _bf(st)))
            cum_last = cum[(j + 1) * cc - 1:(j + 1) * cc, :]
            k_dec = kk[sl] * jnp.exp(cum_last - cum[sl])
            upd = _dot_tn(_bf(v[sl]), _bf(k_dec))
            st_ref[...] = st * jnp.exp(cum_last) + jnp.where(same_head, upd, 0.0)
        o = o + jnp.concatenate(parts, axis=0)
        ms = _group_sum(o * o, ones_bd) * (1.0 / HEAD_DIM)
        o_ref[pl.ds(r0, rr), :] = o * lax.rsqrt(ms + NORM_EPS) * ng * _silu(g_ref[pl.ds(r0, rr), :])
        return carry

    lax.fori_loop(0, seq // rr, body, 0)


def _hgrn(h, batch, seq, lb, ng):
    blk = lambda j: pl.BlockSpec((seq, GROUP_WIDTH), lambda b: (b, C_HGRN // GROUP_WIDTH + j))
    full = lambda a: pl.BlockSpec(a.shape, lambda b: (0, 0))
    return pl.pallas_call(
        _hgrn_kernel,
        grid=(batch,),
        in_specs=[blk(0), blk(1), blk(2), blk(3), full(lb), full(ng)],
        out_specs=pl.BlockSpec((seq, GROUP_WIDTH), lambda b: (b, 0)),
        out_shape=jax.ShapeDtypeStruct((batch * seq, GROUP_WIDTH), F32),
        scratch_shapes=[pltpu.VMEM((GROUP_WIDTH, GROUP_WIDTH), F32)],
        compiler_params=_params("parallel"),
        name="hgrn",
    )(h, h, h, h, lb, ng)


def _dsa_kernel(q_ref, k_ref, v_ref, o_ref, num_ref, m_ref, l_ref):
    seq = q_ref.shape[0]
    blk = DSA_BLOCK
    pair = pl.program_id(1)
    qi = _iota2((blk, 2 * blk), 0)
    ki = _iota2((blk, 2 * blk), 1)
    steps = blk + qi - ki
    lane = _iota2((blk, LANES), 1)
    scale = HEAD_DIM ** -0.5

    for bi, (window, dil) in enumerate(DSA_BRANCHES):
        length = seq // dil
        nblk = length // blk
        steps_max = window // dil
        in_window = (steps >= 0) & (steps <= steps_max)
        dist = (steps * dil).astype(F32)

        def body(idx, carry, bi=bi, dil=dil, nblk=nblk, in_window=in_window, dist=dist):
            r = idx // nblk
            n = idx - r * nblk
            start = r + dil * blk * n
            start_prev = jnp.maximum(start - dil * blk, 0)
            if dil == 1:
                ld = lambda ref, s0: ref[pl.ds(s0, blk), :]
            else:
                ld = lambda ref, s0: ref[pl.ds(s0, blk, stride=dil), :]
            qb = ld(q_ref, start)
            kb = _bf(jnp.concatenate([ld(k_ref, start_prev), ld(k_ref, start)], axis=0))
            vb = _bf(jnp.concatenate([ld(v_ref, start_prev), ld(v_ref, start)], axis=0))
            valid = in_window & ((n > 0) | (ki >= blk))
            nums, ms, ls = [], [], []
            for hh in range(2):
                slope = jnp.where(pair == 0, ALIBI_SLOPES[hh], ALIBI_SLOPES[2 + hh])
                qm = _bf(jnp.where((lane >> 6) == hh, qb, 0.0) * scale)
                s = _dot_nt(qm, kb) - slope * dist
                s = jnp.where(valid, s, MASK_VALUE)
                m = jnp.max(s, axis=-1, keepdims=True)
                p = jnp.exp(s - m)
                ls.append(jnp.sum(p, axis=-1, keepdims=True))
                ms.append(m)
                nums.append(_dot(_bf(p), vb))
            first = lane < HEAD_DIM
            num_t = jnp.where(first, nums[0], nums[1])
            m_t = jnp.where(first, ms[0], ms[1])
            l_t = jnp.where(first, ls[0], ls[1])
            if dil == 1:
                dst = pl.ds(bi * seq + start, blk)
            else:
                dst = pl.ds(bi * seq + start, blk, stride=dil)
            num_ref[dst, :] = num_t
            m_ref[dst, :] = m_t
            l_ref[dst, :] = l_t
            return carry

        lax.fori_loop(0, seq // blk, body, 0)

    def merge(n, carry):
        r0 = pl.multiple_of(n * blk, blk)
        rows = [pl.ds(bi * seq + r0, blk) for bi in range(len(DSA_BRANCHES))]
        m0, m1, m2 = [m_ref[r, :] for r in rows]
        mm = jnp.maximum(jnp.maximum(m0, m1), m2)
        w0, w1, w2 = jnp.exp(m0 - mm), jnp.exp(m1 - mm), jnp.exp(m2 - mm)
        num = w0 * num_ref[rows[0], :] + w1 * num_ref[rows[1], :] + w2 * num_ref[rows[2], :]
        den = w0 * l_ref[rows[0], :] + w1 * l_ref[rows[1], :] + w2 * l_ref[rows[2], :]
        o_ref[pl.ds(r0, blk), :] = num / den
        return carry

    lax.fori_loop(0, seq // blk, merge, 0)


def _dsa(h, batch, seq):
    base = C_DSA // LANES
    blk = lambda j: pl.BlockSpec((seq, LANES), lambda b, p: (b, base + 2 * j + p))
    nb = len(DSA_BRANCHES)
    return pl.pallas_call(
        _dsa_kernel,
        grid=(batch, 2),
        in_specs=[blk(0), blk(1), blk(2)],
        out_specs=pl.BlockSpec((seq, LANES), lambda b, p: (b, p)),
        out_shape=jax.ShapeDtypeStruct((batch * seq, GROUP_WIDTH), F32),
        scratch_shapes=[pltpu.VMEM((nb * seq, LANES), F32), pltpu.VMEM((nb * seq, LANES), F32),
                        pltpu.VMEM((nb * seq, LANES), F32)],
        compiler_params=_params("parallel", "parallel"),
        name="dsa",
    )(h, h, h)


def _outproj_kernel(oa_ref, ob_ref, oc_ref, od_ref, x_ref, w_ref, g_ref, b_ref, y_ref):
    gw = GROUP_WIDTH
    mixed = _dot(_bf(oa_ref[...]), w_ref[0:gw, :])
    mixed = mixed + _dot(_bf(ob_ref[...]), w_ref[gw:2 * gw, :])
    mixed = mixed + _dot(_bf(oc_ref[...]), w_ref[2 * gw:3 * gw, :])
    mixed = mixed + _dot(_bf(od_ref[...]), w_ref[3 * gw:4 * gw, :])
    y_ref[...] = _layer_norm_rows(DEEPNORM_ALPHA * x_ref[...] + mixed, g_ref[...], b_ref[...])


def _outproj_ln(oa, ob, oc, od, x2d, w, g, b):
    n = x2d.shape[0]
    tm = ROW_TILE
    grp = pl.BlockSpec((tm, GROUP_WIDTH), lambda i: (i, 0))
    full = lambda a: pl.BlockSpec(a.shape, lambda i: (0, 0))
    row = pl.BlockSpec((tm, D_MODEL), lambda i: (i, 0))
    return pl.pallas_call(
        _outproj_kernel,
        grid=(n // tm,),
        in_specs=[grp, grp, grp, grp, row, full(w), full(g), full(b)],
        out_specs=row,
        out_shape=jax.ShapeDtypeStruct((n, D_MODEL), F32),
        compiler_params=_params("parallel"),
        name="outproj_ln",
    )(oa, ob, oc, od, x2d, w, g, b)


FF_CHUNK = 1024


def _ffn_kernel(x_ref, w1_ref, w2_ref, g_ref, b_ref, y_ref):
    x = x_ref[...]
    xb = _bf(x)
    acc = jnp.zeros(x.shape, F32)
    for c in range(D_FF // FF_CHUNK):
        hmid = _dot(xb, w1_ref[:, c * FF_CHUNK:(c + 1) * FF_CHUNK])
        hmid = jnp.square(jnp.maximum(hmid, 0.0))
        acc = acc + _dot(_bf(hmid), w2_ref[c * FF_CHUNK:(c + 1) * FF_CHUNK, :])
    y_ref[...] = _layer_norm_rows(DEEPNORM_ALPHA * x + acc, g_ref[...], b_ref[...])


def _ffn_ln(x2d, w1, w2, g, b):
    n = x2d.shape[0]
    tm = ROW_TILE
    row = pl.BlockSpec((tm, D_MODEL), lambda i: (i, 0))
    resident = lambda a: pl.BlockSpec(a.shape, lambda i: (0, 0), pipeline_mode=pl.Buffered(1))
    full = lambda a: pl.BlockSpec(a.shape, lambda i: (0, 0))
    return pl.pallas_call(
        _ffn_kernel,
        grid=(n // tm,),
        in_specs=[row, resident(w1), resident(w2), full(g), full(b)],
        out_specs=row,
        out_shape=jax.ShapeDtypeStruct((n, D_MODEL), F32),
        compiler_params=_params("parallel"),
        name="ffn_ln",
    )(x2d, w1, w2, g, b)


def _expand_heads(wcols):
    return jnp.repeat(wcols, HEAD_DIM, axis=1)


def _arrange_w_in(w):
    pts = np.cumsum(IN_SIZES)[:-1].tolist()
    (a_qkv, a_a, a_b, a_z, b_cq, b_ckv, b_kr, c_q, c_f, c_i, c_g, d_qkv) = jnp.split(w, pts, axis=1)
    d = w.shape[0]
    z = lambda n: jnp.zeros((d, n), w.dtype)
    half = MLA_ROPE // 2
    kr_sw = jnp.concatenate([b_kr[:, half:], b_kr[:, :half]], axis=1)
    cols = [a_qkv, _expand_heads(a_a), _expand_heads(a_b), a_z,
            b_cq, z(256 - MLA_Q_RANK), b_ckv,
            z(MLA_NOPE), b_kr, z(LANES - MLA_NOPE - MLA_ROPE),
            z(MLA_NOPE), kr_sw, z(LANES - MLA_NOPE - MLA_ROPE), z(C_HGRN - C_MLA_KRS - LANES),
            c_q, c_f, c_i, c_g, d_qkv]
    out = jnp.concatenate(cols, axis=1)
    out = jnp.concatenate([out, z(C_TOTAL - out.shape[1])], axis=1)
    return _bf(out)


def _arrange_mla_weights(w_uq, w_ukv):
    rq = w_uq.shape[0]
    wq = w_uq.reshape(rq, N_HEADS, MLA_NOPE + MLA_ROPE)
    half = MLA_ROPE // 2
    zq = jnp.zeros((rq, N_HEADS, LANES - MLA_NOPE - MLA_ROPE), w_uq.dtype)
    q_main = jnp.concatenate([wq, zq], axis=2)
    rope = wq[:, :, MLA_NOPE:]
    rope_sw = jnp.concatenate([rope[:, :, half:], rope[:, :, :half]], axis=2)
    q_swap = jnp.concatenate([jnp.zeros((rq, N_HEADS, MLA_NOPE), w_uq.dtype), rope_sw, zq], axis=2)
    pad_rows = lambda m: jnp.concatenate([m, jnp.zeros((256 - rq, m.shape[1]), m.dtype)], axis=0)
    q_main = pad_rows(q_main.reshape(rq, N_HEADS * LANES))
    q_swap = pad_rows(q_swap.reshape(rq, N_HEADS * LANES))
    rkv = w_ukv.shape[0]
    wkv = w_ukv.reshape(rkv, N_HEADS, MLA_NOPE + HEAD_DIM)
    k_w = jnp.concatenate([wkv[:, :, :MLA_NOPE], jnp.zeros((rkv, N_HEADS, LANES - MLA_NOPE), w_ukv.dtype)], axis=2)
    v_w = wkv[:, :, MLA_NOPE:]
    return _bf(q_main), _bf(q_swap), _bf(k_w.reshape(rkv, N_HEADS * LANES)), _bf(v_w.reshape(rkv, N_HEADS * HEAD_DIM))


def _rope_tables(seq):
    half = MLA_ROPE // 2
    pos = jnp.arange(seq, dtype=F32)
    inv_freq = ROPE_BASE ** (-jnp.arange(half, dtype=F32) / half)
    ang = pos[:, None] * inv_freq[None, :]
    cos, sin = jnp.cos(ang), jnp.sin(ang)
    ones = jnp.ones((seq, MLA_NOPE), F32)
    zeros = jnp.zeros((seq, MLA_NOPE), F32)
    tail1 = jnp.ones((seq, LANES - MLA_NOPE - MLA_ROPE), F32)
    tail0 = jnp.zeros((seq, LANES - MLA_NOPE - MLA_ROPE), F32)
    cos_t = jnp.concatenate([ones, cos, cos, tail1], axis=1)
    sin_t = jnp.concatenate([zeros, -sin, sin, tail0], axis=1)
    return cos_t, sin_t


def _tile_heads(vec):
    return jnp.tile(vec.astype(F32), N_HEADS)[None, :]


def kernel(x, w_in, gdn_conv_w, gdn_a_log, gdn_dt_bias, gdn_norm_g, mla_q_norm_g, mla_kv_norm_g,
           mla_w_uq, mla_w_ukv, hgrn_lb_logits, hgrn_norm_g, w_out, ln1_g, ln1_b, w_ff1, w_ff2,
           ln2_g, ln2_b):
    batch, seq, d_model = x.shape
    assert d_model == D_MODEL and seq % (16 * DSA_BLOCK) == 0 and (batch * seq) % ROW_TILE == 0
    x2d = x.reshape(batch * seq, d_model)
    p_lb = jax.nn.softmax(hgrn_lb_logits.astype(F32), axis=0)
    lower_bounds = jnp.cumsum(p_lb, axis=0) - p_lb[:1]
    cos_t, sin_t = _rope_tables(seq)
    for l in range(DEPTH):
        h = _inproj(x2d, _arrange_w_in(w_in[l]))
        avec = jnp.repeat(-jnp.exp(gdn_a_log[l].astype(F32)), HEAD_DIM)[None, :]
        dtb = jnp.repeat(gdn_dt_bias[l].astype(F32), HEAD_DIM)[None, :]
        o_a = _gdn(h, batch, seq, gdn_conv_w[l].astype(F32), avec, dtb, _tile_heads(gdn_norm_g[l]))
        wq, wqs, wk, wv = _arrange_mla_weights(mla_w_uq[l], mla_w_ukv[l])
        qg = jnp.concatenate([mla_q_norm_g[l].astype(F32), jnp.zeros((256 - MLA_Q_RANK,), F32)])[None, :]
        q_m, k_m, v_m = _mla_prep(h, seq, cos_t, sin_t, qg, mla_kv_norm_g[l].astype(F32)[None, :],
                                  wq, wqs, wk, wv)
        o_b = _mla_attn(q_m, k_m, v_m, batch, seq)
        o_c = _hgrn(h, batch, seq, lower_bounds[l][None, :], _tile_heads(hgrn_norm_g[l]))
        o_d = _dsa(h, batch, seq)
        x2d = _outproj_ln(o_a, o_b, o_c, o_d, x2d, _bf(w_out[l]), ln1_g[l][None, :], ln1_b[l][None, :])
        x2d = _ffn_ln(x2d, _bf(w_ff1[l]), _bf(w_ff2[l]), ln2_g[l][None, :], ln2_b[l][None, :])
    return x2d.reshape(batch, seq, d_model)
```

```python
import functools
import math

import numpy as np
import jax
import jax.numpy as jnp
from jax import lax
from jax.experimental import pallas as pl
from jax.experimental.pallas import tpu as pltpu

F32 = jnp.float32
BF16 = jnp.bfloat16

D_MODEL = 1024
DEPTH = 2
GROUP_WIDTH = 256
N_HEADS = 4
HEAD_DIM = 64
GDN_CONV = 4
GDN_CHUNK = 64
MLA_NOPE = 64
MLA_ROPE = 32
MLA_Q_RANK = 192
MLA_KV_RANK = 128
ROPE_BASE = 10000.0
HGRN_CHUNK = 16
DSA_BRANCHES = ((128, 1), (512, 4), (2048, 16))
DSA_BLOCK = 128
ALIBI_SLOPES = tuple(2.0 ** (-8.0 * (j + 1) / N_HEADS) for j in range(N_HEADS))
D_FF = 4 * D_MODEL
DEEPNORM_ALPHA = (2 * DEPTH) ** 0.25
NORM_EPS = 1e-6
MASK_VALUE = -1e30
LOG2E = 1.4426950408889634

LANES = 128
SUBLANES = 8
VMEM_LIMIT_BYTES = 56 * 1024 * 1024

C_GDN_QKV = 0
C_GDN_A = 768
C_GDN_B = 1024
C_GDN_Z = 1280
C_MLA_CQ = 1536
C_MLA_CKV = 1792
C_MLA_KR = 1920
C_MLA_KRS = 2048
C_HGRN = 2304
C_DSA = 3328
C_TOTAL = 4096
IN_SIZES = (768, 4, 4, 256, 192, 128, 32, 256, 256, 256, 256, 768)
ROW_TILE = 512


def _bf(x):
    return x.astype(BF16)


def _dot(a, b):
    return jnp.dot(a, b, preferred_element_type=F32)


def _dot_nt(a, b):
    return lax.dot_general(a, b, (((1,), (1,)), ((), ())), preferred_element_type=F32)


def _dot_tn(a, b):
    return lax.dot_general(a, b, (((0,), (0,)), ((), ())), preferred_element_type=F32)


def _split3(x):
    hi = _bf(x)
    r1 = x - hi.astype(F32)
    mid = _bf(r1)
    lo = _bf(r1 - mid.astype(F32))
    return hi, mid, lo


def _dot_sel_r(x, sel):
    hi, mid, lo = _split3(x)
    return _dot(hi, sel) + _dot(mid, sel) + _dot(lo, sel)


def _dot_sel_l(sel, x):
    hi, mid, lo = _split3(x)
    return _dot(sel, hi) + _dot(sel, mid) + _dot(sel, lo)


def _sigmoid(x):
    return 1.0 / (1.0 + jnp.exp(-x))


def _silu(x):
    return x * _sigmoid(x)


def _softplus(x):
    return jnp.maximum(x, 0.0) + jnp.log(1.0 + jnp.exp(-jnp.abs(x)))


def _iota2(shape, axis):
    return lax.broadcasted_iota(jnp.int32, shape, axis)


def _head_block_ones(n):
    r = _iota2((n, n), 0) >> 6
    c = _iota2((n, n), 1) >> 6
    return jnp.where(r == c, 1.0, 0.0).astype(BF16)


def _group_sum(x, ones_bd):
    return _dot_sel_r(x, ones_bd)


def _layer_norm_rows(y, g, b):
    mu = jnp.mean(y, axis=-1, keepdims=True)
    d = y - mu
    var = jnp.mean(d * d, axis=-1, keepdims=True)
    return d * lax.rsqrt(var + NORM_EPS) * g + b


def _params(*sem):
    return pltpu.CompilerParams(dimension_semantics=sem, vmem_limit_bytes=VMEM_LIMIT_BYTES)


def _inproj_kernel(x_ref, w_ref, o_ref):
    xb = _bf(x_ref[...])
    for c in range(C_TOTAL // 512):
        o_ref[:, c * 512:(c + 1) * 512] = _dot(xb, w_ref[:, c * 512:(c + 1) * 512])


def _inproj(x2d, w):
    n = x2d.shape[0]
    return pl.pallas_call(
        _inproj_kernel,
        grid=(n // ROW_TILE,),
        in_specs=[pl.BlockSpec((ROW_TILE, D_MODEL), lambda i: (i, 0)),
                  pl.BlockSpec((D_MODEL, C_TOTAL), lambda i: (0, 0))],
        out_specs=pl.BlockSpec((ROW_TILE, C_TOTAL), lambda i: (i, 0)),
        out_shape=jax.ShapeDtypeStruct((n, C_TOTAL), F32),
        compiler_params=_params("parallel"),
        name="inproj",
    )(x2d, w)


GDN_GROUP = 4


def _gdn_kernel(qkv_ref, a_ref, b_ref, z_ref, convw_ref, avec_ref, dtb_ref, ng_ref, o_ref,
                s_ref, u_ref, egl_ref, w_ref, qd_ref, kd_ref, qk_ref):
    seq = qkv_ref.shape[0]
    c = GDN_CHUNK
    w = GROUP_WIDTH
    rows_per_step = GDN_GROUP * c
    s_ref[...] = jnp.zeros_like(s_ref)

    same_head = (_iota2((w, w), 0) >> 6) == (_iota2((w, w), 1) >> 6)
    ones_bd = jnp.where(same_head, 1.0, 0.0).astype(BF16)
    rr = _iota2((rows_per_step, rows_per_step), 0)
    rc = _iota2((rows_per_step, rows_per_step), 1)
    same_chunk = (rr >> 6) == (rc >> 6)
    tril_chunks = jnp.where(same_chunk & (rc <= rr), 1.0, 0.0).astype(BF16)
    ones_chunks = jnp.where(same_chunk, 1.0, 0.0).astype(BF16)
    last_sel = jnp.where(rc == (rr | (c - 1)), 1.0, 0.0).astype(BF16)
    li = _iota2((rows_per_step, w), 0) & (c - 1)
    lj = _iota2((rows_per_step, w), 1) & (c - 1)
    causal = lj <= li
    strict = lj < li
    eye = lj == li
    lane_head = _iota2((c, w), 1) >> 6

    def stack_heads(t):
        return jnp.concatenate([jnp.where(lane_head == h, t, jnp.zeros_like(t)) for h in range(N_HEADS)], axis=0)

    def block_diag(t):
        return jnp.where(same_head, jnp.concatenate([t] * N_HEADS, axis=0), jnp.zeros((w, w), t.dtype))

    convw = convw_ref[...]
    avec = avec_ref[...]
    dtb = dtb_ref[...]
    ng = ng_ref[...]
    chunks = [slice(i * c, (i + 1) * c) for i in range(GDN_GROUP)]

    def prep(n, carry):
        r0 = pl.multiple_of(n * rows_per_step, rows_per_step)
        rows = pl.ds(r0, rows_per_step)
        cur = qkv_ref[rows, :]
        prev = qkv_ref[pl.ds(pl.multiple_of(jnp.maximum(r0 - SUBLANES, 0), SUBLANES), SUBLANES), :]
        prev = jnp.where(n > 0, prev, 0.0)
        win = jnp.concatenate([prev, cur], axis=0)
        y = cur * convw[GDN_CONV - 1:GDN_CONV, :]
        for j in range(1, GDN_CONV):
            y = y + pltpu.roll(win, j, 0)[SUBLANES:, :] * convw[GDN_CONV - 1 - j:GDN_CONV - j, :]
        y = _silu(y)
        q = y[:, 0:w]
        k = y[:, w:2 * w]
        v = y[:, 2 * w:3 * w]
        q = q * lax.rsqrt(_group_sum(q * q, ones_bd) + NORM_EPS) * (HEAD_DIM ** -0.5)
        k = k * lax.rsqrt(_group_sum(k * k, ones_bd) + NORM_EPS)
        beta = _sigmoid(b_ref[rows, :])
        gstep = avec * _softplus(a_ref[rows, :] + dtb)
        g = _dot_sel_l(tril_chunks, gstep)
        g_last = _dot_sel_l(last_sel, g)
        gr = _dot_sel_l(ones_chunks, jnp.where(eye, g, 0.0))
        decay = jnp.where(causal, jnp.exp(jnp.where(causal, g - gr, 0.0)), 0.0)
        eg = jnp.exp(g)
        kb = k * beta
        kbb = _bf(kb)
        qb = _bf(q)
        prods = [_dot_nt(jnp.concatenate([kbb[sl], qb[sl]], axis=0), _bf(stack_heads(k[sl]))) for sl in chunks]
        lower = jnp.where(strict, jnp.concatenate([p[0:c] for p in prods], axis=0) * decay, 0.0)
        qk = jnp.concatenate([p[c:2 * c] for p in prods], axis=0) * decay
        m = -lower
        t = jnp.where(eye, 1.0, 0.0) + m
        for level in range(6):
            mb = _bf(m)
            tb = _bf(t)
            new_m, t_m = [], []
            for sl in chunks:
                m_bd = block_diag(mb[sl])
                if level == 0:
                    new_m.append(_dot(mb[sl], m_bd))
                elif level < 5:
                    both = _dot(jnp.concatenate([mb[sl], tb[sl]], axis=0), m_bd)
                    new_m.append(both[0:c])
                    t_m.append(both[c:2 * c])
                else:
                    t_m.append(_dot(tb[sl], m_bd))
            if level < 5:
                m = jnp.concatenate(new_m, axis=0)
            if level > 0:
                t = t + jnp.concatenate(t_m, axis=0)
        tb = _bf(t)
        vb = v * beta
        kbg = kb * eg
        uw = [_dot(tb[sl], _bf(jnp.concatenate([stack_heads(vb[sl]), stack_heads(kbg[sl])], axis=1)))
              for sl in chunks]
        u_ref[rows, :] = jnp.concatenate([x[:, 0:w] for x in uw], axis=0)
        w_ref[rows, :] = _bf(jnp.concatenate([x[:, w:2 * w] for x in uw], axis=0))
        qd_ref[rows, :] = _bf(q * eg)
        kd_ref[rows, :] = _bf(k * jnp.exp(g_last - g))
        qk_ref[rows, :] = _bf(qk)
        egl_ref[rows, :] = jnp.exp(g_last)
        return carry

    lax.fori_loop(0, seq // rows_per_step, prep, 0)

    def scan(n, carry):
        r0 = pl.multiple_of(n * c, c)
        rows = pl.ds(r0, c)
        s_bd = s_ref[...]
        ws_qs = _dot(jnp.concatenate([w_ref[rows, :], qd_ref[rows, :]], axis=0), _bf(s_bd))
        v_new = u_ref[rows, :] - ws_qs[0:c]
        o_ref[rows, :] = ws_qs[c:2 * c] + _dot(qk_ref[rows, :], _bf(stack_heads(v_new)))
        upd = _dot_tn(kd_ref[rows, :], _bf(v_new))
        s_ref[...] = s_bd * egl_ref[pl.ds(r0, 1), :] + jnp.where(same_head, upd, 0.0)
        return carry

    lax.fori_loop(0, seq // c, scan, 0)

    def readout(n, carry):
        rows = pl.ds(pl.multiple_of(n * rows_per_step, rows_per_step), rows_per_step)
        o = o_ref[rows, :]
        ms = _group_sum(o * o, ones_bd) * (1.0 / HEAD_DIM)
        o_ref[rows, :] = o * lax.rsqrt(ms + NORM_EPS) * ng * _silu(z_ref[rows, :])
        return carry

    lax.fori_loop(0, seq // rows_per_step, readout, 0)


def _gdn(h, batch, seq, convw, avec, dtb, ng):
    blk = lambda width, cblk: pl.BlockSpec((seq, width), lambda b: (b, cblk))
    full = lambda a: pl.BlockSpec(a.shape, lambda b: (0, 0))
    return pl.pallas_call(
        _gdn_kernel,
        grid=(batch,),
        in_specs=[blk(768, C_GDN_QKV // 768), blk(256, C_GDN_A // 256), blk(256, C_GDN_B // 256),
                  blk(256, C_GDN_Z // 256), full(convw), full(avec), full(dtb), full(ng)],
        out_specs=pl.BlockSpec((seq, GROUP_WIDTH), lambda b: (b, 0)),
        out_shape=jax.ShapeDtypeStruct((batch * seq, GROUP_WIDTH), F32),
        scratch_shapes=[pltpu.VMEM((GROUP_WIDTH, GROUP_WIDTH), F32),
                        pltpu.VMEM((seq, GROUP_WIDTH), F32), pltpu.VMEM((seq, GROUP_WIDTH), F32),
                        pltpu.VMEM((seq, GROUP_WIDTH), BF16), pltpu.VMEM((seq, GROUP_WIDTH), BF16),
                        pltpu.VMEM((seq, GROUP_WIDTH), BF16), pltpu.VMEM((seq, GROUP_WIDTH), BF16)],
        compiler_params=_params("parallel"),
        name="gdn",
    )(h, h, h, h, convw, avec, dtb, ng)


def _mla_prep_kernel(cq_ref, ckv_ref, kr_ref, krs_ref, cos_ref, sin_ref, qg_ref, kvg_ref,
                     wq_ref, wqs_ref, wk_ref, wv_ref, q_ref, k_ref, v_ref):
    cq = cq_ref[...]
    nq = cq * lax.rsqrt(jnp.sum(cq * cq, axis=-1, keepdims=True) * (1.0 / MLA_Q_RANK) + NORM_EPS) * qg_ref[...]
    nqb = _bf(nq)
    cos1 = cos_ref[...]
    sin1 = sin_ref[...]
    cos4 = jnp.concatenate([cos1] * N_HEADS, axis=1)
    sin4 = jnp.concatenate([sin1] * N_HEADS, axis=1)
    scale = (MLA_NOPE + MLA_ROPE) ** -0.5
    q = (_dot(nqb, wq_ref[...]) * cos4 + _dot(nqb, wqs_ref[...]) * sin4) * scale
    q_ref[...] = _bf(q)
    ckv = ckv_ref[...]
    nkv = ckv * lax.rsqrt(jnp.mean(ckv * ckv, axis=-1, keepdims=True) + NORM_EPS) * kvg_ref[...]
    nkvb = _bf(nkv)
    kr = kr_ref[...] * cos1 + krs_ref[...] * sin1
    k = _dot(nkvb, wk_ref[...]) + jnp.concatenate([kr] * N_HEADS, axis=1)
    k_ref[...] = _bf(k)
    v_ref[...] = _bf(_dot(nkvb, wv_ref[...]))


def _mla_prep(h, seq, cos_t, sin_t, qg, kvg, wq, wqs, wk, wv):
    n = h.shape[0]
    tm = ROW_TILE
    pos_blocks = seq // tm
    blk = lambda width, cblk: pl.BlockSpec((tm, width), lambda i: (i, cblk))
    full = lambda a: pl.BlockSpec(a.shape, lambda i: (0, 0))
    tab = pl.BlockSpec((tm, LANES), lambda i: (i % pos_blocks, 0))
    return pl.pallas_call(
        _mla_prep_kernel,
        grid=(n // tm,),
        in_specs=[blk(256, C_MLA_CQ // 256), blk(128, C_MLA_CKV // 128), blk(128, C_MLA_KR // 128),
                  blk(128, C_MLA_KRS // 128), tab, tab, full(qg), full(kvg),
                  full(wq), full(wqs), full(wk), full(wv)],
        out_specs=[pl.BlockSpec((tm, 512), lambda i: (i, 0)), pl.BlockSpec((tm, 512), lambda i: (i, 0)),
                   pl.BlockSpec((tm, 256), lambda i: (i, 0))],
        out_shape=[jax.ShapeDtypeStruct((n, 512), BF16), jax.ShapeDtypeStruct((n, 512), BF16),
                   jax.ShapeDtypeStruct((n, 256), BF16)],
        compiler_params=_params("parallel"),
        name="mla_prep",
    )(h, h, h, h, cos_t, sin_t, qg, kvg, wq, wqs, wk, wv)


MLA_Q_BLOCK = 256


def _mla_attn_kernel(q_ref, k_ref, v_ref, o_ref):
    seq = q_ref.shape[0]
    tq = MLA_Q_BLOCK
    ri = _iota2((tq, tq), 0)
    ci = _iota2((tq, tq), 1)
    diag_ok = ci <= ri
    lane = _iota2((tq, LANES), 1)
    for qi in range(seq // tq):
        q0 = qi * tq
        v_off = v_ref[0:q0, :] if qi else None
        v_diag = v_ref[q0:q0 + tq, :]
        outs = []
        for hh in range(2):
            qh = q_ref[q0:q0 + tq, hh * LANES:(hh + 1) * LANES]
            s_diag = _dot_nt(qh, k_ref[q0:q0 + tq, hh * LANES:(hh + 1) * LANES])
            s_diag = jnp.where(diag_ok, s_diag, MASK_VALUE)
            m = jnp.max(s_diag, axis=-1, keepdims=True)
            if qi:
                s_off = _dot_nt(qh, k_ref[0:q0, hh * LANES:(hh + 1) * LANES])
                m = jnp.maximum(m, jnp.max(s_off, axis=-1, keepdims=True))
            p_diag = jnp.exp(s_diag - m)
            l = jnp.sum(p_diag, axis=-1, keepdims=True)
            acc = _dot(_bf(p_diag), v_diag)
            if qi:
                p_off = jnp.exp(s_off - m)
                l = l + jnp.sum(p_off, axis=-1, keepdims=True)
                acc = acc + _dot(_bf(p_off), v_off)
            outs.append(acc / l)
        o_ref[q0:q0 + tq, :] = jnp.where(lane < HEAD_DIM, outs[0], outs[1])


def _mla_attn(q, k, v, batch, seq):
    return pl.pallas_call(
        _mla_attn_kernel,
        grid=(batch, 2),
        in_specs=[pl.BlockSpec((seq, 256), lambda b, p: (b, p)),
                  pl.BlockSpec((seq, 256), lambda b, p: (b, p)),
                  pl.BlockSpec((seq, LANES), lambda b, p: (b, p))],
        out_specs=pl.BlockSpec((seq, LANES), lambda b, p: (b, p)),
        out_shape=jax.ShapeDtypeStruct((batch * seq, GROUP_WIDTH), F32),
        compiler_params=_params("parallel", "parallel"),
        name="mla_attn",
    )(q, k, v)


HGRN_ROWS = 128


def _hgrn_kernel(q_ref, f_ref, i_ref, g_ref, lb_ref, ng_ref, o_ref, st_ref):
    seq = q_ref.shape[0]
    w = GROUP_WIDTH
    rr = HGRN_ROWS
    cc = HGRN_CHUNK
    st_ref[...] = jnp.zeros_like(st_ref)
    row = _iota2((w, w), 0)
    col = _iota2((w, w), 1)
    same_head = (row >> 6) == (col >> 6)
    ones_bd = jnp.where(same_head, 1.0, 0.0).astype(BF16)
    tr = _iota2((rr, rr), 0)
    tc = _iota2((rr, rr), 1)
    tril_chunks = jnp.where(((tr >> 4) == (tc >> 4)) & (tc <= tr), 1.0, 0.0).astype(BF16)
    pos_in_chunk = _iota2((rr, w), 0) & (cc - 1)
    lb = lb_ref[...]
    ng = ng_ref[...]

    def body(n, carry):
        r0 = pl.multiple_of(n * rr, rr)
        q = q_ref[pl.ds(r0, rr), :]
        fl = f_ref[pl.ds(r0, rr), :]
        v = i_ref[pl.ds(r0, rr), :]
        z = jnp.exp(-jnp.abs(fl))
        r = 1.0 / (1.0 + z)
        zr = z * r
        pos = fl >= 0.0
        log_f = jnp.log(lb + (1.0 - lb) * jnp.where(pos, r, zr))
        kk = (1.0 - lb) * jnp.where(pos, zr, r)
        cum = _dot_sel_l(tril_chunks, log_f)
        xs = [_bf(q * kk)]
        for d in range(1, cc):
            diff = jnp.where(pos_in_chunk >= d, cum - pltpu.roll(cum, d, 0), MASK_VALUE)
            xs.append(_bf(q * pltpu.roll(kk, d, 0) * jnp.exp(diff)))
        sums = _dot(jnp.concatenate(xs, axis=0), ones_bd)
        o = sums[0:rr] * v
        for d in range(1, cc):
            o = o + sums[d * rr:(d + 1) * rr] * pltpu.roll(v, d, 0)
        q_dec = q * jnp.exp(cum)
        parts = []
        for j in range(rr // cc):
            sl = slice(j * cc, (j + 1) * cc)
            st = st_ref[...]
            parts.append(_dot_nt(_bf(q_dec[sl]), _bf(st)))
            cum_last = cum[(j + 1) * cc - 1:(j + 1) * cc, :]
            k_dec = kk[sl] * jnp.exp(cum_last - cum[sl])
            upd = _dot_tn(_bf(v[sl]), _bf(k_dec))
            st_ref[...] = st * jnp.exp(cum_last) + jnp.where(same_head, upd, 0.0)
        o = o + jnp.concatenate(parts, axis=0)
        ms = _group_sum(o * o, ones_bd) * (1.0 / HEAD_DIM)
        o_ref[pl.ds(r0, rr), :] = o * lax.rsqrt(ms + NORM_EPS) * ng * _silu(g_ref[pl.ds(r0, rr), :])
        return carry

    lax.fori_loop(0, seq // rr, body, 0)


def _hgrn(h, batch, seq, lb, ng):
    blk = lambda j: pl.BlockSpec((seq, GROUP_WIDTH), lambda b: (b, C_HGRN // GROUP_WIDTH + j))
    full = lambda a: pl.BlockSpec(a.shape, lambda b: (0, 0))
    return pl.pallas_call(
        _hgrn_kernel,
        grid=(batch,),
        in_specs=[blk(0), blk(1), blk(2), blk(3), full(lb), full(ng)],
        out_specs=pl.BlockSpec((seq, GROUP_WIDTH), lambda b: (b, 0)),
        out_shape=jax.ShapeDtypeStruct((batch * seq, GROUP_WIDTH), F32),
        scratch_shapes=[pltpu.VMEM((GROUP_WIDTH, GROUP_WIDTH), F32)],
        compiler_params=_params("parallel"),
        name="hgrn",
    )(h, h, h, h, lb, ng)


def _dsa_kernel(q_ref, k_ref, v_ref, o_ref, num_ref, m_ref, l_ref):
    seq = q_ref.shape[0]
    blk = DSA_BLOCK
    pair = pl.program_id(1)
    qi = _iota2((blk, 2 * blk), 0)
    ki = _iota2((blk, 2 * blk), 1)
    steps = blk + qi - ki
    lane = _iota2((blk, LANES), 1)
    first = lane < HEAD_DIM
    qscale = HEAD_DIM ** -0.5 * LOG2E
    slopes = [jnp.where(pair == 0, ALIBI_SLOPES[hh], ALIBI_SLOPES[2 + hh]) * LOG2E for hh in range(2)]

    for bi, (window, dil) in enumerate(DSA_BRANCHES):
        nblk = seq // (dil * blk)
        in_window = (steps >= 0) & (steps <= window // dil)
        dist = (steps * dil).astype(F32)
        two_blocks = nblk > 1
        if two_blocks:
            bias = [jnp.where(in_window, -slopes[hh] * dist, MASK_VALUE) for hh in range(2)]
            bias_first = [jnp.where(ki >= blk, bias[hh], MASK_VALUE) for hh in range(2)]
        else:
            bias_first = [jnp.where(in_window, -slopes[hh] * dist, MASK_VALUE)[:, blk:] for hh in range(2)]
            bias = bias_first

        def attend_group(blocks, bi=bi, dil=dil, bias=bias, bias_first=bias_first, two_blocks=two_blocks):
            if dil == 1:
                ld = lambda ref, s0: ref[pl.ds(s0, blk), :]
                dst = lambda s0: pl.ds(bi * seq + s0, blk)
            else:
                ld = lambda ref, s0: ref[pl.ds(s0, blk, stride=dil), :]
                dst = lambda s0: pl.ds(bi * seq + s0, blk, stride=dil)
            tiles, vbs = [], []
            for start, is_first in blocks:
                qb = ld(q_ref, start) * qscale
                if two_blocks:
                    prev = start if is_first else start - dil * blk
                    kb = _bf(jnp.concatenate([ld(k_ref, prev), ld(k_ref, start)], axis=0))
                    vbs.append(_bf(jnp.concatenate([ld(v_ref, prev), ld(v_ref, start)], axis=0)))
                else:
                    kb = _bf(ld(k_ref, start))
                    vbs.append(_bf(ld(v_ref, start)))
                for hh in range(2):
                    qm = _bf(jnp.where(first if hh == 0 else jnp.logical_not(first), qb, 0.0))
                    tiles.append(_dot_nt(qm, kb) + (bias_first if is_first else bias)[hh])
            s = jnp.concatenate(tiles, axis=0)
            m = jnp.max(s, axis=-1, keepdims=True)
            p = jnp.exp2(s - m)
            l = jnp.sum(p, axis=-1, keepdims=True)
            pb = _bf(p)
            for i, (start, _) in enumerate(blocks):
                r0, r1, r2 = 2 * i * blk, (2 * i + 1) * blk, (2 * i + 2) * blk
                num_ref[dst(start), :] = jnp.where(first, _dot(pb[r0:r1], vbs[i]), _dot(pb[r1:r2], vbs[i]))
                m_ref[dst(start), :] = jnp.where(first, m[r0:r1], m[r1:r2])
                l_ref[dst(start), :] = jnp.where(first, l[r0:r1], l[r1:r2])

        group = 4
        if nblk == 1:
            def body(g, carry, attend_group=attend_group):
                attend_group([(g * group + j, True) for j in range(group)])
                return carry
            lax.fori_loop(0, dil // group, body, 0)
        elif nblk == group:
            def body(r, carry, attend_group=attend_group, dil=dil):
                attend_group([(r + dil * blk * n, n == 0) for n in range(group)])
                return carry
            lax.fori_loop(0, dil, body, 0)
        else:
            assert dil == 1 and nblk % group == 0
            attend_group([(n * blk, n == 0) for n in range(group)])

            def body(g, carry, attend_group=attend_group):
                attend_group([(pl.multiple_of((g * group + j) * blk, blk), False) for j in range(group)])
                return carry
            lax.fori_loop(1, nblk // group, body, 0)

    def merge(n, carry):
        r0 = pl.multiple_of(n * blk, blk)
        rows = [pl.ds(bi * seq + r0, blk) for bi in range(len(DSA_BRANCHES))]
        m0, m1, m2 = [m_ref[r, :] for r in rows]
        mm = jnp.maximum(jnp.maximum(m0, m1), m2)
        w0, w1, w2 = jnp.exp2(m0 - mm), jnp.exp2(m1 - mm), jnp.exp2(m2 - mm)
        num = w0 * num_ref[rows[0], :] + w1 * num_ref[rows[1], :] + w2 * num_ref[rows[2], :]
        den = w0 * l_ref[rows[0], :] + w1 * l_ref[rows[1], :] + w2 * l_ref[rows[2], :]
        o_ref[pl.ds(r0, blk), :] = num / den
        return carry

    lax.fori_loop(0, seq // blk, merge, 0)


def _dsa(h, batch, seq):
    base = C_DSA // LANES
    blk = lambda j: pl.BlockSpec((seq, LANES), lambda b, p: (b, base + 2 * j + p))
    nb = len(DSA_BRANCHES)
    return pl.pallas_call(
        _dsa_kernel,
        grid=(batch, 2),
        in_specs=[blk(0), blk(1), blk(2)],
        out_specs=pl.BlockSpec((seq, LANES), lambda b, p: (b, p)),
        out_shape=jax.ShapeDtypeStruct((batch * seq, GROUP_WIDTH), F32),
        scratch_shapes=[pltpu.VMEM((nb * seq, LANES), F32), pltpu.VMEM((nb * seq, LANES), F32),
                        pltpu.VMEM((nb * seq, LANES), F32)],
        compiler_params=_params("parallel", "parallel"),
        name="dsa",
    )(h, h, h)


def _outproj_kernel(oa_ref, ob_ref, oc_ref, od_ref, x_ref, w_ref, g_ref, b_ref, y_ref):
    gw = GROUP_WIDTH
    mixed = _dot(_bf(oa_ref[...]), w_ref[0:gw, :])
    mixed = mixed + _dot(_bf(ob_ref[...]), w_ref[gw:2 * gw, :])
    mixed = mixed + _dot(_bf(oc_ref[...]), w_ref[2 * gw:3 * gw, :])
    mixed = mixed + _dot(_bf(od_ref[...]), w_ref[3 * gw:4 * gw, :])
    y_ref[...] = _layer_norm_rows(DEEPNORM_ALPHA * x_ref[...] + mixed, g_ref[...], b_ref[...])


def _outproj_ln(oa, ob, oc, od, x2d, w, g, b):
    n = x2d.shape[0]
    tm = ROW_TILE
    grp = pl.BlockSpec((tm, GROUP_WIDTH), lambda i: (i, 0))
    full = lambda a: pl.BlockSpec(a.shape, lambda i: (0, 0))
    row = pl.BlockSpec((tm, D_MODEL), lambda i: (i, 0))
    return pl.pallas_call(
        _outproj_kernel,
        grid=(n // tm,),
        in_specs=[grp, grp, grp, grp, row, full(w), full(g), full(b)],
        out_specs=row,
        out_shape=jax.ShapeDtypeStruct((n, D_MODEL), F32),
        compiler_params=_params("parallel"),
        name="outproj_ln",
    )(oa, ob, oc, od, x2d, w, g, b)


FF_CHUNK = 1024


def _ffn_kernel(x_ref, w1_ref, w2_ref, g_ref, b_ref, y_ref):
    x = x_ref[...]
    xb = _bf(x)
    acc = jnp.zeros(x.shape, F32)
    for c in range(D_FF // FF_CHUNK):
        hmid = _dot(xb, w1_ref[:, c * FF_CHUNK:(c + 1) * FF_CHUNK])
        hmid = jnp.square(jnp.maximum(hmid, 0.0))
        acc = acc + _dot(_bf(hmid), w2_ref[c * FF_CHUNK:(c + 1) * FF_CHUNK, :])
    y_ref[...] = _layer_norm_rows(DEEPNORM_ALPHA * x + acc, g_ref[...], b_ref[...])


def _ffn_ln(x2d, w1, w2, g, b):
    n = x2d.shape[0]
    tm = ROW_TILE
    row = pl.BlockSpec((tm, D_MODEL), lambda i: (i, 0))
    resident = lambda a: pl.BlockSpec(a.shape, lambda i: (0, 0), pipeline_mode=pl.Buffered(1))
    full = lambda a: pl.BlockSpec(a.shape, lambda i: (0, 0))
    return pl.pallas_call(
        _ffn_kernel,
        grid=(n // tm,),
        in_specs=[row, resident(w1), resident(w2), full(g), full(b)],
        out_specs=row,
        out_shape=jax.ShapeDtypeStruct((n, D_MODEL), F32),
        compiler_params=_params("parallel"),
        name="ffn_ln",
    )(x2d, w1, w2, g, b)


def _expand_heads(wcols):
    return jnp.repeat(wcols, HEAD_DIM, axis=1)


def _arrange_w_in(w):
    pts = np.cumsum(IN_SIZES)[:-1].tolist()
    (a_qkv, a_a, a_b, a_z, b_cq, b_ckv, b_kr, c_q, c_f, c_i, c_g, d_qkv) = jnp.split(w, pts, axis=1)
    d = w.shape[0]
    z = lambda n: jnp.zeros((d, n), w.dtype)
    half = MLA_ROPE // 2
    kr_sw = jnp.concatenate([b_kr[:, half:], b_kr[:, :half]], axis=1)
    cols = [a_qkv, _expand_heads(a_a), _expand_heads(a_b), a_z,
            b_cq, z(256 - MLA_Q_RANK), b_ckv,
            z(MLA_NOPE), b_kr, z(LANES - MLA_NOPE - MLA_ROPE),
            z(MLA_NOPE), kr_sw, z(LANES - MLA_NOPE - MLA_ROPE), z(C_HGRN - C_MLA_KRS - LANES),
            c_q, c_f, c_i, c_g, d_qkv]
    out = jnp.concatenate(cols, axis=1)
    out = jnp.concatenate([out, z(C_TOTAL - out.shape[1])], axis=1)
    return _bf(out)


def _arrange_mla_weights(w_uq, w_ukv):
    rq = w_uq.shape[0]
    wq = w_uq.reshape(rq, N_HEADS, MLA_NOPE + MLA_ROPE)
    half = MLA_ROPE // 2
    zq = jnp.zeros((rq, N_HEADS, LANES - MLA_NOPE - MLA_ROPE), w_uq.dtype)
    q_main = jnp.concatenate([wq, zq], axis=2)
    rope = wq[:, :, MLA_NOPE:]
    rope_sw = jnp.concatenate([rope[:, :, half:], rope[:, :, :half]], axis=2)
    q_swap = jnp.concatenate([jnp.zeros((rq, N_HEADS, MLA_NOPE), w_uq.dtype), rope_sw, zq], axis=2)
    pad_rows = lambda m: jnp.concatenate([m, jnp.zeros((256 - rq, m.shape[1]), m.dtype)], axis=0)
    q_main = pad_rows(q_main.reshape(rq, N_HEADS * LANES))
    q_swap = pad_rows(q_swap.reshape(rq, N_HEADS * LANES))
    rkv = w_ukv.shape[0]
    wkv = w_ukv.reshape(rkv, N_HEADS, MLA_NOPE + HEAD_DIM)
    k_w = jnp.concatenate([wkv[:, :, :MLA_NOPE], jnp.zeros((rkv, N_HEADS, LANES - MLA_NOPE), w_ukv.dtype)], axis=2)
    v_w = wkv[:, :, MLA_NOPE:]
    return _bf(q_main), _bf(q_swap), _bf(k_w.reshape(rkv, N_HEADS * LANES)), _bf(v_w.reshape(rkv, N_HEADS * HEAD_DIM))


def _rope_tables(seq):
    half = MLA_ROPE // 2
    pos = jnp.arange(seq, dtype=F32)
    inv_freq = ROPE_BASE ** (-jnp.arange(half, dtype=F32) / half)
    ang = pos[:, None] * inv_freq[None, :]
    cos, sin = jnp.cos(ang), jnp.sin(ang)
    ones = jnp.ones((seq, MLA_NOPE), F32)
    zeros = jnp.zeros((seq, MLA_NOPE), F32)
    tail1 = jnp.ones((seq, LANES - MLA_NOPE - MLA_ROPE), F32)
    tail0 = jnp.zeros((seq, LANES - MLA_NOPE - MLA_ROPE), F32)
    cos_t = jnp.concatenate([ones, cos, cos, tail1], axis=1)
    sin_t = jnp.concatenate([zeros, -sin, sin, tail0], axis=1)
    return cos_t, sin_t


def _tile_heads(vec):
    return jnp.tile(vec.astype(F32), N_HEADS)[None, :]


def kernel(x, w_in, gdn_conv_w, gdn_a_log, gdn_dt_bias, gdn_norm_g, mla_q_norm_g, mla_kv_norm_g,
           mla_w_uq, mla_w_ukv, hgrn_lb_logits, hgrn_norm_g, w_out, ln1_g, ln1_b, w_ff1, w_ff2,
           ln2_g, ln2_b):
    batch, seq, d_model = x.shape
    assert d_model == D_MODEL and seq % (16 * DSA_BLOCK) == 0 and (batch * seq) % ROW_TILE == 0
    x2d = x.reshape(batch * seq, d_model)
    p_lb = jax.nn.softmax(hgrn_lb_logits.astype(F32), axis=0)
    lower_bounds = jnp.cumsum(p_lb, axis=0) - p_lb[:1]
    cos_t, sin_t = _rope_tables(seq)
    for l in range(DEPTH):
        h = _inproj(x2d, _arrange_w_in(w_in[l]))
        avec = jnp.repeat(-jnp.exp(gdn_a_log[l].astype(F32)), HEAD_DIM)[None, :]
        dtb = jnp.repeat(gdn_dt_bias[l].astype(F32), HEAD_DIM)[None, :]
        o_a = _gdn(h, batch, seq, gdn_conv_w[l].astype(F32), avec, dtb, _tile_heads(gdn_norm_g[l]))
        wq, wqs, wk, wv = _arrange_mla_weights(mla_w_uq[l], mla_w_ukv[l])
        qg = jnp.concatenate([mla_q_norm_g[l].astype(F32), jnp.zeros((256 - MLA_Q_RANK,), F32)])[None, :]
        q_m, k_m, v_m = _mla_prep(h, seq, cos_t, sin_t, qg, mla_kv_norm_g[l].astype(F32)[None, :],
                                  wq, wqs, wk, wv)
        o_b = _mla_attn(q_m, k_m, v_m, batch, seq)
        o_c = _hgrn(h, batch, seq, lower_bounds[l][None, :], _tile_heads(hgrn_norm_g[l]))
        o_d = _dsa(h, batch, seq)
        x2d = _outproj_ln(o_a, o_b, o_c, o_d, x2d, _bf(w_out[l]), ln1_g[l][None, :], ln1_b[l][None, :])
        x2d = _ffn_ln(x2d, _bf(w_ff1[l]), _bf(w_ff2[l]), ln2_g[l][None, :], ln2_b[l][None, :])
    return x2d.reshape(batch, seq, d_model)
```

```python
import functools
import math

import numpy as np
import jax
import jax.numpy as jnp
from jax import lax
from jax.experimental import pallas as pl
from jax.experimental.pallas import tpu as pltpu

F32 = jnp.float32
BF16 = jnp.bfloat16

D_MODEL = 1024
DEPTH = 2
GROUP_WIDTH = 256
N_HEADS = 4
HEAD_DIM = 64
GDN_CONV = 4
GDN_CHUNK = 64
MLA_NOPE = 64
MLA_ROPE = 32
MLA_Q_RANK = 192
MLA_KV_RANK = 128
ROPE_BASE = 10000.0
HGRN_CHUNK = 16
DSA_BRANCHES = ((128, 1), (512, 4), (2048, 16))
DSA_BLOCK = 128
DSA_GROUP = 8
ALIBI_SLOPES = tuple(2.0 ** (-8.0 * (j + 1) / N_HEADS) for j in range(N_HEADS))
D_FF = 4 * D_MODEL
DEEPNORM_ALPHA = (2 * DEPTH) ** 0.25
NORM_EPS = 1e-6
MASK_VALUE = -1e30
LOG2E = 1.4426950408889634

LANES = 128
SUBLANES = 8
VMEM_LIMIT_BYTES = 56 * 1024 * 1024

C_GDN_QKV = 0
C_GDN_A = 768
C_GDN_B = 1024
C_GDN_Z = 1280
C_MLA_CQ = 1536
C_MLA_CKV = 1792
C_MLA_KR = 1920
C_MLA_KRS = 2048
C_HGRN = 2304
C_DSA = 3328
C_TOTAL = 4096
IN_SIZES = (768, 4, 4, 256, 192, 128, 32, 256, 256, 256, 256, 768)
ROW_TILE = 512


def _bf(x):
    return x.astype(BF16)


def _dot(a, b):
    return jnp.dot(a, b, preferred_element_type=F32)


def _dot_nt(a, b):
    return lax.dot_general(a, b, (((1,), (1,)), ((), ())), preferred_element_type=F32)


def _dot_tn(a, b):
    return lax.dot_general(a, b, (((0,), (0,)), ((), ())), preferred_element_type=F32)


def _split3(x):
    hi = _bf(x)
    r1 = x - hi.astype(F32)
    mid = _bf(r1)
    lo = _bf(r1 - mid.astype(F32))
    return hi, mid, lo


def _dot_sel_r(x, sel):
    hi, mid, lo = _split3(x)
    return _dot(hi, sel) + _dot(mid, sel) + _dot(lo, sel)


def _dot_sel_l(sel, x):
    hi, mid, lo = _split3(x)
    return _dot(sel, hi) + _dot(sel, mid) + _dot(sel, lo)


def _sigmoid(x):
    return 0.5 + 0.5 * jnp.tanh(0.5 * x)


def _silu(x):
    return x * _sigmoid(x)


def _softplus(x):
    return jnp.maximum(x, 0.0) + jnp.log(1.0 + jnp.exp(-jnp.abs(x)))


def _iota2(shape, axis):
    return lax.broadcasted_iota(jnp.int32, shape, axis)


def _head_block_ones(n):
    r = _iota2((n, n), 0) >> 6
    c = _iota2((n, n), 1) >> 6
    return jnp.where(r == c, 1.0, 0.0).astype(BF16)


def _group_sum(x, ones_bd):
    return _dot(_bf(x), ones_bd)


def _layer_norm_rows(y, g, b):
    mu = jnp.mean(y, axis=-1, keepdims=True)
    d = y - mu
    var = jnp.mean(d * d, axis=-1, keepdims=True)
    return d * lax.rsqrt(var + NORM_EPS) * g + b


def _params(*sem):
    return pltpu.CompilerParams(dimension_semantics=sem, vmem_limit_bytes=VMEM_LIMIT_BYTES)


def _inproj_kernel(x_ref, w_ref, o_ref):
    xb = _bf(x_ref[...])
    for c in range(C_TOTAL // 512):
        o_ref[:, c * 512:(c + 1) * 512] = _dot(xb, w_ref[:, c * 512:(c + 1) * 512])


def _inproj(x2d, w):
    n = x2d.shape[0]
    return pl.pallas_call(
        _inproj_kernel,
        grid=(n // ROW_TILE,),
        in_specs=[pl.BlockSpec((ROW_TILE, D_MODEL), lambda i: (i, 0)),
                  pl.BlockSpec((D_MODEL, C_TOTAL), lambda i: (0, 0))],
        out_specs=pl.BlockSpec((ROW_TILE, C_TOTAL), lambda i: (i, 0)),
        out_shape=jax.ShapeDtypeStruct((n, C_TOTAL), F32),
        compiler_params=_params("parallel"),
        name="inproj",
    )(x2d, w)


GDN_GROUP = 4


def _gdn_kernel(qkv_ref, a_ref, b_ref, z_ref, convw_ref, avec_ref, dtb_ref, ng_ref, o_ref,
                s_ref, u_ref, egl_ref, w_ref, qd_ref, kd_ref, qk_ref):
    seq = qkv_ref.shape[0]
    c = GDN_CHUNK
    w = GROUP_WIDTH
    rows_per_step = GDN_GROUP * c
    s_ref[...] = jnp.zeros_like(s_ref)

    same_head = (_iota2((w, w), 0) >> 6) == (_iota2((w, w), 1) >> 6)
    ones_bd = jnp.where(same_head, 1.0, 0.0).astype(BF16)
    rr = _iota2((rows_per_step, rows_per_step), 0)
    rc = _iota2((rows_per_step, rows_per_step), 1)
    same_chunk = (rr >> 6) == (rc >> 6)
    tril_chunks = jnp.where(same_chunk & (rc <= rr), 1.0, 0.0).astype(BF16)
    ones_chunks = jnp.where(same_chunk, 1.0, 0.0).astype(BF16)
    li = _iota2((rows_per_step, w), 0) & (c - 1)
    lj = _iota2((rows_per_step, w), 1) & (c - 1)
    causal = lj <= li
    strict = lj < li
    eye = lj == li
    lane_head = _iota2((c, w), 1) >> 6
    first_rows = _iota2((SUBLANES, 3 * w), 0)

    def stack_heads(t):
        return jnp.concatenate([jnp.where(lane_head == h, t, jnp.zeros_like(t)) for h in range(N_HEADS)], axis=0)

    def block_diag(t):
        return jnp.where(same_head, jnp.concatenate([t] * N_HEADS, axis=0), jnp.zeros((w, w), t.dtype))

    convw = convw_ref[...]
    avec = avec_ref[...]
    dtb = dtb_ref[...]
    ng = ng_ref[...]
    chunks = [slice(i * c, (i + 1) * c) for i in range(GDN_GROUP)]

    def prep(n, carry):
        r0 = pl.multiple_of(n * rows_per_step, rows_per_step)
        rows = pl.ds(r0, rows_per_step)
        cur = qkv_ref[rows, :]
        prev = qkv_ref[pl.ds(pl.multiple_of(jnp.maximum(r0 - SUBLANES, 0), SUBLANES), SUBLANES), :]
        prev = jnp.where(n > 0, prev, 0.0)
        y = cur * convw[GDN_CONV - 1:GDN_CONV, :]
        for j in range(1, GDN_CONV):
            shifted = pltpu.roll(cur, j, 0)
            head = jnp.where(first_rows < j, pltpu.roll(prev, j, 0), shifted[0:SUBLANES, :])
            shifted = jnp.concatenate([head, shifted[SUBLANES:, :]], axis=0)
            y = y + shifted * convw[GDN_CONV - 1 - j:GDN_CONV - j, :]
        y = _silu(y)
        q = y[:, 0:w]
        k = y[:, w:2 * w]
        v = y[:, 2 * w:3 * w]
        q = q * lax.rsqrt(_group_sum(q * q, ones_bd) + NORM_EPS) * (HEAD_DIM ** -0.5)
        k = k * lax.rsqrt(_group_sum(k * k, ones_bd) + NORM_EPS)
        beta = _sigmoid(b_ref[rows, :])
        gstep = avec * _softplus(a_ref[rows, :] + dtb)
        g = _dot_sel_l(tril_chunks, gstep)
        g_last = jnp.concatenate([jnp.broadcast_to(g[sl][c - 1:c, :], (c, w)) for sl in chunks], axis=0)
        gr = _dot_sel_l(ones_chunks, jnp.where(eye, g, 0.0))
        decay = jnp.where(causal, jnp.exp(jnp.where(causal, g - gr, 0.0)), 0.0)
        eg = jnp.exp(g)
        kb = k * beta
        kbb = _bf(kb)
        qb = _bf(q)
        prods = [_dot_nt(jnp.concatenate([kbb[sl], qb[sl]], axis=0), _bf(stack_heads(k[sl]))) for sl in chunks]
        lower = jnp.where(strict, jnp.concatenate([p[0:c] for p in prods], axis=0) * decay, 0.0)
        qk = jnp.concatenate([p[c:2 * c] for p in prods], axis=0) * decay
        m = -lower
        t = jnp.where(eye, 1.0, 0.0) + m
        for level in range(6):
            mb = _bf(m)
            tb = _bf(t)
            new_m, t_m = [], []
            for sl in chunks:
                m_bd = block_diag(mb[sl])
                if level == 0:
                    new_m.append(_dot(mb[sl], m_bd))
                elif level < 5:
                    both = _dot(jnp.concatenate([mb[sl], tb[sl]], axis=0), m_bd)
                    new_m.append(both[0:c])
                    t_m.append(both[c:2 * c])
                else:
                    t_m.append(_dot(tb[sl], m_bd))
            if level < 5:
                m = jnp.concatenate(new_m, axis=0)
            if level > 0:
                t = t + jnp.concatenate(t_m, axis=0)
        tb = _bf(t)
        vb = v * beta
        kbg = kb * eg
        uw = [_dot(tb[sl], _bf(jnp.concatenate([stack_heads(vb[sl]), stack_heads(kbg[sl])], axis=1)))
              for sl in chunks]
        u_ref[rows, :] = jnp.concatenate([x[:, 0:w] for x in uw], axis=0)
        w_ref[rows, :] = _bf(jnp.concatenate([x[:, w:2 * w] for x in uw], axis=0))
        qd_ref[rows, :] = _bf(q * eg)
        kd_ref[rows, :] = _bf(jnp.transpose(k * jnp.exp(g_last - g)))
        qk_ref[rows, :] = _bf(qk)
        egl_ref[rows, :] = jnp.exp(g_last)
        return carry

    lax.fori_loop(0, seq // rows_per_step, prep, 0)

    zero_rows = jnp.zeros((c, w), BF16)

    def scan(n, carry):
        step0 = pl.multiple_of(n * rows_per_step, rows_per_step)
        kd_t = kd_ref[pl.ds(step0, rows_per_step), :]
        for ci in range(GDN_GROUP):
            rows = pl.ds(step0 + ci * c, c)
            s_bd = s_ref[...]
            ws_qs = _dot(jnp.concatenate([w_ref[rows, :], qd_ref[rows, :]], axis=0), _bf(s_bd))
            v_new = u_ref[rows, :] - ws_qs[0:c]
            o_ref[rows, :] = ws_qs[c:2 * c] + _dot(qk_ref[rows, :], _bf(stack_heads(v_new)))
            v_rows = jnp.concatenate([zero_rows] * ci + [_bf(v_new)] + [zero_rows] * (GDN_GROUP - 1 - ci), axis=0)
            upd = _dot(kd_t, v_rows)
            s_ref[...] = s_bd * egl_ref[pl.ds(step0 + ci * c, 1), :] + jnp.where(same_head, upd, 0.0)
        return carry

    lax.fori_loop(0, seq // rows_per_step, scan, 0)

    def readout(n, carry):
        rows = pl.ds(pl.multiple_of(n * rows_per_step, rows_per_step), rows_per_step)
        o = o_ref[rows, :]
        ms = _group_sum(o * o, ones_bd) * (1.0 / HEAD_DIM)
        o_ref[rows, :] = o * lax.rsqrt(ms + NORM_EPS) * ng * _silu(z_ref[rows, :])
        return carry

    lax.fori_loop(0, seq // rows_per_step, readout, 0)


def _gdn(h, batch, seq, convw, avec, dtb, ng):
    blk = lambda width, cblk: pl.BlockSpec((seq, width), lambda b: (b, cblk))
    full = lambda a: pl.BlockSpec(a.shape, lambda b: (0, 0))
    return pl.pallas_call(
        _gdn_kernel,
        grid=(batch,),
        in_specs=[blk(768, C_GDN_QKV // 768), blk(256, C_GDN_A // 256), blk(256, C_GDN_B // 256),
                  blk(256, C_GDN_Z // 256), full(convw), full(avec), full(dtb), full(ng)],
        out_specs=pl.BlockSpec((seq, GROUP_WIDTH), lambda b: (b, 0)),
        out_shape=jax.ShapeDtypeStruct((batch * seq, GROUP_WIDTH), F32),
        scratch_shapes=[pltpu.VMEM((GROUP_WIDTH, GROUP_WIDTH), F32),
                        pltpu.VMEM((seq, GROUP_WIDTH), F32), pltpu.VMEM((seq, GROUP_WIDTH), F32),
                        pltpu.VMEM((seq, GROUP_WIDTH), BF16), pltpu.VMEM((seq, GROUP_WIDTH), BF16),
                        pltpu.VMEM((seq, GROUP_WIDTH), BF16), pltpu.VMEM((seq, GROUP_WIDTH), BF16)],
        compiler_params=_params("parallel"),
        name="gdn",
    )(h, h, h, h, convw, avec, dtb, ng)


def _mla_prep_kernel(cq_ref, ckv_ref, kr_ref, krs_ref, cos_ref, sin_ref, qg_ref, kvg_ref,
                     wq_ref, wqs_ref, wk_ref, wv_ref, q_ref, k_ref, v_ref):
    cq = cq_ref[...]
    nq = cq * lax.rsqrt(jnp.sum(cq * cq, axis=-1, keepdims=True) * (1.0 / MLA_Q_RANK) + NORM_EPS) * qg_ref[...]
    nqb = _bf(nq)
    cos1 = cos_ref[...]
    sin1 = sin_ref[...]
    cos4 = jnp.concatenate([cos1] * N_HEADS, axis=1)
    sin4 = jnp.concatenate([sin1] * N_HEADS, axis=1)
    scale = (MLA_NOPE + MLA_ROPE) ** -0.5 * LOG2E
    q = (_dot(nqb, wq_ref[...]) * cos4 + _dot(nqb, wqs_ref[...]) * sin4) * scale
    q_ref[...] = _bf(q)
    ckv = ckv_ref[...]
    nkv = ckv * lax.rsqrt(jnp.mean(ckv * ckv, axis=-1, keepdims=True) + NORM_EPS) * kvg_ref[...]
    nkvb = _bf(nkv)
    kr = kr_ref[...] * cos1 + krs_ref[...] * sin1
    k = _dot(nkvb, wk_ref[...]) + jnp.concatenate([kr] * N_HEADS, axis=1)
    k_ref[...] = _bf(k)
    v = _bf(_dot(nkvb, wv_ref[...]))
    ones = jnp.ones((v.shape[0], LANES), BF16)
    v_ref[...] = jnp.concatenate([v[:, 0:LANES], ones, v[:, LANES:2 * LANES], ones], axis=1)


def _mla_prep(h, seq, cos_t, sin_t, qg, kvg, wq, wqs, wk, wv):
    n = h.shape[0]
    tm = ROW_TILE
    pos_blocks = seq // tm
    blk = lambda width, cblk: pl.BlockSpec((tm, width), lambda i: (i, cblk))
    full = lambda a: pl.BlockSpec(a.shape, lambda i: (0, 0))
    tab = pl.BlockSpec((tm, LANES), lambda i: (i % pos_blocks, 0))
    return pl.pallas_call(
        _mla_prep_kernel,
        grid=(n // tm,),
        in_specs=[blk(256, C_MLA_CQ // 256), blk(128, C_MLA_CKV // 128), blk(128, C_MLA_KR // 128),
                  blk(128, C_MLA_KRS // 128), tab, tab, full(qg), full(kvg),
                  full(wq), full(wqs), full(wk), full(wv)],
        out_specs=[pl.BlockSpec((tm, 512), lambda i: (i, 0)), pl.BlockSpec((tm, 512), lambda i: (i, 0)),
                   pl.BlockSpec((tm, 512), lambda i: (i, 0))],
        out_shape=[jax.ShapeDtypeStruct((n, 512), BF16), jax.ShapeDtypeStruct((n, 512), BF16),
                   jax.ShapeDtypeStruct((n, 512), BF16)],
        compiler_params=_params("parallel"),
        name="mla_prep",
    )(h, h, h, h, cos_t, sin_t, qg, kvg, wq, wqs, wk, wv)


MLA_Q_BLOCK = 256


def _mla_attn_kernel(q_ref, k_ref, v_ref, o_ref):
    seq = q_ref.shape[0]
    tq = MLA_Q_BLOCK
    ri = _iota2((tq, tq), 0)
    ci = _iota2((tq, tq), 1)
    diag_ok = ci <= ri
    lane = _iota2((tq, LANES), 1)
    for qi in range(seq // tq):
        q0 = qi * tq
        v_off = v_ref[0:q0, :] if qi else None
        v_diag = v_ref[q0:q0 + tq, :]
        outs = []
        for hh in range(2):
            qh = q_ref[q0:q0 + tq, hh * LANES:(hh + 1) * LANES]
            s_diag = _dot_nt(qh, k_ref[q0:q0 + tq, hh * LANES:(hh + 1) * LANES])
            s_diag = jnp.where(diag_ok, s_diag, MASK_VALUE)
            m = jnp.max(s_diag, axis=-1, keepdims=True)
            if qi:
                s_off = _dot_nt(qh, k_ref[0:q0, hh * LANES:(hh + 1) * LANES])
                m = jnp.maximum(m, jnp.max(s_off, axis=-1, keepdims=True))
            acc = _dot(_bf(jnp.exp2(s_diag - m)), v_diag)
            if qi:
                acc = acc + _dot(_bf(jnp.exp2(s_off - m)), v_off)
            outs.append(acc[:, 0:LANES] / acc[:, LANES:2 * LANES])
        o_ref[q0:q0 + tq, :] = jnp.where(lane < HEAD_DIM, outs[0], outs[1])


def _mla_attn(q, k, v, batch, seq):
    return pl.pallas_call(
        _mla_attn_kernel,
        grid=(batch, 2),
        in_specs=[pl.BlockSpec((seq, 256), lambda b, p: (b, p)),
                  pl.BlockSpec((seq, 256), lambda b, p: (b, p)),
                  pl.BlockSpec((seq, 256), lambda b, p: (b, p))],
        out_specs=pl.BlockSpec((seq, LANES), lambda b, p: (b, p)),
        out_shape=jax.ShapeDtypeStruct((batch * seq, GROUP_WIDTH), F32),
        compiler_params=_params("parallel", "parallel"),
        name="mla_attn",
    )(q, k, v)


HGRN_ROWS = 128


def _hgrn_kernel(q_ref, f_ref, i_ref, g_ref, lb_ref, ng_ref, o_ref, st_ref):
    seq = q_ref.shape[0]
    w = GROUP_WIDTH
    rr = HGRN_ROWS
    cc = HGRN_CHUNK
    st_ref[...] = jnp.zeros_like(st_ref)
    row = _iota2((w, w), 0)
    col = _iota2((w, w), 1)
    same_head = (row >> 6) == (col >> 6)
    ones_bd = jnp.where(same_head, 1.0, 0.0).astype(BF16)
    tr = _iota2((rr, rr), 0)
    tc = _iota2((rr, rr), 1)
    tril_chunks = jnp.where(((tr >> 4) == (tc >> 4)) & (tc <= tr), 1.0, 0.0).astype(BF16)
    pos_in_chunk = _iota2((rr, w), 0) & (cc - 1)
    lb = lb_ref[...]
    ng = ng_ref[...]

    def body(n, carry):
        r0 = pl.multiple_of(n * rr, rr)
        q = q_ref[pl.ds(r0, rr), :]
        fl = f_ref[pl.ds(r0, rr), :]
        v = i_ref[pl.ds(r0, rr), :]
        z = jnp.exp(-jnp.abs(fl))
        r = 1.0 / (1.0 + z)
        zr = z * r
        pos = fl >= 0.0
        log_f = jnp.log(lb + (1.0 - lb) * jnp.where(pos, r, zr))
        kk = (1.0 - lb) * jnp.where(pos, zr, r)
        cum = _dot_sel_l(tril_chunks, log_f)
        xs = [_bf(q * kk)]
        for d in range(1, cc):
            diff = jnp.where(pos_in_chunk >= d, cum - pltpu.roll(cum, d, 0), MASK_VALUE)
            xs.append(_bf(q * pltpu.roll(kk, d, 0) * jnp.exp(diff)))
        sums = _dot(jnp.concatenate(xs, axis=0), ones_bd)
        o = sums[0:rr] * v
        for d in range(1, cc):
            o = o + sums[d * rr:(d + 1) * rr] * pltpu.roll(v, d, 0)
        q_dec = q * jnp.exp(cum)
        parts = []
        for j in range(rr // cc):
            sl = slice(j * cc, (j + 1) * cc)
            st = st_ref[...]
            parts.append(_dot_nt(_bf(q_dec[sl]), _bf(st)))
            cum_last = cum[(j + 1) * cc - 1:(j + 1) * cc, :]
            k_dec = kk[sl] * jnp.exp(cum_last - cum[sl])
            upd = _dot_tn(_bf(v[sl]), _bf(k_dec))
            st_ref[...] = st * jnp.exp(cum_last) + jnp.where(same_head, upd, 0.0)
        o = o + jnp.concatenate(parts, axis=0)
        ms = _group_sum(o * o, ones_bd) * (1.0 / HEAD_DIM)
        o_ref[pl.ds(r0, rr), :] = o * lax.rsqrt(ms + NORM_EPS) * ng * _silu(g_ref[pl.ds(r0, rr), :])
        return carry

    lax.fori_loop(0, seq // rr, body, 0)


def _hgrn(h, batch, seq, lb, ng):
    blk = lambda j: pl.BlockSpec((seq, GROUP_WIDTH), lambda b: (b, C_HGRN // GROUP_WIDTH + j))
    full = lambda a: pl.BlockSpec(a.shape, lambda b: (0, 0))
    return pl.pallas_call(
        _hgrn_kernel,
        grid=(batch,),
        in_specs=[blk(0), blk(1), blk(2), blk(3), full(lb), full(ng)],
        out_specs=pl.BlockSpec((seq, GROUP_WIDTH), lambda b: (b, 0)),
        out_shape=jax.ShapeDtypeStruct((batch * seq, GROUP_WIDTH), F32),
        scratch_shapes=[pltpu.VMEM((GROUP_WIDTH, GROUP_WIDTH), F32)],
        compiler_params=_params("parallel"),
        name="hgrn",
    )(h, h, h, h, lb, ng)


def _dsa_kernel(q_ref, k_ref, v_ref, o_ref, num_ref, m_ref, l_ref):
    seq = q_ref.shape[0]
    blk = DSA_BLOCK
    pair = pl.program_id(1)
    qi = _iota2((blk, 2 * blk), 0)
    ki = _iota2((blk, 2 * blk), 1)
    steps = blk + qi - ki
    lane = _iota2((blk, LANES), 1)
    first = lane < HEAD_DIM
    qscale = HEAD_DIM ** -0.5 * LOG2E
    slopes = [jnp.where(pair == 0, ALIBI_SLOPES[hh], ALIBI_SLOPES[2 + hh]) * LOG2E for hh in range(2)]

    for bi, (window, dil) in enumerate(DSA_BRANCHES):
        nblk = seq // (dil * blk)
        in_window = (steps >= 0) & (steps <= window // dil)
        dist = (steps * dil).astype(F32)
        two_blocks = nblk > 1
        if two_blocks:
            bias = [jnp.where(in_window, -slopes[hh] * dist, MASK_VALUE) for hh in range(2)]
            bias_first = [jnp.where(ki >= blk, bias[hh], MASK_VALUE) for hh in range(2)]
        else:
            bias_first = [jnp.where(in_window, -slopes[hh] * dist, MASK_VALUE)[:, blk:] for hh in range(2)]
            bias = bias_first

        def attend_group(blocks, bi=bi, dil=dil, bias=bias, bias_first=bias_first, two_blocks=two_blocks):
            if dil == 1:
                ld = lambda ref, s0: ref[pl.ds(s0, blk), :]
                dst = lambda s0: pl.ds(bi * seq + s0, blk)
            else:
                ld = lambda ref, s0: ref[pl.ds(s0, blk, stride=dil), :]
                dst = lambda s0: pl.ds(bi * seq + s0, blk, stride=dil)
            tiles, vbs = [], []
            for start, is_first in blocks:
                qb = ld(q_ref, start) * qscale
                if two_blocks:
                    prev = start if is_first else start - dil * blk
                    kb = _bf(jnp.concatenate([ld(k_ref, prev), ld(k_ref, start)], axis=0))
                    vb = _bf(jnp.concatenate([ld(v_ref, prev), ld(v_ref, start)], axis=0))
                else:
                    kb = _bf(ld(k_ref, start))
                    vb = _bf(ld(v_ref, start))
                vbs.append(jnp.concatenate([vb, jnp.ones(vb.shape, BF16)], axis=1))
                for hh in range(2):
                    qm = _bf(jnp.where(first if hh == 0 else jnp.logical_not(first), qb, 0.0))
                    tiles.append(_dot_nt(qm, kb) + (bias_first if is_first else bias)[hh])
            s = jnp.concatenate(tiles, axis=0)
            m = jnp.max(s, axis=-1, keepdims=True)
            pb = _bf(jnp.exp2(s - m))
            for i, (start, _) in enumerate(blocks):
                r0, r1, r2 = 2 * i * blk, (2 * i + 1) * blk, (2 * i + 2) * blk
                pv = jnp.where(jnp.concatenate([first, first], axis=1), _dot(pb[r0:r1], vbs[i]), _dot(pb[r1:r2], vbs[i]))
                num_ref[dst(start), :] = pv[:, 0:LANES]
                l_ref[dst(start), :] = pv[:, LANES:2 * LANES]
                m_ref[dst(start), :] = jnp.where(first, m[r0:r1], m[r1:r2])

        group = DSA_GROUP
        if nblk == 1:
            def body(g, carry, attend_group=attend_group):
                attend_group([(g * group + j, True) for j in range(group)])
                return carry
            lax.fori_loop(0, dil // group, body, 0)
        elif nblk < group:
            per = group // nblk

            def body(g, carry, attend_group=attend_group, dil=dil, nblk=nblk, per=per):
                attend_group([(g * per + j + dil * blk * n, n == 0) for j in range(per) for n in range(nblk)])
                return carry
            lax.fori_loop(0, dil // per, body, 0)
        else:
            assert dil == 1 and nblk % group == 0
            attend_group([(n * blk, n == 0) for n in range(group)])

            def body(g, carry, attend_group=attend_group):
                attend_group([(pl.multiple_of((g * group + j) * blk, blk), False) for j in range(group)])
                return carry
            lax.fori_loop(1, nblk // group, body, 0)

    def merge(n, carry):
        r0 = pl.multiple_of(n * blk, blk)
        rows = [pl.ds(bi * seq + r0, blk) for bi in range(len(DSA_BRANCHES))]
        m0, m1, m2 = [m_ref[r, :] for r in rows]
        mm = jnp.maximum(jnp.maximum(m0, m1), m2)
        w0, w1, w2 = jnp.exp2(m0 - mm), jnp.exp2(m1 - mm), jnp.exp2(m2 - mm)
        num = w0 * num_ref[rows[0], :] + w1 * num_ref[rows[1], :] + w2 * num_ref[rows[2], :]
        den = w0 * l_ref[rows[0], :] + w1 * l_ref[rows[1], :] + w2 * l_ref[rows[2], :]
        o_ref[pl.ds(r0, blk), :] = num / den
        return carry

    lax.fori_loop(0, seq // blk, merge, 0)


def _dsa(h, batch, seq):
    base = C_DSA // LANES
    blk = lambda j: pl.BlockSpec((seq, LANES), lambda b, p: (b, base + 2 * j + p))
    nb = len(DSA_BRANCHES)
    return pl.pallas_call(
        _dsa_kernel,
        grid=(batch, 2),
        in_specs=[blk(0), blk(1), blk(2)],
        out_specs=pl.BlockSpec((seq, LANES), lambda b, p: (b, p)),
        out_shape=jax.ShapeDtypeStruct((batch * seq, GROUP_WIDTH), F32),
        scratch_shapes=[pltpu.VMEM((nb * seq, LANES), F32), pltpu.VMEM((nb * seq, LANES), F32),
                        pltpu.VMEM((nb * seq, LANES), F32)],
        compiler_params=_params("parallel", "parallel"),
        name="dsa",
    )(h, h, h)


FF_CHUNK = 1024


def _post_kernel(oa_ref, ob_ref, oc_ref, od_ref, x_ref, wo_ref, g1_ref, b1_ref, w1_ref, w2_ref, g2_ref, b2_ref,
                 y_ref):
    gw = GROUP_WIDTH
    mixed = _dot(_bf(oa_ref[...]), wo_ref[0:gw, :])
    mixed = mixed + _dot(_bf(ob_ref[...]), wo_ref[gw:2 * gw, :])
    mixed = mixed + _dot(_bf(oc_ref[...]), wo_ref[2 * gw:3 * gw, :])
    mixed = mixed + _dot(_bf(od_ref[...]), wo_ref[3 * gw:4 * gw, :])
    x = _layer_norm_rows(DEEPNORM_ALPHA * x_ref[...] + mixed, g1_ref[...], b1_ref[...])
    xb = _bf(x)
    acc = jnp.zeros(x.shape, F32)
    for c in range(D_FF // FF_CHUNK):
        hmid = _dot(xb, w1_ref[:, c * FF_CHUNK:(c + 1) * FF_CHUNK])
        hmid = jnp.square(jnp.maximum(hmid, 0.0))
        acc = acc + _dot(_bf(hmid), w2_ref[c * FF_CHUNK:(c + 1) * FF_CHUNK, :])
    y_ref[...] = _layer_norm_rows(DEEPNORM_ALPHA * x + acc, g2_ref[...], b2_ref[...])


def _post(oa, ob, oc, od, x2d, wo, g1, b1, w1, w2, g2, b2):
    n = x2d.shape[0]
    tm = ROW_TILE
    grp = pl.BlockSpec((tm, GROUP_WIDTH), lambda i: (i, 0))
    row = pl.BlockSpec((tm, D_MODEL), lambda i: (i, 0))
    resident = lambda a: pl.BlockSpec(a.shape, lambda i: (0, 0), pipeline_mode=pl.Buffered(1))
    return pl.pallas_call(
        _post_kernel,
        grid=(n // tm,),
        in_specs=[grp, grp, grp, grp, row, resident(wo), resident(g1), resident(b1),
                  resident(w1), resident(w2), resident(g2), resident(b2)],
        out_specs=row,
        out_shape=jax.ShapeDtypeStruct((n, D_MODEL), F32),
        compiler_params=_params("parallel"),
        name="post",
    )(oa, ob, oc, od, x2d, wo, g1, b1, w1, w2, g2, b2)


def _expand_heads(wcols):
    return jnp.repeat(wcols, HEAD_DIM, axis=-1)


def _arrange_w_in(w):
    pts = np.cumsum(IN_SIZES)[:-1].tolist()
    (a_qkv, a_a, a_b, a_z, b_cq, b_ckv, b_kr, c_q, c_f, c_i, c_g, d_qkv) = jnp.split(_bf(w), pts, axis=-1)
    z = lambda n: jnp.zeros(w.shape[:-1] + (n,), BF16)
    half = MLA_ROPE // 2
    kr_sw = jnp.concatenate([b_kr[..., half:], b_kr[..., :half]], axis=-1)
    cols = [a_qkv, _expand_heads(a_a), _expand_heads(a_b), a_z,
            b_cq, z(256 - MLA_Q_RANK), b_ckv,
            z(MLA_NOPE), b_kr, z(LANES - MLA_NOPE - MLA_ROPE),
            z(MLA_NOPE), kr_sw, z(LANES - MLA_NOPE - MLA_ROPE), z(C_HGRN - C_MLA_KRS - LANES),
            c_q, c_f, c_i, c_g, d_qkv]
    return jnp.concatenate(cols, axis=-1)


def _arrange_mla_weights(w_uq, w_ukv):
    lead = w_uq.shape[:-2]
    rq = w_uq.shape[-2]
    wq = _bf(w_uq).reshape(lead + (rq, N_HEADS, MLA_NOPE + MLA_ROPE))
    half = MLA_ROPE // 2
    zq = jnp.zeros(lead + (rq, N_HEADS, LANES - MLA_NOPE - MLA_ROPE), BF16)
    q_main = jnp.concatenate([wq, zq], axis=-1)
    rope = wq[..., MLA_NOPE:]
    rope_sw = jnp.concatenate([rope[..., half:], rope[..., :half]], axis=-1)
    q_swap = jnp.concatenate([jnp.zeros(lead + (rq, N_HEADS, MLA_NOPE), BF16), rope_sw, zq], axis=-1)
    pad_rows = lambda m: jnp.concatenate([m, jnp.zeros(lead + (256 - rq, m.shape[-1]), BF16)], axis=-2)
    q_main = pad_rows(q_main.reshape(lead + (rq, N_HEADS * LANES)))
    q_swap = pad_rows(q_swap.reshape(lead + (rq, N_HEADS * LANES)))
    rkv = w_ukv.shape[-2]
    wkv = _bf(w_ukv).reshape(lead + (rkv, N_HEADS, MLA_NOPE + HEAD_DIM))
    k_w = jnp.concatenate([wkv[..., :MLA_NOPE], jnp.zeros(lead + (rkv, N_HEADS, LANES - MLA_NOPE), BF16)], axis=-1)
    v_w = wkv[..., MLA_NOPE:]
    return (q_main, q_swap, k_w.reshape(lead + (rkv, N_HEADS * LANES)),
            v_w.reshape(lead + (rkv, N_HEADS * HEAD_DIM)))


def _rope_tables(seq):
    half = MLA_ROPE // 2
    pos = jnp.arange(seq, dtype=F32)
    inv_freq = ROPE_BASE ** (-jnp.arange(half, dtype=F32) / half)
    ang = pos[:, None] * inv_freq[None, :]
    cos, sin = jnp.cos(ang), jnp.sin(ang)
    ones = jnp.ones((seq, MLA_NOPE), F32)
    zeros = jnp.zeros((seq, MLA_NOPE), F32)
    tail1 = jnp.ones((seq, LANES - MLA_NOPE - MLA_ROPE), F32)
    tail0 = jnp.zeros((seq, LANES - MLA_NOPE - MLA_ROPE), F32)
    cos_t = jnp.concatenate([ones, cos, cos, tail1], axis=1)
    sin_t = jnp.concatenate([zeros, -sin, sin, tail0], axis=1)
    return cos_t, sin_t


def _tile_heads(vec):
    return jnp.tile(vec.astype(F32), N_HEADS)[..., None, :]


def kernel(x, w_in, gdn_conv_w, gdn_a_log, gdn_dt_bias, gdn_norm_g, mla_q_norm_g, mla_kv_norm_g,
           mla_w_uq, mla_w_ukv, hgrn_lb_logits, hgrn_norm_g, w_out, ln1_g, ln1_b, w_ff1, w_ff2,
           ln2_g, ln2_b):
    batch, seq, d_model = x.shape
    assert d_model == D_MODEL and seq % (16 * DSA_BLOCK) == 0 and (batch * seq) % ROW_TILE == 0
    x2d = x.reshape(batch * seq, d_model)
    p_lb = jax.nn.softmax(hgrn_lb_logits.astype(F32), axis=0)
    lower_bounds = (jnp.cumsum(p_lb, axis=0) - p_lb[:1])[:, None, :]
    cos_t, sin_t = _rope_tables(seq)
    w_in_a = _arrange_w_in(w_in)
    avec = jnp.repeat(-jnp.exp(gdn_a_log.astype(F32)), HEAD_DIM, axis=-1)[:, None, :]
    dtb = jnp.repeat(gdn_dt_bias.astype(F32), HEAD_DIM, axis=-1)[:, None, :]
    conv_w = gdn_conv_w.astype(F32)
    gdn_g = _tile_heads(gdn_norm_g)
    hgrn_g = _tile_heads(hgrn_norm_g)
    wq, wqs, wk, wv = _arrange_mla_weights(mla_w_uq, mla_w_ukv)
    qg = jnp.concatenate([mla_q_norm_g.astype(F32), jnp.zeros((DEPTH, 256 - MLA_Q_RANK), F32)], axis=-1)[:, None, :]
    kvg = mla_kv_norm_g.astype(F32)[:, None, :]
    wo, w1, w2 = _bf(w_out), _bf(w_ff1), _bf(w_ff2)
    g1, b1, g2, b2 = (t.astype(F32)[:, None, :] for t in (ln1_g, ln1_b, ln2_g, ln2_b))
    for l in range(DEPTH):
        h = _inproj(x2d, w_in_a[l])
        o_a = _gdn(h, batch, seq, conv_w[l], avec[l], dtb[l], gdn_g[l])
        q_m, k_m, v_m = _mla_prep(h, seq, cos_t, sin_t, qg[l], kvg[l], wq[l], wqs[l], wk[l], wv[l])
        o_b = _mla_attn(q_m, k_m, v_m, batch, seq)
        o_c = _hgrn(h, batch, seq, lower_bounds[l], hgrn_g[l])
        o_d = _dsa(h, batch, seq)
        x2d = _post(o_a, o_b, o_c, o_d, x2d, wo[l], g1[l], b1[l], w1[l], w2[l], g2[l], b2[l])
    return x2d.reshape(batch, seq, d_model)
```

```python
import functools
import math

import numpy as np
import jax
import jax.numpy as jnp
from jax import lax
from jax.experimental import pallas as pl
from jax.experimental.pallas import tpu as pltpu

F32 = jnp.float32
BF16 = jnp.bfloat16

D_MODEL = 1024
DEPTH = 2
GROUP_WIDTH = 256
N_HEADS = 4
HEAD_DIM = 64
GDN_CONV = 4
GDN_CHUNK = 64
MLA_NOPE = 64
MLA_ROPE = 32
MLA_Q_RANK = 192
MLA_KV_RANK = 128
ROPE_BASE = 10000.0
HGRN_CHUNK = 16
DSA_BRANCHES = ((128, 1), (512, 4), (2048, 16))
DSA_BLOCK = 128
DSA_GROUP = 8
ALIBI_SLOPES = tuple(2.0 ** (-8.0 * (j + 1) / N_HEADS) for j in range(N_HEADS))
D_FF = 4 * D_MODEL
DEEPNORM_ALPHA = (2 * DEPTH) ** 0.25
NORM_EPS = 1e-6
MASK_VALUE = -1e30
LOG2E = 1.4426950408889634

LANES = 128
SUBLANES = 8
VMEM_LIMIT_BYTES = 56 * 1024 * 1024

C_GDN_QKV = 0
C_GDN_A = 768
C_GDN_B = 1024
C_GDN_Z = 1280
C_MLA_CQ = 1536
C_MLA_CKV = 1792
C_MLA_KR = 1920
C_MLA_KRS = 2048
C_HGRN = 2304
C_DSA = 3328
C_TOTAL = 4096
IN_SIZES = (768, 4, 4, 256, 192, 128, 32, 256, 256, 256, 256, 768)
ROW_TILE = 512


def _bf(x):
    return x.astype(BF16)


def _dot(a, b):
    return jnp.dot(a, b, preferred_element_type=F32)


def _dot_nt(a, b):
    return lax.dot_general(a, b, (((1,), (1,)), ((), ())), preferred_element_type=F32)


def _dot_tn(a, b):
    return lax.dot_general(a, b, (((0,), (0,)), ((), ())), preferred_element_type=F32)


def _split3(x):
    hi = _bf(x)
    r1 = x - hi.astype(F32)
    mid = _bf(r1)
    lo = _bf(r1 - mid.astype(F32))
    return hi, mid, lo


def _dot_sel_r(x, sel):
    hi, mid, lo = _split3(x)
    return _dot(hi, sel) + _dot(mid, sel) + _dot(lo, sel)


def _dot_sel_l(sel, x):
    hi, mid, lo = _split3(x)
    return _dot(sel, hi) + _dot(sel, mid) + _dot(sel, lo)


def _sigmoid(x):
    return 0.5 + 0.5 * jnp.tanh(0.5 * x)


def _silu(x):
    return x * _sigmoid(x)


def _softplus(x):
    return jnp.maximum(x, 0.0) + jnp.log(1.0 + jnp.exp(-jnp.abs(x)))


def _iota2(shape, axis):
    return lax.broadcasted_iota(jnp.int32, shape, axis)


def _head_block_ones(n):
    r = _iota2((n, n), 0) >> 6
    c = _iota2((n, n), 1) >> 6
    return jnp.where(r == c, 1.0, 0.0).astype(BF16)


def _group_sum(x, ones_bd):
    return _dot(_bf(x), ones_bd)


def _layer_norm_rows(y, g, b):
    mu = jnp.mean(y, axis=-1, keepdims=True)
    d = y - mu
    var = jnp.mean(d * d, axis=-1, keepdims=True)
    return d * lax.rsqrt(var + NORM_EPS) * g + b


def _params(*sem):
    return pltpu.CompilerParams(dimension_semantics=sem, vmem_limit_bytes=VMEM_LIMIT_BYTES)


def _inproj_kernel(x_ref, w_ref, o_ref):
    xb = _bf(x_ref[...])
    for c in range(C_TOTAL // 512):
        o_ref[:, c * 512:(c + 1) * 512] = _dot(xb, w_ref[:, c * 512:(c + 1) * 512])


def _inproj(x2d, w):
    n = x2d.shape[0]
    return pl.pallas_call(
        _inproj_kernel,
        grid=(n // ROW_TILE,),
        in_specs=[pl.BlockSpec((ROW_TILE, D_MODEL), lambda i: (i, 0)),
                  pl.BlockSpec((D_MODEL, C_TOTAL), lambda i: (0, 0))],
        out_specs=pl.BlockSpec((ROW_TILE, C_TOTAL), lambda i: (i, 0)),
        out_shape=jax.ShapeDtypeStruct((n, C_TOTAL), F32),
        compiler_params=_params("parallel"),
        name="inproj",
    )(x2d, w)


GDN_GROUP = 4


def _gdn_kernel(qkv_ref, a_ref, b_ref, z_ref, convw_ref, avec_ref, dtb_ref, ng_ref, o_ref,
                s_ref, u_ref, egl_ref, w_ref, qd_ref, kd_ref, qk_ref):
    seq = qkv_ref.shape[0]
    c = GDN_CHUNK
    w = GROUP_WIDTH
    rows_per_step = GDN_GROUP * c
    s_ref[...] = jnp.zeros_like(s_ref)

    same_head = (_iota2((w, w), 0) >> 6) == (_iota2((w, w), 1) >> 6)
    ones_bd = jnp.where(same_head, 1.0, 0.0).astype(BF16)
    rr = _iota2((rows_per_step, rows_per_step), 0)
    rc = _iota2((rows_per_step, rows_per_step), 1)
    same_chunk = (rr >> 6) == (rc >> 6)
    tril_chunks = jnp.where(same_chunk & (rc <= rr), 1.0, 0.0).astype(BF16)
    ones_chunks = jnp.where(same_chunk, 1.0, 0.0).astype(BF16)
    li = _iota2((rows_per_step, w), 0) & (c - 1)
    lj = _iota2((rows_per_step, w), 1) & (c - 1)
    causal = lj <= li
    strict = lj < li
    eye = lj == li
    lane_head = _iota2((c, w), 1) >> 6
    first_rows = _iota2((SUBLANES, 3 * w), 0)

    def stack_heads(t):
        return jnp.concatenate([jnp.where(lane_head == h, t, jnp.zeros_like(t)) for h in range(N_HEADS)], axis=0)

    def block_diag(t):
        return jnp.where(same_head, jnp.concatenate([t] * N_HEADS, axis=0), jnp.zeros((w, w), t.dtype))

    convw = convw_ref[...]
    avec = avec_ref[...]
    dtb = dtb_ref[...]
    ng = ng_ref[...]
    chunks = [slice(i * c, (i + 1) * c) for i in range(GDN_GROUP)]

    def prep(n, carry):
        r0 = pl.multiple_of(n * rows_per_step, rows_per_step)
        rows = pl.ds(r0, rows_per_step)
        cur = qkv_ref[rows, :]
        prev = qkv_ref[pl.ds(pl.multiple_of(jnp.maximum(r0 - SUBLANES, 0), SUBLANES), SUBLANES), :]
        prev = jnp.where(n > 0, prev, 0.0)
        y = cur * convw[GDN_CONV - 1:GDN_CONV, :]
        for j in range(1, GDN_CONV):
            shifted = pltpu.roll(cur, j, 0)
            head = jnp.where(first_rows < j, pltpu.roll(prev, j, 0), shifted[0:SUBLANES, :])
            shifted = jnp.concatenate([head, shifted[SUBLANES:, :]], axis=0)
            y = y + shifted * convw[GDN_CONV - 1 - j:GDN_CONV - j, :]
        y = _silu(y)
        q = y[:, 0:w]
        k = y[:, w:2 * w]
        v = y[:, 2 * w:3 * w]
        q = q * lax.rsqrt(_group_sum(q * q, ones_bd) + NORM_EPS) * (HEAD_DIM ** -0.5)
        k = k * lax.rsqrt(_group_sum(k * k, ones_bd) + NORM_EPS)
        beta = _sigmoid(b_ref[rows, :])
        gstep = avec * _softplus(a_ref[rows, :] + dtb)
        g = _dot_sel_l(tril_chunks, gstep)
        g_last = jnp.concatenate([jnp.broadcast_to(g[sl][c - 1:c, :], (c, w)) for sl in chunks], axis=0)
        gr = _dot_sel_l(ones_chunks, jnp.where(eye, g, 0.0))
        decay = jnp.where(causal, jnp.exp(jnp.where(causal, g - gr, 0.0)), 0.0)
        eg = jnp.exp(g)
        kb = k * beta
        kbb = _bf(kb)
        qb = _bf(q)
        prods = [_dot_nt(jnp.concatenate([kbb[sl], qb[sl]], axis=0), _bf(stack_heads(k[sl]))) for sl in chunks]
        lower = jnp.where(strict, jnp.concatenate([p[0:c] for p in prods], axis=0) * decay, 0.0)
        qk = jnp.concatenate([p[c:2 * c] for p in prods], axis=0) * decay
        m = -lower
        t = jnp.where(eye, 1.0, 0.0) + m
        for level in range(6):
            mb = _bf(m)
            tb = _bf(t)
            new_m, t_m = [], []
            for sl in chunks:
                m_bd = block_diag(mb[sl])
                if level == 0:
                    new_m.append(_dot(mb[sl], m_bd))
                elif level < 5:
                    both = _dot(jnp.concatenate([mb[sl], tb[sl]], axis=0), m_bd)
                    new_m.append(both[0:c])
                    t_m.append(both[c:2 * c])
                else:
                    t_m.append(_dot(tb[sl], m_bd))
            if level < 5:
                m = jnp.concatenate(new_m, axis=0)
            if level > 0:
                t = t + jnp.concatenate(t_m, axis=0)
        tb = _bf(t)
        vb = v * beta
        kbg = kb * eg
        uw = [_dot(tb[sl], _bf(jnp.concatenate([stack_heads(vb[sl]), stack_heads(kbg[sl])], axis=1)))
              for sl in chunks]
        u_ref[rows, :] = jnp.concatenate([x[:, 0:w] for x in uw], axis=0)
        w_ref[rows, :] = _bf(jnp.concatenate([x[:, w:2 * w] for x in uw], axis=0))
        qd_ref[rows, :] = _bf(q * eg)
        kd_ref[rows, :] = _bf(jnp.transpose(k * jnp.exp(g_last - g)))
        qk_ref[rows, :] = _bf(qk)
        egl_ref[rows, :] = jnp.exp(g_last)
        return carry

    lax.fori_loop(0, seq // rows_per_step, prep, 0)

    zero_rows = jnp.zeros((c, w), BF16)

    def scan(n, carry):
        step0 = pl.multiple_of(n * rows_per_step, rows_per_step)
        kd_t = kd_ref[pl.ds(step0, rows_per_step), :]
        for ci in range(GDN_GROUP):
            rows = pl.ds(step0 + ci * c, c)
            s_bd = s_ref[...]
            ws_qs = _dot(jnp.concatenate([w_ref[rows, :], qd_ref[rows, :]], axis=0), _bf(s_bd))
            v_new = u_ref[rows, :] - ws_qs[0:c]
            o_ref[rows, :] = ws_qs[c:2 * c] + _dot(qk_ref[rows, :], _bf(stack_heads(v_new)))
            v_rows = jnp.concatenate([zero_rows] * ci + [_bf(v_new)] + [zero_rows] * (GDN_GROUP - 1 - ci), axis=0)
            upd = _dot(kd_t, v_rows)
            s_ref[...] = s_bd * egl_ref[pl.ds(step0 + ci * c, 1), :] + jnp.where(same_head, upd, 0.0)
        return carry

    lax.fori_loop(0, seq // rows_per_step, scan, 0)

    def readout(n, carry):
        rows = pl.ds(pl.multiple_of(n * rows_per_step, rows_per_step), rows_per_step)
        o = o_ref[rows, :]
        ms = _group_sum(o * o, ones_bd) * (1.0 / HEAD_DIM)
        o_ref[rows, :] = o * lax.rsqrt(ms + NORM_EPS) * ng * _silu(z_ref[rows, :])
        return carry

    lax.fori_loop(0, seq // rows_per_step, readout, 0)


def _gdn(h, batch, seq, convw, avec, dtb, ng):
    blk = lambda width, cblk: pl.BlockSpec((seq, width), lambda b: (b, cblk))
    full = lambda a: pl.BlockSpec(a.shape, lambda b: (0, 0))
    return pl.pallas_call(
        _gdn_kernel,
        grid=(batch,),
        in_specs=[blk(768, C_GDN_QKV // 768), blk(256, C_GDN_A // 256), blk(256, C_GDN_B // 256),
                  blk(256, C_GDN_Z // 256), full(convw), full(avec), full(dtb), full(ng)],
        out_specs=pl.BlockSpec((seq, GROUP_WIDTH), lambda b: (b, 0)),
        out_shape=jax.ShapeDtypeStruct((batch * seq, GROUP_WIDTH), F32),
        scratch_shapes=[pltpu.VMEM((GROUP_WIDTH, GROUP_WIDTH), F32),
                        pltpu.VMEM((seq, GROUP_WIDTH), F32), pltpu.VMEM((seq, GROUP_WIDTH), F32),
                        pltpu.VMEM((seq, GROUP_WIDTH), BF16), pltpu.VMEM((seq, GROUP_WIDTH), BF16),
                        pltpu.VMEM((seq, GROUP_WIDTH), BF16), pltpu.VMEM((seq, GROUP_WIDTH), BF16)],
        compiler_params=_params("parallel"),
        name="gdn",
    )(h, h, h, h, convw, avec, dtb, ng)


def _mla_prep_kernel(cq_ref, ckv_ref, kr_ref, krs_ref, cos_ref, sin_ref, qg_ref, kvg_ref,
                     wq_ref, wqs_ref, wk_ref, wv_ref, q_ref, k_ref, v_ref):
    cq = cq_ref[...]
    nq = cq * lax.rsqrt(jnp.sum(cq * cq, axis=-1, keepdims=True) * (1.0 / MLA_Q_RANK) + NORM_EPS) * qg_ref[...]
    nqb = _bf(nq)
    cos1 = cos_ref[...]
    sin1 = sin_ref[...]
    cos4 = jnp.concatenate([cos1] * N_HEADS, axis=1)
    sin4 = jnp.concatenate([sin1] * N_HEADS, axis=1)
    scale = (MLA_NOPE + MLA_ROPE) ** -0.5 * LOG2E
    q = (_dot(nqb, wq_ref[...]) * cos4 + _dot(nqb, wqs_ref[...]) * sin4) * scale
    q_ref[...] = _bf(q)
    ckv = ckv_ref[...]
    nkv = ckv * lax.rsqrt(jnp.mean(ckv * ckv, axis=-1, keepdims=True) + NORM_EPS) * kvg_ref[...]
    nkvb = _bf(nkv)
    kr = kr_ref[...] * cos1 + krs_ref[...] * sin1
    k = _dot(nkvb, wk_ref[...]) + jnp.concatenate([kr] * N_HEADS, axis=1)
    k_ref[...] = _bf(k)
    v = _bf(_dot(nkvb, wv_ref[...]))
    ones = jnp.ones((v.shape[0], LANES), BF16)
    v_ref[...] = jnp.concatenate([v[:, 0:LANES], ones, v[:, LANES:2 * LANES], ones], axis=1)


def _mla_prep(h, seq, cos_t, sin_t, qg, kvg, wq, wqs, wk, wv):
    n = h.shape[0]
    tm = ROW_TILE
    pos_blocks = seq // tm
    blk = lambda width, cblk: pl.BlockSpec((tm, width), lambda i: (i, cblk))
    full = lambda a: pl.BlockSpec(a.shape, lambda i: (0, 0))
    tab = pl.BlockSpec((tm, LANES), lambda i: (i % pos_blocks, 0))
    return pl.pallas_call(
        _mla_prep_kernel,
        grid=(n // tm,),
        in_specs=[blk(256, C_MLA_CQ // 256), blk(128, C_MLA_CKV // 128), blk(128, C_MLA_KR // 128),
                  blk(128, C_MLA_KRS // 128), tab, tab, full(qg), full(kvg),
                  full(wq), full(wqs), full(wk), full(wv)],
        out_specs=[pl.BlockSpec((tm, 512), lambda i: (i, 0)), pl.BlockSpec((tm, 512), lambda i: (i, 0)),
                   pl.BlockSpec((tm, 512), lambda i: (i, 0))],
        out_shape=[jax.ShapeDtypeStruct((n, 512), BF16), jax.ShapeDtypeStruct((n, 512), BF16),
                   jax.ShapeDtypeStruct((n, 512), BF16)],
        compiler_params=_params("parallel"),
        name="mla_prep",
    )(h, h, h, h, cos_t, sin_t, qg, kvg, wq, wqs, wk, wv)


MLA_Q_BLOCK = 256


def _mla_attn_kernel(q_ref, k_ref, v_ref, o_ref):
    seq = q_ref.shape[0]
    tq = MLA_Q_BLOCK
    ri = _iota2((tq, tq), 0)
    ci = _iota2((tq, tq), 1)
    diag_ok = ci <= ri
    lane = _iota2((tq, LANES), 1)
    for qi in range(seq // tq):
        q0 = qi * tq
        v_off = v_ref[0:q0, :] if qi else None
        v_diag = v_ref[q0:q0 + tq, :]
        outs = []
        for hh in range(2):
            qh = q_ref[q0:q0 + tq, hh * LANES:(hh + 1) * LANES]
            s_diag = _dot_nt(qh, k_ref[q0:q0 + tq, hh * LANES:(hh + 1) * LANES])
            s_diag = jnp.where(diag_ok, s_diag, MASK_VALUE)
            m = jnp.max(s_diag, axis=-1, keepdims=True)
            if qi:
                s_off = _dot_nt(qh, k_ref[0:q0, hh * LANES:(hh + 1) * LANES])
                m = jnp.maximum(m, jnp.max(s_off, axis=-1, keepdims=True))
            acc = _dot(_bf(jnp.exp2(s_diag - m)), v_diag)
            if qi:
                acc = acc + _dot(_bf(jnp.exp2(s_off - m)), v_off)
            outs.append(acc[:, 0:LANES] / acc[:, LANES:2 * LANES])
        o_ref[q0:q0 + tq, :] = jnp.where(lane < HEAD_DIM, outs[0], outs[1])


def _mla_attn(q, k, v, batch, seq):
    return pl.pallas_call(
        _mla_attn_kernel,
        grid=(batch, 2),
        in_specs=[pl.BlockSpec((seq, 256), lambda b, p: (b, p)),
                  pl.BlockSpec((seq, 256), lambda b, p: (b, p)),
                  pl.BlockSpec((seq, 256), lambda b, p: (b, p))],
        out_specs=pl.BlockSpec((seq, LANES), lambda b, p: (b, p)),
        out_shape=jax.ShapeDtypeStruct((batch * seq, GROUP_WIDTH), F32),
        compiler_params=_params("parallel", "parallel"),
        name="mla_attn",
    )(q, k, v)


HGRN_SCAN_UNROLL = 8


def _hgrn_kernel(q0_ref, q1_ref, f0_ref, f1_ref, i0_ref, i1_ref, g0_ref, g1_ref, lb_ref, ng_ref, o_ref,
                 q_s, kk_s, cum_s, v_s, oi_s, qd0, qd1, kd0, kd1, oc0, oc1, dec_s, incr_s):
    seq = q0_ref.shape[0]
    cc = HGRN_CHUNK
    nch = seq // cc
    w = GROUP_WIDTH
    same_head = (_iota2((w, w), 0) >> 6) == (_iota2((w, w), 1) >> 6)
    ones_bd = jnp.where(same_head, 1.0, 0.0).astype(BF16)
    lb = lb_ref[...]
    ng = ng_ref[...]
    one_m_lb = 1.0 - lb

    def rows_of(p):
        return pl.ds(p, nch, stride=cc)

    def both(ref0, ref1, rows):
        return jnp.concatenate([ref0[rows, :], ref1[rows, :]], axis=1)

    def put(ref0, ref1, rows, val):
        ref0[rows, :] = val[:, 0:LANES]
        ref1[rows, :] = val[:, LANES:2 * LANES]

    cum = None
    for p in range(cc):
        fl = both(f0_ref, f1_ref, rows_of(p))
        z = jnp.exp(-jnp.abs(fl))
        r = 1.0 / (1.0 + z)
        zr = z * r
        pos = fl >= 0.0
        log_f = jnp.log(lb + one_m_lb * jnp.where(pos, r, zr))
        cum = log_f if p == 0 else cum + log_f
        q_s[p] = both(q0_ref, q1_ref, rows_of(p))
        kk_s[p] = one_m_lb * jnp.where(pos, zr, r)
        cum_s[p] = cum
        v_s[p] = both(i0_ref, i1_ref, rows_of(p))
    dec_s[...] = jnp.exp(cum)
    for p in range(cc):
        c_p = cum_s[p]
        put(qd0, qd1, rows_of(p), q_s[p] * jnp.exp(c_p))
        put(kd0, kd1, rows_of(p), kk_s[p] * jnp.exp(cum_s[cc - 1] - c_p))

    for p in range(cc):
        q_p = q_s[p]
        c_p = cum_s[p]
        xs = [_bf(q_p * kk_s[s] * jnp.exp(c_p - cum_s[s])) for s in range(p)] + [_bf(q_p * kk_s[p])]
        sums = _dot(jnp.concatenate(xs, axis=0), ones_bd)
        acc = sums[p * nch:(p + 1) * nch] * v_s[p]
        for s in range(p):
            acc = acc + sums[s * nch:(s + 1) * nch] * v_s[s]
        oi_s[p] = acc

    gs = HGRN_SCAN_UNROLL
    blk_rows = gs * cc
    chunk_of_row = _iota2((blk_rows, w), 0) >> 4
    chunk_masks = [jnp.where(chunk_of_row == j, 1.0, 0.0).astype(BF16) for j in range(gs)]

    def expand(t):
        return jnp.concatenate([t * chunk_masks[j] for j in range(gs)], axis=1)

    def increments(g):
        rows = pl.ds(pl.multiple_of(g * blk_rows, blk_rows), blk_rows)
        return _dot_tn(_bf(both(i0_ref, i1_ref, rows)), expand(_bf(both(kd0, kd1, rows))))

    nsteps = nch // gs
    incr_s[...] = increments(0)

    def scan(g, st):
        incr_next = increments(jnp.minimum(g + 1, nsteps - 1))
        rows = pl.ds(pl.multiple_of(g * blk_rows, blk_rows), blk_rows)
        states = []
        for j in range(gs):
            states.append(_bf(st))
            st = st * dec_s[pl.ds(g * gs + j, 1), :] + jnp.where(same_head, incr_s[:, j * w:(j + 1) * w], 0.0)
        put(oc0, oc1, rows, _dot_nt(expand(_bf(both(qd0, qd1, rows))), jnp.concatenate(states, axis=1)))
        incr_s[...] = incr_next
        return st

    lax.fori_loop(0, nsteps, scan, jnp.zeros((w, w), F32))

    for p in range(cc):
        o = oi_s[p] + both(oc0, oc1, rows_of(p))
        ms = _group_sum(o * o, ones_bd) * (1.0 / HEAD_DIM)
        put(qd0, qd1, rows_of(p), o * lax.rsqrt(ms + NORM_EPS) * ng * _silu(both(g0_ref, g1_ref, rows_of(p))))
    o_ref[:, 0:LANES] = qd0[...]
    o_ref[:, LANES:2 * LANES] = qd1[...]


def _hgrn(h, batch, seq, lb, ng):
    base = C_HGRN // LANES
    half = lambda j: pl.BlockSpec((seq, LANES), lambda b: (b, base + j))
    full = lambda a: pl.BlockSpec(a.shape, lambda b: (0, 0))
    nch = seq // HGRN_CHUNK
    tiles = pltpu.VMEM((HGRN_CHUNK, nch, GROUP_WIDTH), F32)
    nat = pltpu.VMEM((seq, LANES), F32)
    return pl.pallas_call(
        _hgrn_kernel,
        grid=(batch,),
        in_specs=[half(j) for j in range(8)] + [full(lb), full(ng)],
        out_specs=pl.BlockSpec((seq, GROUP_WIDTH), lambda b: (b, 0)),
        out_shape=jax.ShapeDtypeStruct((batch * seq, GROUP_WIDTH), F32),
        scratch_shapes=[tiles] * 5 + [nat] * 6 + [pltpu.VMEM((nch, GROUP_WIDTH), F32),
                                                  pltpu.VMEM((GROUP_WIDTH, HGRN_SCAN_UNROLL * GROUP_WIDTH), F32)],
        compiler_params=_params("parallel"),
        name="hgrn",
    )(h, h, h, h, h, h, h, h, lb, ng)


def _dsa_kernel(q_ref, k_ref, v_ref, o_ref, num_ref, m_ref, l_ref):
    seq = q_ref.shape[0]
    blk = DSA_BLOCK
    pair = pl.program_id(1)
    qi = _iota2((blk, 2 * blk), 0)
    ki = _iota2((blk, 2 * blk), 1)
    steps = blk + qi - ki
    lane = _iota2((blk, LANES), 1)
    first = lane < HEAD_DIM
    qscale = HEAD_DIM ** -0.5 * LOG2E
    slopes = [jnp.where(pair == 0, ALIBI_SLOPES[hh], ALIBI_SLOPES[2 + hh]) * LOG2E for hh in range(2)]

    for bi, (window, dil) in enumerate(DSA_BRANCHES):
        nblk = seq // (dil * blk)
        in_window = (steps >= 0) & (steps <= window // dil)
        dist = (steps * dil).astype(F32)
        two_blocks = nblk > 1
        if two_blocks:
            bias = [jnp.where(in_window, -slopes[hh] * dist, MASK_VALUE) for hh in range(2)]
            bias_first = [jnp.where(ki >= blk, bias[hh], MASK_VALUE) for hh in range(2)]
        else:
            bias_first = [jnp.where(in_window, -slopes[hh] * dist, MASK_VALUE)[:, blk:] for hh in range(2)]
            bias = bias_first

        def attend_group(blocks, bi=bi, dil=dil, bias=bias, bias_first=bias_first, two_blocks=two_blocks):
            if dil == 1:
                ld = lambda ref, s0: ref[pl.ds(s0, blk), :]
                dst = lambda s0: pl.ds(bi * seq + s0, blk)
            else:
                ld = lambda ref, s0: ref[pl.ds(s0, blk, stride=dil), :]
                dst = lambda s0: pl.ds(bi * seq + s0, blk, stride=dil)
            tiles, vbs = [], []
            for start, is_first in blocks:
                qb = ld(q_ref, start) * qscale
                if two_blocks:
                    prev = start if is_first else start - dil * blk
                    kb = _bf(jnp.concatenate([ld(k_ref, prev), ld(k_ref, start)], axis=0))
                    vb = _bf(jnp.concatenate([ld(v_ref, prev), ld(v_ref, start)], axis=0))
                else:
                    kb = _bf(ld(k_ref, start))
                    vb = _bf(ld(v_ref, start))
                vbs.append(jnp.concatenate([vb, jnp.ones(vb.shape, BF16)], axis=1))
                for hh in range(2):
                    qm = _bf(jnp.where(first if hh == 0 else jnp.logical_not(first), qb, 0.0))
                    tiles.append(_dot_nt(qm, kb) + (bias_first if is_first else bias)[hh])
            s = jnp.concatenate(tiles, axis=0)
            m = jnp.max(s, axis=-1, keepdims=True)
            pb = _bf(jnp.exp2(s - m))
            for i, (start, _) in enumerate(blocks):
                r0, r1, r2 = 2 * i * blk, (2 * i + 1) * blk, (2 * i + 2) * blk
                pv = jnp.where(jnp.concatenate([first, first], axis=1), _dot(pb[r0:r1], vbs[i]), _dot(pb[r1:r2], vbs[i]))
                num_ref[dst(start), :] = pv[:, 0:LANES]
                l_ref[dst(start), :] = pv[:, LANES:2 * LANES]
                m_ref[dst(start), :] = jnp.where(first, m[r0:r1], m[r1:r2])

        group = DSA_GROUP
        if nblk == 1:
            def body(g, carry, attend_group=attend_group):
                attend_group([(g * group + j, True) for j in range(group)])
                return carry
            lax.fori_loop(0, dil // group, body, 0)
        elif nblk < group:
            per = group // nblk

            def body(g, carry, attend_group=attend_group, dil=dil, nblk=nblk, per=per):
                attend_group([(g * per + j + dil * blk * n, n == 0) for j in range(per) for n in range(nblk)])
                return carry
            lax.fori_loop(0, dil // per, body, 0)
        else:
            assert dil == 1 and nblk % group == 0
            attend_group([(n * blk, n == 0) for n in range(group)])

            def body(g, carry, attend_group=attend_group):
                attend_group([(pl.multiple_of((g * group + j) * blk, blk), False) for j in range(group)])
                return carry
            lax.fori_loop(1, nblk // group, body, 0)

    def merge(n, carry):
        r0 = pl.multiple_of(n * blk, blk)
        rows = [pl.ds(bi * seq + r0, blk) for bi in range(len(DSA_BRANCHES))]
        m0, m1, m2 = [m_ref[r, :] for r in rows]
        mm = jnp.maximum(jnp.maximum(m0, m1), m2)
        w0, w1, w2 = jnp.exp2(m0 - mm), jnp.exp2(m1 - mm), jnp.exp2(m2 - mm)
        num = w0 * num_ref[rows[0], :] + w1 * num_ref[rows[1], :] + w2 * num_ref[rows[2], :]
        den = w0 * l_ref[rows[0], :] + w1 * l_ref[rows[1], :] + w2 * l_ref[rows[2], :]
        o_ref[pl.ds(r0, blk), :] = num / den
        return carry

    lax.fori_loop(0, seq // blk, merge, 0)


def _dsa(h, batch, seq):
    base = C_DSA // LANES
    blk = lambda j: pl.BlockSpec((seq, LANES), lambda b, p: (b, base + 2 * j + p))
    nb = len(DSA_BRANCHES)
    return pl.pallas_call(
        _dsa_kernel,
        grid=(batch, 2),
        in_specs=[blk(0), blk(1), blk(2)],
        out_specs=pl.BlockSpec((seq, LANES), lambda b, p: (b, p)),
        out_shape=jax.ShapeDtypeStruct((batch * seq, GROUP_WIDTH), F32),
        scratch_shapes=[pltpu.VMEM((nb * seq, LANES), F32), pltpu.VMEM((nb * seq, LANES), F32),
                        pltpu.VMEM((nb * seq, LANES), F32)],
        compiler_params=_params("parallel", "parallel"),
        name="dsa",
    )(h, h, h)


FF_CHUNK = 1024


def _post_kernel(oa_ref, ob_ref, oc_ref, od_ref, x_ref, wo_ref, g1_ref, b1_ref, w1_ref, w2_ref, g2_ref, b2_ref,
                 y_ref):
    gw = GROUP_WIDTH
    mixed = _dot(_bf(oa_ref[...]), wo_ref[0:gw, :])
    mixed = mixed + _dot(_bf(ob_ref[...]), wo_ref[gw:2 * gw, :])
    mixed = mixed + _dot(_bf(oc_ref[...]), wo_ref[2 * gw:3 * gw, :])
    mixed = mixed + _dot(_bf(od_ref[...]), wo_ref[3 * gw:4 * gw, :])
    x = _layer_norm_rows(DEEPNORM_ALPHA * x_ref[...] + mixed, g1_ref[...], b1_ref[...])
    xb = _bf(x)
    acc = jnp.zeros(x.shape, F32)
    for c in range(D_FF // FF_CHUNK):
        hmid = _dot(xb, w1_ref[:, c * FF_CHUNK:(c + 1) * FF_CHUNK])
        hmid = jnp.square(jnp.maximum(hmid, 0.0))
        acc = acc + _dot(_bf(hmid), w2_ref[c * FF_CHUNK:(c + 1) * FF_CHUNK, :])
    y_ref[...] = _layer_norm_rows(DEEPNORM_ALPHA * x + acc, g2_ref[...], b2_ref[...])


def _post(oa, ob, oc, od, x2d, wo, g1, b1, w1, w2, g2, b2):
    n = x2d.shape[0]
    tm = ROW_TILE
    grp = pl.BlockSpec((tm, GROUP_WIDTH), lambda i: (i, 0))
    row = pl.BlockSpec((tm, D_MODEL), lambda i: (i, 0))
    resident = lambda a: pl.BlockSpec(a.shape, lambda i: (0, 0), pipeline_mode=pl.Buffered(1))
    return pl.pallas_call(
        _post_kernel,
        grid=(n // tm,),
        in_specs=[grp, grp, grp, grp, row, resident(wo), resident(g1), resident(b1),
                  resident(w1), resident(w2), resident(g2), resident(b2)],
        out_specs=row,
        out_shape=jax.ShapeDtypeStruct((n, D_MODEL), F32),
        compiler_params=_params("parallel"),
        name="post",
    )(oa, ob, oc, od, x2d, wo, g1, b1, w1, w2, g2, b2)


def _expand_heads(wcols):
    return jnp.repeat(wcols, HEAD_DIM, axis=-1)


def _arrange_w_in(w):
    pts = np.cumsum(IN_SIZES)[:-1].tolist()
    (a_qkv, a_a, a_b, a_z, b_cq, b_ckv, b_kr, c_q, c_f, c_i, c_g, d_qkv) = jnp.split(_bf(w), pts, axis=-1)
    z = lambda n: jnp.zeros(w.shape[:-1] + (n,), BF16)
    half = MLA_ROPE // 2
    kr_sw = jnp.concatenate([b_kr[..., half:], b_kr[..., :half]], axis=-1)
    cols = [a_qkv, _expand_heads(a_a), _expand_heads(a_b), a_z,
            b_cq, z(256 - MLA_Q_RANK), b_ckv,
            z(MLA_NOPE), b_kr, z(LANES - MLA_NOPE - MLA_ROPE),
            z(MLA_NOPE), kr_sw, z(LANES - MLA_NOPE - MLA_ROPE), z(C_HGRN - C_MLA_KRS - LANES),
            c_q, c_f, c_i, c_g, d_qkv]
    return jnp.concatenate(cols, axis=-1)


def _arrange_mla_weights(w_uq, w_ukv):
    lead = w_uq.shape[:-2]
    rq = w_uq.shape[-2]
    wq = _bf(w_uq).reshape(lead + (rq, N_HEADS, MLA_NOPE + MLA_ROPE))
    half = MLA_ROPE // 2
    zq = jnp.zeros(lead + (rq, N_HEADS, LANES - MLA_NOPE - MLA_ROPE), BF16)
    q_main = jnp.concatenate([wq, zq], axis=-1)
    rope = wq[..., MLA_NOPE:]
    rope_sw = jnp.concatenate([rope[..., half:], rope[..., :half]], axis=-1)
    q_swap = jnp.concatenate([jnp.zeros(lead + (rq, N_HEADS, MLA_NOPE), BF16), rope_sw, zq], axis=-1)
    pad_rows = lambda m: jnp.concatenate([m, jnp.zeros(lead + (256 - rq, m.shape[-1]), BF16)], axis=-2)
    q_main = pad_rows(q_main.reshape(lead + (rq, N_HEADS * LANES)))
    q_swap = pad_rows(q_swap.reshape(lead + (rq, N_HEADS * LANES)))
    rkv = w_ukv.shape[-2]
    wkv = _bf(w_ukv).reshape(lead + (rkv, N_HEADS, MLA_NOPE + HEAD_DIM))
    k_w = jnp.concatenate([wkv[..., :MLA_NOPE], jnp.zeros(lead + (rkv, N_HEADS, LANES - MLA_NOPE), BF16)], axis=-1)
    v_w = wkv[..., MLA_NOPE:]
    return (q_main, q_swap, k_w.reshape(lead + (rkv, N_HEADS * LANES)),
            v_w.reshape(lead + (rkv, N_HEADS * HEAD_DIM)))


def _rope_tables(seq):
    half = MLA_ROPE // 2
    pos = jnp.arange(seq, dtype=F32)
    inv_freq = ROPE_BASE ** (-jnp.arange(half, dtype=F32) / half)
    ang = pos[:, None] * inv_freq[None, :]
    cos, sin = jnp.cos(ang), jnp.sin(ang)
    ones = jnp.ones((seq, MLA_NOPE), F32)
    zeros = jnp.zeros((seq, MLA_NOPE), F32)
    tail1 = jnp.ones((seq, LANES - MLA_NOPE - MLA_ROPE), F32)
    tail0 = jnp.zeros((seq, LANES - MLA_NOPE - MLA_ROPE), F32)
    cos_t = jnp.concatenate([ones, cos, cos, tail1], axis=1)
    sin_t = jnp.concatenate([zeros, -sin, sin, tail0], axis=1)
    return cos_t, sin_t


def _tile_heads(vec):
    return jnp.tile(vec.astype(F32), N_HEADS)[..., None, :]


def kernel(x, w_in, gdn_conv_w, gdn_a_log, gdn_dt_bias, gdn_norm_g, mla_q_norm_g, mla_kv_norm_g,
           mla_w_uq, mla_w_ukv, hgrn_lb_logits, hgrn_norm_g, w_out, ln1_g, ln1_b, w_ff1, w_ff2,
           ln2_g, ln2_b):
    batch, seq, d_model = x.shape
    assert d_model == D_MODEL and seq % (16 * DSA_BLOCK) == 0 and (batch * seq) % ROW_TILE == 0
    x2d = x.reshape(batch * seq, d_model)
    p_lb = jax.nn.softmax(hgrn_lb_logits.astype(F32), axis=0)
    lower_bounds = (jnp.cumsum(p_lb, axis=0) - p_lb[:1])[:, None, :]
    cos_t, sin_t = _rope_tables(seq)
    w_in_a = _arrange_w_in(w_in)
    avec = jnp.repeat(-jnp.exp(gdn_a_log.astype(F32)), HEAD_DIM, axis=-1)[:, None, :]
    dtb = jnp.repeat(gdn_dt_bias.astype(F32), HEAD_DIM, axis=-1)[:, None, :]
    conv_w = gdn_conv_w.astype(F32)
    gdn_g = _tile_heads(gdn_norm_g)
    hgrn_g = _tile_heads(hgrn_norm_g)
    wq, wqs, wk, wv = _arrange_mla_weights(mla_w_uq, mla_w_ukv)
    qg = jnp.concatenate([mla_q_norm_g.astype(F32), jnp.zeros((DEPTH, 256 - MLA_Q_RANK), F32)], axis=-1)[:, None, :]
    kvg = mla_kv_norm_g.astype(F32)[:, None, :]
    wo, w1, w2 = _bf(w_out), _bf(w_ff1), _bf(w_ff2)
    g1, b1, g2, b2 = (t.astype(F32)[:, None, :] for t in (ln1_g, ln1_b, ln2_g, ln2_b))
    for l in range(DEPTH):
        h = _inproj(x2d, w_in_a[l])
        o_a = _gdn(h, batch, seq, conv_w[l], avec[l], dtb[l], gdn_g[l])
        q_m, k_m, v_m = _mla_prep(h, seq, cos_t, sin_t, qg[l], kvg[l], wq[l], wqs[l], wk[l], wv[l])
        o_b = _mla_attn(q_m, k_m, v_m, batch, seq)
        o_c = _hgrn(h, batch, seq, lower_bounds[l], hgrn_g[l])
        o_d = _dsa(h, batch, seq)
        x2d = _post(o_a, o_b, o_c, o_d, x2d, wo[l], g1[l], b1[l], w1[l], w2[l], g2[l], b2[l])
    return x2d.reshape(batch, seq, d_model)
```

```python
import functools
import math

import numpy as np
import jax
import jax.numpy as jnp
from jax import lax
from jax.experimental import pallas as pl
from jax.experimental.pallas import tpu as pltpu

F32 = jnp.float32
BF16 = jnp.bfloat16

D_MODEL = 1024
DEPTH = 2
GROUP_WIDTH = 256
N_HEADS = 4
HEAD_DIM = 64
GDN_CONV = 4
GDN_CHUNK = 64
MLA_NOPE = 64
MLA_ROPE = 32
MLA_Q_RANK = 192
MLA_KV_RANK = 128
ROPE_BASE = 10000.0
HGRN_CHUNK = 16
DSA_BRANCHES = ((128, 1), (512, 4), (2048, 16))
DSA_BLOCK = 128
DSA_GROUP = 8
ALIBI_SLOPES = tuple(2.0 ** (-8.0 * (j + 1) / N_HEADS) for j in range(N_HEADS))
D_FF = 4 * D_MODEL
DEEPNORM_ALPHA = (2 * DEPTH) ** 0.25
NORM_EPS = 1e-6
MASK_VALUE = -1e30
LOG2E = 1.4426950408889634

LANES = 128
SUBLANES = 8
VMEM_LIMIT_BYTES = 56 * 1024 * 1024

C_GDN_QKV = 0
C_GDN_A = 768
C_GDN_B = 1024
C_GDN_Z = 1280
C_MLA_CQ = 1536
C_MLA_CKV = 1792
C_MLA_KR = 1920
C_MLA_KRS = 2048
C_HGRN = 2304
C_DSA = 3328
C_TOTAL = 4096
IN_SIZES = (768, 4, 4, 256, 192, 128, 32, 256, 256, 256, 256, 768)
ROW_TILE = 512


def _bf(x):
    return x.astype(BF16)


def _dot(a, b):
    return jnp.dot(a, b, preferred_element_type=F32)


def _dot_nt(a, b):
    return lax.dot_general(a, b, (((1,), (1,)), ((), ())), preferred_element_type=F32)


def _dot_tn(a, b):
    return lax.dot_general(a, b, (((0,), (0,)), ((), ())), preferred_element_type=F32)


def _split3(x):
    hi = _bf(x)
    r1 = x - hi.astype(F32)
    mid = _bf(r1)
    lo = _bf(r1 - mid.astype(F32))
    return hi, mid, lo


def _dot_sel_r(x, sel):
    hi, mid, lo = _split3(x)
    return _dot(hi, sel) + _dot(mid, sel) + _dot(lo, sel)


def _dot_sel_l(sel, x):
    hi, mid, lo = _split3(x)
    return _dot(sel, hi) + _dot(sel, mid) + _dot(sel, lo)


def _sigmoid(x):
    return 0.5 + 0.5 * jnp.tanh(0.5 * x)


def _silu(x):
    return x * _sigmoid(x)


def _softplus(x):
    return jnp.maximum(x, 0.0) + jnp.log(1.0 + jnp.exp(-jnp.abs(x)))


def _iota2(shape, axis):
    return lax.broadcasted_iota(jnp.int32, shape, axis)


def _head_block_ones(n):
    r = _iota2((n, n), 0) >> 6
    c = _iota2((n, n), 1) >> 6
    return jnp.where(r == c, 1.0, 0.0).astype(BF16)


def _group_sum(x, ones_bd):
    return _dot(_bf(x), ones_bd)


def _layer_norm_rows(y, g, b):
    mu = jnp.mean(y, axis=-1, keepdims=True)
    d = y - mu
    var = jnp.mean(d * d, axis=-1, keepdims=True)
    return d * lax.rsqrt(var + NORM_EPS) * g + b


def _params(*sem):
    return pltpu.CompilerParams(dimension_semantics=sem, vmem_limit_bytes=VMEM_LIMIT_BYTES)


def _inproj_kernel(x_ref, w_ref, o_ref):
    xb = _bf(x_ref[...])
    for c in range(C_TOTAL // 512):
        o_ref[:, c * 512:(c + 1) * 512] = _dot(xb, w_ref[:, c * 512:(c + 1) * 512])


def _inproj(x2d, w):
    n = x2d.shape[0]
    return pl.pallas_call(
        _inproj_kernel,
        grid=(n // ROW_TILE,),
        in_specs=[pl.BlockSpec((ROW_TILE, D_MODEL), lambda i: (i, 0)),
                  pl.BlockSpec((D_MODEL, C_TOTAL), lambda i: (0, 0))],
        out_specs=pl.BlockSpec((ROW_TILE, C_TOTAL), lambda i: (i, 0)),
        out_shape=jax.ShapeDtypeStruct((n, C_TOTAL), F32),
        compiler_params=_params("parallel"),
        name="inproj",
    )(x2d, w)


GDN_GROUP = 8


def _gdn_kernel(qkv_ref, a_ref, b_ref, z_ref, convw_ref, avec_ref, dtb_ref, ng_ref, o_ref,
                s_ref, u_ref, egl_ref, ku_ref, w_ref, qd_ref, qk_ref, kw_ref):
    seq = qkv_ref.shape[0]
    c = GDN_CHUNK
    w = GROUP_WIDTH
    rows_per_step = GDN_GROUP * c
    s_ref[...] = jnp.zeros_like(s_ref)

    same_head = (_iota2((w, w), 0) >> 6) == (_iota2((w, w), 1) >> 6)
    ones_bd = jnp.where(same_head, 1.0, 0.0).astype(BF16)
    rr = _iota2((rows_per_step, rows_per_step), 0)
    rc = _iota2((rows_per_step, rows_per_step), 1)
    same_chunk = (rr >> 6) == (rc >> 6)
    tril_chunks = jnp.where(same_chunk & (rc <= rr), 1.0, 0.0).astype(BF16)
    ones_chunks = jnp.where(same_chunk, 1.0, 0.0).astype(BF16)
    li = _iota2((rows_per_step, w), 0) & (c - 1)
    lj = _iota2((rows_per_step, w), 1) & (c - 1)
    causal = lj <= li
    strict = lj < li
    eye = lj == li
    lane_head = _iota2((c, w), 1) >> 6
    first_rows = _iota2((SUBLANES, 3 * w), 0)

    def stack_heads(t):
        return jnp.concatenate([jnp.where(lane_head == h, t, jnp.zeros_like(t)) for h in range(N_HEADS)], axis=0)

    def block_diag(t):
        return jnp.where(same_head, jnp.concatenate([t] * N_HEADS, axis=0), jnp.zeros((w, w), t.dtype))

    convw = convw_ref[...]
    avec = avec_ref[...]
    dtb = dtb_ref[...]
    ng = ng_ref[...]
    chunks = [slice(i * c, (i + 1) * c) for i in range(GDN_GROUP)]

    def prep(n, carry):
        r0 = pl.multiple_of(n * rows_per_step, rows_per_step)
        rows = pl.ds(r0, rows_per_step)
        cur = qkv_ref[rows, :]
        prev = qkv_ref[pl.ds(pl.multiple_of(jnp.maximum(r0 - SUBLANES, 0), SUBLANES), SUBLANES), :]
        prev = jnp.where(n > 0, prev, 0.0)
        y = cur * convw[GDN_CONV - 1:GDN_CONV, :]
        for j in range(1, GDN_CONV):
            shifted = pltpu.roll(cur, j, 0)
            head = jnp.where(first_rows < j, pltpu.roll(prev, j, 0), shifted[0:SUBLANES, :])
            shifted = jnp.concatenate([head, shifted[SUBLANES:, :]], axis=0)
            y = y + shifted * convw[GDN_CONV - 1 - j:GDN_CONV - j, :]
        y = _silu(y)
        q = y[:, 0:w]
        k = y[:, w:2 * w]
        v = y[:, 2 * w:3 * w]
        q = q * lax.rsqrt(_group_sum(q * q, ones_bd) + NORM_EPS) * (HEAD_DIM ** -0.5)
        k = k * lax.rsqrt(_group_sum(k * k, ones_bd) + NORM_EPS)
        beta = _sigmoid(b_ref[rows, :])
        gstep = avec * _softplus(a_ref[rows, :] + dtb)
        g = _dot_sel_l(tril_chunks, gstep)
        g_last = jnp.concatenate([jnp.broadcast_to(g[sl][c - 1:c, :], (c, w)) for sl in chunks], axis=0)
        gr = _dot_sel_l(ones_chunks, jnp.where(eye, g, 0.0))
        decay = jnp.where(causal, jnp.exp(jnp.where(causal, g - gr, 0.0)), 0.0)
        eg = jnp.exp(g)
        kb = k * beta
        kbb = _bf(kb)
        qb = _bf(q)
        prods = [_dot_nt(jnp.concatenate([kbb[sl], qb[sl]], axis=0), _bf(stack_heads(k[sl]))) for sl in chunks]
        lower = jnp.where(strict, jnp.concatenate([p[0:c] for p in prods], axis=0) * decay, 0.0)
        qk = jnp.concatenate([p[c:2 * c] for p in prods], axis=0) * decay
        m = -lower
        t = jnp.where(eye, 1.0, 0.0) + m
        for level in range(6):
            mb = _bf(m)
            tb = _bf(t)
            new_m, t_m = [], []
            for sl in chunks:
                m_bd = block_diag(mb[sl])
                if level == 0:
                    new_m.append(_dot(mb[sl], m_bd))
                elif level < 5:
                    both = _dot(jnp.concatenate([mb[sl], tb[sl]], axis=0), m_bd)
                    new_m.append(both[0:c])
                    t_m.append(both[c:2 * c])
                else:
                    t_m.append(_dot(tb[sl], m_bd))
            if level < 5:
                m = jnp.concatenate(new_m, axis=0)
            if level > 0:
                t = t + jnp.concatenate(t_m, axis=0)
        tb = _bf(t)
        vb = v * beta
        kbg = kb * eg
        uw = [_dot(tb[sl], _bf(jnp.concatenate([stack_heads(vb[sl]), stack_heads(kbg[sl])], axis=1)))
              for sl in chunks]
        u_ref[rows, :] = jnp.concatenate([x[:, 0:w] for x in uw], axis=0)
        w_ref[rows, :] = _bf(jnp.concatenate([x[:, w:2 * w] for x in uw], axis=0))
        qd_ref[rows, :] = _bf(q * eg)
        qk_ref[rows, :] = _bf(qk)
        egl_ref[rows, :] = jnp.exp(g_last)
        k_dec = _bf(k * jnp.exp(g_last - g))
        for ci, sl in enumerate(chunks):
            kw_ku = _dot_tn(k_dec[sl], _bf(uw[ci]))
            mat_rows = pl.ds(pl.multiple_of((n * GDN_GROUP + ci) * w, w), w)
            ku_ref[mat_rows, :] = kw_ku[:, 0:w]
            kw_ref[mat_rows, :] = _bf(kw_ku[:, w:2 * w])
        return carry

    lax.fori_loop(0, seq // rows_per_step, prep, 0)

    def scan(n, carry):
        step0 = pl.multiple_of(n * rows_per_step, rows_per_step)
        for ci in range(GDN_GROUP):
            rows = pl.ds(step0 + ci * c, c)
            mat_rows = pl.ds(pl.multiple_of((n * GDN_GROUP + ci) * w, w), w)
            s_bd = s_ref[...]
            lhs = jnp.concatenate([kw_ref[mat_rows, :], w_ref[rows, :], qd_ref[rows, :]], axis=0)
            r = _dot(lhs, _bf(s_bd))
            s_ref[...] = s_bd * egl_ref[pl.ds(step0 + ci * c, 1), :] + jnp.where(
                same_head, ku_ref[mat_rows, :] - r[0:w], 0.0)
            v_new = u_ref[rows, :] - r[w:w + c]
            o_ref[rows, :] = r[w + c:w + 2 * c] + _dot(qk_ref[rows, :], _bf(stack_heads(v_new)))
        return carry

    lax.fori_loop(0, seq // rows_per_step, scan, 0)

    def readout(n, carry):
        rows = pl.ds(pl.multiple_of(n * rows_per_step, rows_per_step), rows_per_step)
        o = o_ref[rows, :]
        ms = _group_sum(o * o, ones_bd) * (1.0 / HEAD_DIM)
        o_ref[rows, :] = o * lax.rsqrt(ms + NORM_EPS) * ng * _silu(z_ref[rows, :])
        return carry

    lax.fori_loop(0, seq // rows_per_step, readout, 0)


def _gdn(h, batch, seq, convw, avec, dtb, ng):
    blk = lambda width, cblk: pl.BlockSpec((seq, width), lambda b: (b, cblk))
    full = lambda a: pl.BlockSpec(a.shape, lambda b: (0, 0))
    return pl.pallas_call(
        _gdn_kernel,
        grid=(batch,),
        in_specs=[blk(768, C_GDN_QKV // 768), blk(256, C_GDN_A // 256), blk(256, C_GDN_B // 256),
                  blk(256, C_GDN_Z // 256), full(convw), full(avec), full(dtb), full(ng)],
        out_specs=pl.BlockSpec((seq, GROUP_WIDTH), lambda b: (b, 0)),
        out_shape=jax.ShapeDtypeStruct((batch * seq, GROUP_WIDTH), F32),
        scratch_shapes=[pltpu.VMEM((GROUP_WIDTH, GROUP_WIDTH), F32),
                        pltpu.VMEM((seq, GROUP_WIDTH), F32), pltpu.VMEM((seq, GROUP_WIDTH), F32),
                        pltpu.VMEM((seq // GDN_CHUNK * GROUP_WIDTH, GROUP_WIDTH), F32),
                        pltpu.VMEM((seq, GROUP_WIDTH), BF16), pltpu.VMEM((seq, GROUP_WIDTH), BF16),
                        pltpu.VMEM((seq, GROUP_WIDTH), BF16),
                        pltpu.VMEM((seq // GDN_CHUNK * GROUP_WIDTH, GROUP_WIDTH), BF16)],
        compiler_params=_params("parallel"),
        name="gdn",
    )(h, h, h, h, convw, avec, dtb, ng)


def _mla_prep_kernel(cq_ref, ckv_ref, kr_ref, krs_ref, cos_ref, sin_ref, qg_ref, kvg_ref,
                     wq_ref, wqs_ref, wk_ref, wv_ref, q_ref, k_ref, v_ref):
    cq = cq_ref[...]
    nq = cq * lax.rsqrt(jnp.sum(cq * cq, axis=-1, keepdims=True) * (1.0 / MLA_Q_RANK) + NORM_EPS) * qg_ref[...]
    nqb = _bf(nq)
    cos1 = cos_ref[...]
    sin1 = sin_ref[...]
    cos4 = jnp.concatenate([cos1] * N_HEADS, axis=1)
    sin4 = jnp.concatenate([sin1] * N_HEADS, axis=1)
    scale = (MLA_NOPE + MLA_ROPE) ** -0.5 * LOG2E
    q = (_dot(nqb, wq_ref[...]) * cos4 + _dot(nqb, wqs_ref[...]) * sin4) * scale
    q_ref[...] = _bf(q)
    ckv = ckv_ref[...]
    nkv = ckv * lax.rsqrt(jnp.mean(ckv * ckv, axis=-1, keepdims=True) + NORM_EPS) * kvg_ref[...]
    nkvb = _bf(nkv)
    kr = kr_ref[...] * cos1 + krs_ref[...] * sin1
    k = _dot(nkvb, wk_ref[...]) + jnp.concatenate([kr] * N_HEADS, axis=1)
    k_ref[...] = _bf(k)
    v = _bf(_dot(nkvb, wv_ref[...]))
    ones = jnp.ones((v.shape[0], LANES), BF16)
    v_ref[...] = jnp.concatenate([v[:, 0:LANES], ones, v[:, LANES:2 * LANES], ones], axis=1)


def _mla_prep(h, seq, cos_t, sin_t, qg, kvg, wq, wqs, wk, wv):
    n = h.shape[0]
    tm = ROW_TILE
    pos_blocks = seq // tm
    blk = lambda width, cblk: pl.BlockSpec((tm, width), lambda i: (i, cblk))
    full = lambda a: pl.BlockSpec(a.shape, lambda i: (0, 0))
    tab = pl.BlockSpec((tm, LANES), lambda i: (i % pos_blocks, 0))
    return pl.pallas_call(
        _mla_prep_kernel,
        grid=(n // tm,),
        in_specs=[blk(256, C_MLA_CQ // 256), blk(128, C_MLA_CKV // 128), blk(128, C_MLA_KR // 128),
                  blk(128, C_MLA_KRS // 128), tab, tab, full(qg), full(kvg),
                  full(wq), full(wqs), full(wk), full(wv)],
        out_specs=[pl.BlockSpec((tm, 512), lambda i: (i, 0)), pl.BlockSpec((tm, 512), lambda i: (i, 0)),
                   pl.BlockSpec((tm, 512), lambda i: (i, 0))],
        out_shape=[jax.ShapeDtypeStruct((n, 512), BF16), jax.ShapeDtypeStruct((n, 512), BF16),
                   jax.ShapeDtypeStruct((n, 512), BF16)],
        compiler_params=_params("parallel"),
        name="mla_prep",
    )(h, h, h, h, cos_t, sin_t, qg, kvg, wq, wqs, wk, wv)


MLA_Q_BLOCK = 256


def _mla_attn_kernel(q_ref, k_ref, v_ref, o_ref):
    seq = q_ref.shape[0]
    tq = MLA_Q_BLOCK
    ri = _iota2((tq, tq), 0)
    ci = _iota2((tq, tq), 1)
    diag_ok = ci <= ri
    lane = _iota2((tq, LANES), 1)
    for qi in range(seq // tq):
        q0 = qi * tq
        v_off = v_ref[0:q0, :] if qi else None
        v_diag = v_ref[q0:q0 + tq, :]
        outs = []
        for hh in range(2):
            qh = q_ref[q0:q0 + tq, hh * LANES:(hh + 1) * LANES]
            s_diag = _dot_nt(qh, k_ref[q0:q0 + tq, hh * LANES:(hh + 1) * LANES])
            s_diag = jnp.where(diag_ok, s_diag, MASK_VALUE)
            m = jnp.max(s_diag, axis=-1, keepdims=True)
            if qi:
                s_off = _dot_nt(qh, k_ref[0:q0, hh * LANES:(hh + 1) * LANES])
                m = jnp.maximum(m, jnp.max(s_off, axis=-1, keepdims=True))
            acc = _dot(_bf(jnp.exp2(s_diag - m)), v_diag)
            if qi:
                acc = acc + _dot(_bf(jnp.exp2(s_off - m)), v_off)
            outs.append(acc[:, 0:LANES] / acc[:, LANES:2 * LANES])
        o_ref[q0:q0 + tq, :] = jnp.where(lane < HEAD_DIM, outs[0], outs[1])


def _mla_attn(q, k, v, batch, seq):
    return pl.pallas_call(
        _mla_attn_kernel,
        grid=(batch, 2),
        in_specs=[pl.BlockSpec((seq, 256), lambda b, p: (b, p)),
                  pl.BlockSpec((seq, 256), lambda b, p: (b, p)),
                  pl.BlockSpec((seq, 256), lambda b, p: (b, p))],
        out_specs=pl.BlockSpec((seq, LANES), lambda b, p: (b, p)),
        out_shape=jax.ShapeDtypeStruct((batch * seq, GROUP_WIDTH), F32),
        compiler_params=_params("parallel", "parallel"),
        name="mla_attn",
    )(q, k, v)


HGRN_SCAN_UNROLL = 8
HGRN_SUBTILE = 32


def _hgrn_kernel(q0_ref, q1_ref, f0_ref, f1_ref, i0_ref, i1_ref, g0_ref, g1_ref, lb_ref, ng_ref, o_ref,
                 q_s, kk_s, cum_s, v_s, oi_s, qd0, qd1, kd0, kd1, oc0, oc1, dec_s, incr_s):
    seq = q0_ref.shape[0]
    cc = HGRN_CHUNK
    nch = seq // cc
    w = GROUP_WIDTH
    same_head = (_iota2((w, w), 0) >> 6) == (_iota2((w, w), 1) >> 6)
    ones_bd = jnp.where(same_head, 1.0, 0.0).astype(BF16)
    lb = lb_ref[...]
    ng = ng_ref[...]
    one_m_lb = 1.0 - lb

    def rows_of(p):
        return pl.ds(p, nch, stride=cc)

    def both(ref0, ref1, rows):
        return jnp.concatenate([ref0[rows, :], ref1[rows, :]], axis=1)

    def put(ref0, ref1, rows, val):
        ref0[rows, :] = val[:, 0:LANES]
        ref1[rows, :] = val[:, LANES:2 * LANES]

    quarter = seq // 4

    def split(dst, src):
        for r in range(4):
            dst[r * quarter:(r + 1) * quarter, :] = src[pl.ds(r, quarter, stride=4), :]

    def rows_of4(p):
        return pl.ds((p % 4) * quarter + p // 4, nch, stride=4)

    split(oc0, f0_ref)
    split(oc1, f1_ref)
    split(qd0, q0_ref)
    split(qd1, q1_ref)
    split(kd0, i0_ref)
    split(kd1, i1_ref)

    cum = None
    for p in range(cc):
        fl = both(oc0, oc1, rows_of4(p))
        z = jnp.exp(-jnp.abs(fl))
        r = 1.0 / (1.0 + z)
        zr = z * r
        pos = fl >= 0.0
        log_f = jnp.log(lb + one_m_lb * jnp.where(pos, r, zr))
        cum = log_f if p == 0 else cum + log_f
        q_s[p] = both(qd0, qd1, rows_of4(p))
        kk_s[p] = one_m_lb * jnp.where(pos, zr, r)
        cum_s[p] = cum
        v_s[p] = both(kd0, kd1, rows_of4(p))
    dec_s[...] = jnp.exp(cum)
    for p in range(cc):
        c_p = cum_s[p]
        put(qd0, qd1, rows_of(p), q_s[p] * jnp.exp(c_p))
        put(kd0, kd1, rows_of(p), kk_s[p] * jnp.exp(cum_s[cc - 1] - c_p))

    sub = HGRN_SUBTILE

    def intra(t, carry):
        rs = pl.ds(pl.multiple_of(t * sub, sub), sub)
        for p in range(cc):
            q_p = q_s[p, rs, :]
            c_p = cum_s[p, rs, :]
            xs = [_bf(q_p * kk_s[s, rs, :] * jnp.exp(c_p - cum_s[s, rs, :])) for s in range(p)]
            xs.append(_bf(q_p * kk_s[p, rs, :]))
            sums = _dot(jnp.concatenate(xs, axis=0), ones_bd)
            acc = sums[p * sub:(p + 1) * sub] * v_s[p, rs, :]
            for s in range(p):
                acc = acc + sums[s * sub:(s + 1) * sub] * v_s[s, rs, :]
            oi_s[p, rs, :] = acc
        return carry

    lax.fori_loop(0, nch // sub, intra, 0)

    gs = HGRN_SCAN_UNROLL
    blk_rows = gs * cc
    chunk_of_row = _iota2((blk_rows, w), 0) >> 4
    chunk_masks = [jnp.where(chunk_of_row == j, 1.0, 0.0).astype(BF16) for j in range(gs)]

    def expand(t):
        return jnp.concatenate([t * chunk_masks[j] for j in range(gs)], axis=1)

    def increments(g):
        rows = pl.ds(pl.multiple_of(g * blk_rows, blk_rows), blk_rows)
        return _dot_tn(_bf(both(i0_ref, i1_ref, rows)), expand(_bf(both(kd0, kd1, rows))))

    nsteps = nch // gs
    incr_s[...] = increments(0)

    def scan(g, st):
        incr_next = increments(jnp.minimum(g + 1, nsteps - 1))
        rows = pl.ds(pl.multiple_of(g * blk_rows, blk_rows), blk_rows)
        states = []
        for j in range(gs):
            states.append(_bf(st))
            st = st * dec_s[pl.ds(g * gs + j, 1), :] + jnp.where(same_head, incr_s[:, j * w:(j + 1) * w], 0.0)
        put(oc0, oc1, rows, _dot_nt(expand(_bf(both(qd0, qd1, rows))), jnp.concatenate(states, axis=1)))
        incr_s[...] = incr_next
        return st

    lax.fori_loop(0, nsteps, scan, jnp.zeros((w, w), F32))

    split(qd0, oc0)
    split(qd1, oc1)
    split(kd0, g0_ref)
    split(kd1, g1_ref)
    for p in range(cc):
        o = oi_s[p] + both(qd0, qd1, rows_of4(p))
        ms = _group_sum(o * o, ones_bd) * (1.0 / HEAD_DIM)
        put(oc0, oc1, rows_of(p), o * lax.rsqrt(ms + NORM_EPS) * ng * _silu(both(kd0, kd1, rows_of4(p))))
    o_ref[:, 0:LANES] = oc0[...]
    o_ref[:, LANES:2 * LANES] = oc1[...]


def _hgrn(h, batch, seq, lb, ng):
    base = C_HGRN // LANES
    half = lambda j: pl.BlockSpec((seq, LANES), lambda b: (b, base + j))
    full = lambda a: pl.BlockSpec(a.shape, lambda b: (0, 0))
    nch = seq // HGRN_CHUNK
    tiles = pltpu.VMEM((HGRN_CHUNK, nch, GROUP_WIDTH), F32)
    nat = pltpu.VMEM((seq, LANES), F32)
    return pl.pallas_call(
        _hgrn_kernel,
        grid=(batch,),
        in_specs=[half(j) for j in range(8)] + [full(lb), full(ng)],
        out_specs=pl.BlockSpec((seq, GROUP_WIDTH), lambda b: (b, 0)),
        out_shape=jax.ShapeDtypeStruct((batch * seq, GROUP_WIDTH), F32),
        scratch_shapes=[tiles] * 5 + [nat] * 6 + [pltpu.VMEM((nch, GROUP_WIDTH), F32),
                                                  pltpu.VMEM((GROUP_WIDTH, HGRN_SCAN_UNROLL * GROUP_WIDTH), F32)],
        compiler_params=_params("parallel"),
        name="hgrn",
    )(h, h, h, h, h, h, h, h, lb, ng)


def _dsa_kernel(q_ref, k_ref, v_ref, o_ref, num_ref, m_ref, l_ref):
    seq = q_ref.shape[0]
    blk = DSA_BLOCK
    pair = pl.program_id(1)
    qi = _iota2((blk, 2 * blk), 0)
    ki = _iota2((blk, 2 * blk), 1)
    steps = blk + qi - ki
    lane = _iota2((blk, LANES), 1)
    first = lane < HEAD_DIM
    qscale = HEAD_DIM ** -0.5 * LOG2E
    slopes = [jnp.where(pair == 0, ALIBI_SLOPES[hh], ALIBI_SLOPES[2 + hh]) * LOG2E for hh in range(2)]

    for bi, (window, dil) in enumerate(DSA_BRANCHES):
        nblk = seq // (dil * blk)
        in_window = (steps >= 0) & (steps <= window // dil)
        dist = (steps * dil).astype(F32)
        two_blocks = nblk > 1
        if two_blocks:
            bias = [jnp.where(in_window, -slopes[hh] * dist, MASK_VALUE) for hh in range(2)]
            bias_first = [jnp.where(ki >= blk, bias[hh], MASK_VALUE) for hh in range(2)]
        else:
            bias_first = [jnp.where(in_window, -slopes[hh] * dist, MASK_VALUE)[:, blk:] for hh in range(2)]
            bias = bias_first

        def attend_group(blocks, bi=bi, dil=dil, bias=bias, bias_first=bias_first, two_blocks=two_blocks):
            if dil == 1:
                ld = lambda ref, s0: ref[pl.ds(s0, blk), :]
                dst = lambda s0: pl.ds(bi * seq + s0, blk)
            else:
                ld = lambda ref, s0: ref[pl.ds(s0, blk, stride=dil), :]
                dst = lambda s0: pl.ds(bi * seq + s0, blk, stride=dil)
            tiles, vbs = [], []
            for start, is_first in blocks:
                qb = ld(q_ref, start) * qscale
                if two_blocks:
                    prev = start if is_first else start - dil * blk
                    kb = _bf(jnp.concatenate([ld(k_ref, prev), ld(k_ref, start)], axis=0))
                    vb = _bf(jnp.concatenate([ld(v_ref, prev), ld(v_ref, start)], axis=0))
                else:
                    kb = _bf(ld(k_ref, start))
                    vb = _bf(ld(v_ref, start))
                vbs.append(jnp.concatenate([vb, jnp.ones(vb.shape, BF16)], axis=1))
                for hh in range(2):
                    qm = _bf(jnp.where(first if hh == 0 else jnp.logical_not(first), qb, 0.0))
                    tiles.append(_dot_nt(qm, kb) + (bias_first if is_first else bias)[hh])
            s = jnp.concatenate(tiles, axis=0)
            m = jnp.max(s, axis=-1, keepdims=True)
            pb = _bf(jnp.exp2(s - m))
            for i, (start, _) in enumerate(blocks):
                r0, r1, r2 = 2 * i * blk, (2 * i + 1) * blk, (2 * i + 2) * blk
                pv = jnp.where(jnp.concatenate([first, first], axis=1), _dot(pb[r0:r1], vbs[i]), _dot(pb[r1:r2], vbs[i]))
                num_ref[dst(start), :] = pv[:, 0:LANES]
                l_ref[dst(start), :] = pv[:, LANES:2 * LANES]
                m_ref[dst(start), :] = jnp.where(first, m[r0:r1], m[r1:r2])

        group = DSA_GROUP
        if nblk == 1:
            def body(g, carry, attend_group=attend_group):
                attend_group([(g * group + j, True) for j in range(group)])
                return carry
            lax.fori_loop(0, dil // group, body, 0)
        elif nblk < group:
            per = group // nblk

            def body(g, carry, attend_group=attend_group, dil=dil, nblk=nblk, per=per):
                attend_group([(g * per + j + dil * blk * n, n == 0) for j in range(per) for n in range(nblk)])
                return carry
            lax.fori_loop(0, dil // per, body, 0)
        else:
            assert dil == 1 and nblk % group == 0
            attend_group([(n * blk, n == 0) for n in range(group)])

            def body(g, carry, attend_group=attend_group):
                attend_group([(pl.multiple_of((g * group + j) * blk, blk), False) for j in range(group)])
                return carry
            lax.fori_loop(1, nblk // group, body, 0)

    def merge(n, carry):
        r0 = pl.multiple_of(n * blk, blk)
        rows = [pl.ds(bi * seq + r0, blk) for bi in range(len(DSA_BRANCHES))]
        m0, m1, m2 = [m_ref[r, :] for r in rows]
        mm = jnp.maximum(jnp.maximum(m0, m1), m2)
        w0, w1, w2 = jnp.exp2(m0 - mm), jnp.exp2(m1 - mm), jnp.exp2(m2 - mm)
        num = w0 * num_ref[rows[0], :] + w1 * num_ref[rows[1], :] + w2 * num_ref[rows[2], :]
        den = w0 * l_ref[rows[0], :] + w1 * l_ref[rows[1], :] + w2 * l_ref[rows[2], :]
        o_ref[pl.ds(r0, blk), :] = num / den
        return carry

    lax.fori_loop(0, seq // blk, merge, 0)


def _dsa(h, batch, seq):
    base = C_DSA // LANES
    blk = lambda j: pl.BlockSpec((seq, LANES), lambda b, p: (b, base + 2 * j + p))
    nb = len(DSA_BRANCHES)
    return pl.pallas_call(
        _dsa_kernel,
        grid=(batch, 2),
        in_specs=[blk(0), blk(1), blk(2)],
        out_specs=pl.BlockSpec((seq, LANES), lambda b, p: (b, p)),
        out_shape=jax.ShapeDtypeStruct((batch * seq, GROUP_WIDTH), F32),
        scratch_shapes=[pltpu.VMEM((nb * seq, LANES), F32), pltpu.VMEM((nb * seq, LANES), F32),
                        pltpu.VMEM((nb * seq, LANES), F32)],
        compiler_params=_params("parallel", "parallel"),
        name="dsa",
    )(h, h, h)


FF_CHUNK = 1024


def _post_kernel(oa_ref, ob_ref, oc_ref, od_ref, x_ref, wo_ref, g1_ref, b1_ref, w1_ref, w2_ref, g2_ref, b2_ref,
                 y_ref):
    gw = GROUP_WIDTH
    mixed = _dot(_bf(oa_ref[...]), wo_ref[0:gw, :])
    mixed = mixed + _dot(_bf(ob_ref[...]), wo_ref[gw:2 * gw, :])
    mixed = mixed + _dot(_bf(oc_ref[...]), wo_ref[2 * gw:3 * gw, :])
    mixed = mixed + _dot(_bf(od_ref[...]), wo_ref[3 * gw:4 * gw, :])
    x = _layer_norm_rows(DEEPNORM_ALPHA * x_ref[...] + mixed, g1_ref[...], b1_ref[...])
    xb = _bf(x)
    acc = jnp.zeros(x.shape, F32)
    for c in range(D_FF // FF_CHUNK):
        hmid = _dot(xb, w1_ref[:, c * FF_CHUNK:(c + 1) * FF_CHUNK])
        hmid = jnp.square(jnp.maximum(hmid, 0.0))
        acc = acc + _dot(_bf(hmid), w2_ref[c * FF_CHUNK:(c + 1) * FF_CHUNK, :])
    y_ref[...] = _layer_norm_rows(DEEPNORM_ALPHA * x + acc, g2_ref[...], b2_ref[...])


def _post(oa, ob, oc, od, x2d, wo, g1, b1, w1, w2, g2, b2):
    n = x2d.shape[0]
    tm = ROW_TILE
    grp = pl.BlockSpec((tm, GROUP_WIDTH), lambda i: (i, 0))
    row = pl.BlockSpec((tm, D_MODEL), lambda i: (i, 0))
    resident = lambda a: pl.BlockSpec(a.shape, lambda i: (0, 0), pipeline_mode=pl.Buffered(1))
    return pl.pallas_call(
        _post_kernel,
        grid=(n // tm,),
        in_specs=[grp, grp, grp, grp, row, resident(wo), resident(g1), resident(b1),
                  resident(w1), resident(w2), resident(g2), resident(b2)],
        out_specs=row,
        out_shape=jax.ShapeDtypeStruct((n, D_MODEL), F32),
        compiler_params=_params("parallel"),
        name="post",
    )(oa, ob, oc, od, x2d, wo, g1, b1, w1, w2, g2, b2)


def _expand_heads(wcols):
    return jnp.repeat(wcols, HEAD_DIM, axis=-1)


def _arrange_w_in(w):
    pts = np.cumsum(IN_SIZES)[:-1].tolist()
    (a_qkv, a_a, a_b, a_z, b_cq, b_ckv, b_kr, c_q, c_f, c_i, c_g, d_qkv) = jnp.split(_bf(w), pts, axis=-1)
    z = lambda n: jnp.zeros(w.shape[:-1] + (n,), BF16)
    half = MLA_ROPE // 2
    kr_sw = jnp.concatenate([b_kr[..., half:], b_kr[..., :half]], axis=-1)
    cols = [a_qkv, _expand_heads(a_a), _expand_heads(a_b), a_z,
            b_cq, z(256 - MLA_Q_RANK), b_ckv,
            z(MLA_NOPE), b_kr, z(LANES - MLA_NOPE - MLA_ROPE),
            z(MLA_NOPE), kr_sw, z(LANES - MLA_NOPE - MLA_ROPE), z(C_HGRN - C_MLA_KRS - LANES),
            c_q, c_f, c_i, c_g, d_qkv]
    return jnp.concatenate(cols, axis=-1)


def _arrange_mla_weights(w_uq, w_ukv):
    lead = w_uq.shape[:-2]
    rq = w_uq.shape[-2]
    wq = _bf(w_uq).reshape(lead + (rq, N_HEADS, MLA_NOPE + MLA_ROPE))
    half = MLA_ROPE // 2
    zq = jnp.zeros(lead + (rq, N_HEADS, LANES - MLA_NOPE - MLA_ROPE), BF16)
    q_main = jnp.concatenate([wq, zq], axis=-1)
    rope = wq[..., MLA_NOPE:]
    rope_sw = jnp.concatenate([rope[..., half:], rope[..., :half]], axis=-1)
    q_swap = jnp.concatenate([jnp.zeros(lead + (rq, N_HEADS, MLA_NOPE), BF16), rope_sw, zq], axis=-1)
    pad_rows = lambda m: jnp.concatenate([m, jnp.zeros(lead + (256 - rq, m.shape[-1]), BF16)], axis=-2)
    q_main = pad_rows(q_main.reshape(lead + (rq, N_HEADS * LANES)))
    q_swap = pad_rows(q_swap.reshape(lead + (rq, N_HEADS * LANES)))
    rkv = w_ukv.shape[-2]
    wkv = _bf(w_ukv).reshape(lead + (rkv, N_HEADS, MLA_NOPE + HEAD_DIM))
    k_w = jnp.concatenate([wkv[..., :MLA_NOPE], jnp.zeros(lead + (rkv, N_HEADS, LANES - MLA_NOPE), BF16)], axis=-1)
    v_w = wkv[..., MLA_NOPE:]
    return (q_main, q_swap, k_w.reshape(lead + (rkv, N_HEADS * LANES)),
            v_w.reshape(lead + (rkv, N_HEADS * HEAD_DIM)))


def _rope_tables(seq):
    half = MLA_ROPE // 2
    pos = jnp.arange(seq, dtype=F32)
    inv_freq = ROPE_BASE ** (-jnp.arange(half, dtype=F32) / half)
    ang = pos[:, None] * inv_freq[None, :]
    cos, sin = jnp.cos(ang), jnp.sin(ang)
    ones = jnp.ones((seq, MLA_NOPE), F32)
    zeros = jnp.zeros((seq, MLA_NOPE), F32)
    tail1 = jnp.ones((seq, LANES - MLA_NOPE - MLA_ROPE), F32)
    tail0 = jnp.zeros((seq, LANES - MLA_NOPE - MLA_ROPE), F32)
    cos_t = jnp.concatenate([ones, cos, cos, tail1], axis=1)
    sin_t = jnp.concatenate([zeros, -sin, sin, tail0], axis=1)
    return cos_t, sin_t


def _tile_heads(vec):
    return jnp.tile(vec.astype(F32), N_HEADS)[..., None, :]


def kernel(x, w_in, gdn_conv_w, gdn_a_log, gdn_dt_bias, gdn_norm_g, mla_q_norm_g, mla_kv_norm_g,
           mla_w_uq, mla_w_ukv, hgrn_lb_logits, hgrn_norm_g, w_out, ln1_g, ln1_b, w_ff1, w_ff2,
           ln2_g, ln2_b):
    batch, seq, d_model = x.shape
    assert d_model == D_MODEL and seq % (16 * DSA_BLOCK) == 0 and (batch * seq) % ROW_TILE == 0
    x2d = x.reshape(batch * seq, d_model)
    p_lb = jax.nn.softmax(hgrn_lb_logits.astype(F32), axis=0)
    lower_bounds = (jnp.cumsum(p_lb, axis=0) - p_lb[:1])[:, None, :]
    cos_t, sin_t = _rope_tables(seq)
    w_in_a = _arrange_w_in(w_in)
    avec = jnp.repeat(-jnp.exp(gdn_a_log.astype(F32)), HEAD_DIM, axis=-1)[:, None, :]
    dtb = jnp.repeat(gdn_dt_bias.astype(F32), HEAD_DIM, axis=-1)[:, None, :]
    conv_w = gdn_conv_w.astype(F32)
    gdn_g = _tile_heads(gdn_norm_g)
    hgrn_g = _tile_heads(hgrn_norm_g)
    wq, wqs, wk, wv = _arrange_mla_weights(mla_w_uq, mla_w_ukv)
    qg = jnp.concatenate([mla_q_norm_g.astype(F32), jnp.zeros((DEPTH, 256 - MLA_Q_RANK), F32)], axis=-1)[:, None, :]
    kvg = mla_kv_norm_g.astype(F32)[:, None, :]
    wo, w1, w2 = _bf(w_out), _bf(w_ff1), _bf(w_ff2)
    g1, b1, g2, b2 = (t.astype(F32)[:, None, :] for t in (ln1_g, ln1_b, ln2_g, ln2_b))
    for l in range(DEPTH):
        h = _inproj(x2d, w_in_a[l])
        o_a = _gdn(h, batch, seq, conv_w[l], avec[l], dtb[l], gdn_g[l])
        q_m, k_m, v_m = _mla_prep(h, seq, cos_t, sin_t, qg[l], kvg[l], wq[l], wqs[l], wk[l], wv[l])
        o_b = _mla_attn(q_m, k_m, v_m, batch, seq)
        o_c = _hgrn(h, batch, seq, lower_bounds[l], hgrn_g[l])
        o_d = _dsa(h, batch, seq)
        x2d = _post(o_a, o_b, o_c, o_d, x2d, wo[l], g1[l], b1[l], w1[l], w2[l], g2[l], b2[l])
    return x2d.reshape(batch, seq, d_model)
```

```python
import functools
import math

import numpy as np
import jax
import jax.numpy as jnp
from jax import lax
from jax.experimental import pallas as pl
from jax.experimental.pallas import tpu as pltpu

F32 = jnp.float32
BF16 = jnp.bfloat16

D_MODEL = 1024
DEPTH = 2
GROUP_WIDTH = 256
N_HEADS = 4
HEAD_DIM = 64
GDN_CONV = 4
GDN_CHUNK = 64
MLA_NOPE = 64
MLA_ROPE = 32
MLA_Q_RANK = 192
MLA_KV_RANK = 128
ROPE_BASE = 10000.0
HGRN_CHUNK = 16
DSA_BRANCHES = ((128, 1), (512, 4), (2048, 16))
DSA_BLOCK = 128
DSA_GROUP = 8
ALIBI_SLOPES = tuple(2.0 ** (-8.0 * (j + 1) / N_HEADS) for j in range(N_HEADS))
D_FF = 4 * D_MODEL
DEEPNORM_ALPHA = (2 * DEPTH) ** 0.25
NORM_EPS = 1e-6
MASK_VALUE = -1e30
LOG2E = 1.4426950408889634

LANES = 128
SUBLANES = 8
VMEM_LIMIT_BYTES = 56 * 1024 * 1024

C_GDN_QKV = 0
C_GDN_A = 768
C_GDN_B = 1024
C_GDN_Z = 1280
C_MLA_CQ = 1536
C_MLA_CKV = 1792
C_MLA_KR = 1920
C_MLA_KRS = 2048
C_HGRN = 2304
C_DSA = 3328
C_TOTAL = 4096
IN_SIZES = (768, 4, 4, 256, 192, 128, 32, 256, 256, 256, 256, 768)
ROW_TILE = 512


def _bf(x):
    return x.astype(BF16)


def _dot(a, b):
    return jnp.dot(a, b, preferred_element_type=F32)


def _dot_nt(a, b):
    return lax.dot_general(a, b, (((1,), (1,)), ((), ())), preferred_element_type=F32)


def _dot_tn(a, b):
    return lax.dot_general(a, b, (((0,), (0,)), ((), ())), preferred_element_type=F32)


def _split3(x):
    hi = _bf(x)
    r1 = x - hi.astype(F32)
    mid = _bf(r1)
    lo = _bf(r1 - mid.astype(F32))
    return hi, mid, lo


def _dot_sel_r(x, sel):
    hi, mid, lo = _split3(x)
    return _dot(hi, sel) + _dot(mid, sel) + _dot(lo, sel)


def _dot_sel_l(sel, x):
    hi, mid, lo = _split3(x)
    return _dot(sel, hi) + _dot(sel, mid) + _dot(sel, lo)


def _sigmoid(x):
    return 0.5 + 0.5 * jnp.tanh(0.5 * x)


def _silu(x):
    return x * _sigmoid(x)


def _softplus(x):
    return jnp.maximum(x, 0.0) + jnp.log(1.0 + jnp.exp(-jnp.abs(x)))


def _iota2(shape, axis):
    return lax.broadcasted_iota(jnp.int32, shape, axis)


def _head_block_ones(n):
    r = _iota2((n, n), 0) >> 6
    c = _iota2((n, n), 1) >> 6
    return jnp.where(r == c, 1.0, 0.0).astype(BF16)


def _group_sum(x, ones_bd):
    return _dot(_bf(x), ones_bd)


def _layer_norm_rows(y, g, b):
    mu = jnp.mean(y, axis=-1, keepdims=True)
    d = y - mu
    var = jnp.mean(d * d, axis=-1, keepdims=True)
    return d * lax.rsqrt(var + NORM_EPS) * g + b


def _params(*sem):
    return pltpu.CompilerParams(dimension_semantics=sem, vmem_limit_bytes=VMEM_LIMIT_BYTES)


def _inproj_kernel(x_ref, w_ref, o_ref):
    xb = _bf(x_ref[...])
    for c in range(C_TOTAL // 512):
        o_ref[:, c * 512:(c + 1) * 512] = _dot(xb, w_ref[:, c * 512:(c + 1) * 512])


def _layer_block(stacked, layer, **kw):
    return pl.BlockSpec((None,) + stacked.shape[1:], lambda *_: (layer, 0, 0), **kw)


def _inproj(x2d, w_layers, layer):
    n = x2d.shape[0]
    return pl.pallas_call(
        _inproj_kernel,
        grid=(n // ROW_TILE,),
        in_specs=[pl.BlockSpec((ROW_TILE, D_MODEL), lambda i: (i, 0)), _layer_block(w_layers, layer)],
        out_specs=pl.BlockSpec((ROW_TILE, C_TOTAL), lambda i: (i, 0)),
        out_shape=jax.ShapeDtypeStruct((n, C_TOTAL), F32),
        compiler_params=_params("parallel"),
        name="inproj",
    )(x2d, w_layers)


GDN_GROUP = 8


def _gdn_kernel(qkv_ref, a_ref, b_ref, z_ref, convw_ref, avec_ref, dtb_ref, ng_ref, o_ref,
                s_ref, u_ref, egl_ref, ku_ref, w_ref, qd_ref, qk_ref, kw_ref):
    seq = qkv_ref.shape[0]
    c = GDN_CHUNK
    w = GROUP_WIDTH
    rows_per_step = GDN_GROUP * c
    s_ref[...] = jnp.zeros_like(s_ref)

    same_head = (_iota2((w, w), 0) >> 6) == (_iota2((w, w), 1) >> 6)
    ones_bd = jnp.where(same_head, 1.0, 0.0).astype(BF16)
    rr = _iota2((rows_per_step, rows_per_step), 0)
    rc = _iota2((rows_per_step, rows_per_step), 1)
    same_chunk = (rr >> 6) == (rc >> 6)
    tril_chunks = jnp.where(same_chunk & (rc <= rr), 1.0, 0.0).astype(BF16)
    ones_chunks = jnp.where(same_chunk, 1.0, 0.0).astype(BF16)
    li = _iota2((rows_per_step, w), 0) & (c - 1)
    lj = _iota2((rows_per_step, w), 1) & (c - 1)
    causal = lj <= li
    strict = lj < li
    eye = lj == li
    lane_head = _iota2((c, w), 1) >> 6
    first_rows = _iota2((SUBLANES, 3 * w), 0)

    def stack_heads(t):
        return jnp.concatenate([jnp.where(lane_head == h, t, jnp.zeros_like(t)) for h in range(N_HEADS)], axis=0)

    def block_diag(t):
        return jnp.where(same_head, jnp.concatenate([t] * N_HEADS, axis=0), jnp.zeros((w, w), t.dtype))

    convw = convw_ref[...]
    avec = avec_ref[...]
    dtb = dtb_ref[...]
    ng = ng_ref[...]
    chunks = [slice(i * c, (i + 1) * c) for i in range(GDN_GROUP)]

    def prep(n, carry):
        r0 = pl.multiple_of(n * rows_per_step, rows_per_step)
        rows = pl.ds(r0, rows_per_step)
        cur = qkv_ref[rows, :]
        prev = qkv_ref[pl.ds(pl.multiple_of(jnp.maximum(r0 - SUBLANES, 0), SUBLANES), SUBLANES), :]
        prev = jnp.where(n > 0, prev, 0.0)
        y = cur * convw[GDN_CONV - 1:GDN_CONV, :]
        for j in range(1, GDN_CONV):
            shifted = pltpu.roll(cur, j, 0)
            head = jnp.where(first_rows < j, pltpu.roll(prev, j, 0), shifted[0:SUBLANES, :])
            shifted = jnp.concatenate([head, shifted[SUBLANES:, :]], axis=0)
            y = y + shifted * convw[GDN_CONV - 1 - j:GDN_CONV - j, :]
        y = _silu(y)
        q = y[:, 0:w]
        k = y[:, w:2 * w]
        v = y[:, 2 * w:3 * w]
        q = q * lax.rsqrt(_group_sum(q * q, ones_bd) + NORM_EPS) * (HEAD_DIM ** -0.5)
        k = k * lax.rsqrt(_group_sum(k * k, ones_bd) + NORM_EPS)
        beta = _sigmoid(b_ref[rows, :])
        gstep = avec * _softplus(a_ref[rows, :] + dtb)
        g = _dot_sel_l(tril_chunks, gstep)
        g_last = jnp.concatenate([jnp.broadcast_to(g[sl][c - 1:c, :], (c, w)) for sl in chunks], axis=0)
        gr = _dot_sel_l(ones_chunks, jnp.where(eye, g, 0.0))
        decay = jnp.where(causal, jnp.exp(jnp.where(causal, g - gr, 0.0)), 0.0)
        eg = jnp.exp(g)
        kb = k * beta
        kbb = _bf(kb)
        qb = _bf(q)
        prods = [_dot_nt(jnp.concatenate([kbb[sl], qb[sl]], axis=0), _bf(stack_heads(k[sl]))) for sl in chunks]
        lower = jnp.where(strict, jnp.concatenate([p[0:c] for p in prods], axis=0) * decay, 0.0)
        qk = jnp.concatenate([p[c:2 * c] for p in prods], axis=0) * decay
        m = -lower
        t = jnp.where(eye, 1.0, 0.0) + m
        for level in range(6):
            mb = _bf(m)
            tb = _bf(t)
            new_m, t_m = [], []
            for sl in chunks:
                m_bd = block_diag(mb[sl])
                if level == 0:
                    new_m.append(_dot(mb[sl], m_bd))
                elif level < 5:
                    both = _dot(jnp.concatenate([mb[sl], tb[sl]], axis=0), m_bd)
                    new_m.append(both[0:c])
                    t_m.append(both[c:2 * c])
                else:
                    t_m.append(_dot(tb[sl], m_bd))
            if level < 5:
                m = jnp.concatenate(new_m, axis=0)
            if level > 0:
                t = t + jnp.concatenate(t_m, axis=0)
        tb = _bf(t)
        vb = v * beta
        kbg = kb * eg
        uw = [_dot(tb[sl], _bf(jnp.concatenate([stack_heads(vb[sl]), stack_heads(kbg[sl])], axis=1)))
              for sl in chunks]
        u_ref[rows, :] = jnp.concatenate([x[:, 0:w] for x in uw], axis=0)
        w_ref[rows, :] = _bf(jnp.concatenate([x[:, w:2 * w] for x in uw], axis=0))
        qd_ref[rows, :] = _bf(q * eg)
        qk_ref[rows, :] = _bf(qk)
        egl_ref[rows, :] = jnp.exp(g_last)
        k_dec = _bf(k * jnp.exp(g_last - g))
        for ci, sl in enumerate(chunks):
            kw_ku = _dot_tn(k_dec[sl], _bf(uw[ci]))
            mat_rows = pl.ds(pl.multiple_of((n * GDN_GROUP + ci) * w, w), w)
            ku_ref[mat_rows, :] = kw_ku[:, 0:w]
            kw_ref[mat_rows, :] = _bf(kw_ku[:, w:2 * w])
        return carry

    lax.fori_loop(0, seq // rows_per_step, prep, 0)

    def scan(n, carry):
        step0 = pl.multiple_of(n * rows_per_step, rows_per_step)
        for ci in range(GDN_GROUP):
            rows = pl.ds(step0 + ci * c, c)
            mat_rows = pl.ds(pl.multiple_of((n * GDN_GROUP + ci) * w, w), w)
            s_bd = s_ref[...]
            lhs = jnp.concatenate([kw_ref[mat_rows, :], w_ref[rows, :], qd_ref[rows, :]], axis=0)
            r = _dot(lhs, _bf(s_bd))
            s_ref[...] = s_bd * egl_ref[pl.ds(step0 + ci * c, 1), :] + jnp.where(
                same_head, ku_ref[mat_rows, :] - r[0:w], 0.0)
            v_new = u_ref[rows, :] - r[w:w + c]
            o_ref[rows, :] = r[w + c:w + 2 * c] + _dot(qk_ref[rows, :], _bf(stack_heads(v_new)))
        return carry

    lax.fori_loop(0, seq // rows_per_step, scan, 0)

    def readout(n, carry):
        rows = pl.ds(pl.multiple_of(n * rows_per_step, rows_per_step), rows_per_step)
        o = o_ref[rows, :]
        ms = _group_sum(o * o, ones_bd) * (1.0 / HEAD_DIM)
        o_ref[rows, :] = o * lax.rsqrt(ms + NORM_EPS) * ng * _silu(z_ref[rows, :])
        return carry

    lax.fori_loop(0, seq // rows_per_step, readout, 0)


def _gdn(h, batch, seq, convw, avec, dtb, ng):
    blk = lambda width, cblk: pl.BlockSpec((seq, width), lambda b: (b, cblk))
    full = lambda a: pl.BlockSpec(a.shape, lambda b: (0, 0))
    return pl.pallas_call(
        _gdn_kernel,
        grid=(batch,),
        in_specs=[blk(768, C_GDN_QKV // 768), blk(256, C_GDN_A // 256), blk(256, C_GDN_B // 256),
                  blk(256, C_GDN_Z // 256), full(convw), full(avec), full(dtb), full(ng)],
        out_specs=pl.BlockSpec((seq, GROUP_WIDTH), lambda b: (b, 0)),
        out_shape=jax.ShapeDtypeStruct((batch * seq, GROUP_WIDTH), F32),
        scratch_shapes=[pltpu.VMEM((GROUP_WIDTH, GROUP_WIDTH), F32),
                        pltpu.VMEM((seq, GROUP_WIDTH), F32), pltpu.VMEM((seq, GROUP_WIDTH), F32),
                        pltpu.VMEM((seq // GDN_CHUNK * GROUP_WIDTH, GROUP_WIDTH), F32),
                        pltpu.VMEM((seq, GROUP_WIDTH), BF16), pltpu.VMEM((seq, GROUP_WIDTH), BF16),
                        pltpu.VMEM((seq, GROUP_WIDTH), BF16),
                        pltpu.VMEM((seq // GDN_CHUNK * GROUP_WIDTH, GROUP_WIDTH), BF16)],
        compiler_params=_params("parallel"),
        name="gdn",
    )(h, h, h, h, convw, avec, dtb, ng)


MLA_Q_BLOCK = 256


def _mla_kernel(cq_ref, ckv_ref, kr_ref, krs_ref, cos_ref, sin_ref, qg_ref, kvg_ref,
                wq_ref, wqs_ref, wk_ref, wv_ref, o_ref, q_ref, k_ref, v_ref):
    seq = cq_ref.shape[0]
    tm = ROW_TILE
    scale = (MLA_NOPE + MLA_ROPE) ** -0.5 * LOG2E

    def prep(i, carry):
        rows = pl.ds(pl.multiple_of(i * tm, tm), tm)
        cq = cq_ref[rows, :]
        nq = cq * lax.rsqrt(jnp.sum(cq * cq, axis=-1, keepdims=True) * (1.0 / MLA_Q_RANK) + NORM_EPS) * qg_ref[...]
        nqb = _bf(nq)
        cos1 = cos_ref[rows, :]
        sin1 = sin_ref[rows, :]
        cos2 = jnp.concatenate([cos1, cos1], axis=1)
        sin2 = jnp.concatenate([sin1, sin1], axis=1)
        q_ref[rows, :] = _bf((_dot(nqb, wq_ref[...]) * cos2 + _dot(nqb, wqs_ref[...]) * sin2) * scale)
        ckv = ckv_ref[rows, :]
        nkv = ckv * lax.rsqrt(jnp.mean(ckv * ckv, axis=-1, keepdims=True) + NORM_EPS) * kvg_ref[...]
        nkvb = _bf(nkv)
        kr = kr_ref[rows, :] * cos1 + krs_ref[rows, :] * sin1
        k_ref[rows, :] = _bf(_dot(nkvb, wk_ref[...]) + jnp.concatenate([kr, kr], axis=1))
        v = _bf(_dot(nkvb, wv_ref[...]))
        v_ref[rows, :] = jnp.concatenate([v, jnp.ones((tm, LANES), BF16)], axis=1)
        return carry

    lax.fori_loop(0, seq // tm, prep, 0)

    tq = MLA_Q_BLOCK
    ri = _iota2((tq, tq), 0)
    ci = _iota2((tq, tq), 1)
    diag_ok = ci <= ri
    lane = _iota2((tq, LANES), 1)
    for qi in range(seq // tq):
        q0 = qi * tq
        v_off = v_ref[0:q0, :] if qi else None
        v_diag = v_ref[q0:q0 + tq, :]
        outs = []
        for hh in range(2):
            qh = q_ref[q0:q0 + tq, hh * LANES:(hh + 1) * LANES]
            s_diag = _dot_nt(qh, k_ref[q0:q0 + tq, hh * LANES:(hh + 1) * LANES])
            s_diag = jnp.where(diag_ok, s_diag, MASK_VALUE)
            m = jnp.max(s_diag, axis=-1, keepdims=True)
            if qi:
                s_off = _dot_nt(qh, k_ref[0:q0, hh * LANES:(hh + 1) * LANES])
                m = jnp.maximum(m, jnp.max(s_off, axis=-1, keepdims=True))
            acc = _dot(_bf(jnp.exp2(s_diag - m)), v_diag)
            if qi:
                acc = acc + _dot(_bf(jnp.exp2(s_off - m)), v_off)
            outs.append(acc[:, 0:LANES] / acc[:, LANES:2 * LANES])
        o_ref[q0:q0 + tq, :] = jnp.where(lane < HEAD_DIM, outs[0], outs[1])


def _mla(h, batch, seq, cos_t, sin_t, qg, kvg, wq, wqs, wk, wv):
    col = lambda width, off: pl.BlockSpec((seq, width), lambda b, p: (b, off // width))
    table = pl.BlockSpec((seq, LANES), lambda b, p: (0, 0))
    full = lambda a: pl.BlockSpec(a.shape, lambda b, p: (0, 0))
    pair_cols = lambda a, width: pl.BlockSpec((a.shape[0], width), lambda b, p: (0, p))
    return pl.pallas_call(
        _mla_kernel,
        grid=(batch, 2),
        in_specs=[col(256, C_MLA_CQ), col(LANES, C_MLA_CKV), col(LANES, C_MLA_KR), col(LANES, C_MLA_KRS),
                  table, table, full(qg), full(kvg),
                  pair_cols(wq, 256), pair_cols(wqs, 256), pair_cols(wk, 256), pair_cols(wv, LANES)],
        out_specs=pl.BlockSpec((seq, LANES), lambda b, p: (b, p)),
        out_shape=jax.ShapeDtypeStruct((batch * seq, GROUP_WIDTH), F32),
        scratch_shapes=[pltpu.VMEM((seq, 256), BF16)] * 3,
        compiler_params=_params("parallel", "parallel"),
        name="mla",
    )(h, h, h, h, cos_t, sin_t, qg, kvg, wq, wqs, wk, wv)


HGRN_SCAN_UNROLL = 8
HGRN_SUBTILE = 32


def _hgrn_kernel(q0_ref, q1_ref, f0_ref, f1_ref, i0_ref, i1_ref, g0_ref, g1_ref, lb_ref, ng_ref, o_ref,
                 q_s, kk_s, cum_s, v_s, oi_s, qd0, qd1, kd0, kd1, oc0, oc1, dec_s, incr_s):
    seq = q0_ref.shape[0]
    cc = HGRN_CHUNK
    nch = seq // cc
    w = GROUP_WIDTH
    same_head = (_iota2((w, w), 0) >> 6) == (_iota2((w, w), 1) >> 6)
    ones_bd = jnp.where(same_head, 1.0, 0.0).astype(BF16)
    lb = lb_ref[...]
    ng = ng_ref[...]
    one_m_lb = 1.0 - lb

    def rows_of(p):
        return pl.ds(p, nch, stride=cc)

    def both(ref0, ref1, rows):
        return jnp.concatenate([ref0[rows, :], ref1[rows, :]], axis=1)

    def put(ref0, ref1, rows, val):
        ref0[rows, :] = val[:, 0:LANES]
        ref1[rows, :] = val[:, LANES:2 * LANES]

    quarter = seq // 4

    def split(dst, src):
        for r in range(4):
            dst[r * quarter:(r + 1) * quarter, :] = src[pl.ds(r, quarter, stride=4), :]

    def rows_of4(p):
        return pl.ds((p % 4) * quarter + p // 4, nch, stride=4)

    split(oc0, f0_ref)
    split(oc1, f1_ref)
    split(qd0, q0_ref)
    split(qd1, q1_ref)
    split(kd0, i0_ref)
    split(kd1, i1_ref)

    cum = None
    for p in range(cc):
        fl = both(oc0, oc1, rows_of4(p))
        z = jnp.exp(-jnp.abs(fl))
        r = 1.0 / (1.0 + z)
        zr = z * r
        pos = fl >= 0.0
        log_f = jnp.log(lb + one_m_lb * jnp.where(pos, r, zr))
        cum = log_f if p == 0 else cum + log_f
        q_s[p] = both(qd0, qd1, rows_of4(p))
        kk_s[p] = one_m_lb * jnp.where(pos, zr, r)
        cum_s[p] = cum
        v_s[p] = both(kd0, kd1, rows_of4(p))
    dec_s[...] = jnp.exp(cum)
    for p in range(cc):
        c_p = cum_s[p]
        put(qd0, qd1, rows_of(p), q_s[p] * jnp.exp(c_p))
        put(kd0, kd1, rows_of(p), kk_s[p] * jnp.exp(cum_s[cc - 1] - c_p))

    sub = HGRN_SUBTILE

    def intra(t, carry):
        rs = pl.ds(pl.multiple_of(t * sub, sub), sub)
        for p in range(cc):
            q_p = q_s[p, rs, :]
            c_p = cum_s[p, rs, :]
            xs = [_bf(q_p * kk_s[s, rs, :] * jnp.exp(c_p - cum_s[s, rs, :])) for s in range(p)]
            xs.append(_bf(q_p * kk_s[p, rs, :]))
            sums = _dot(jnp.concatenate(xs, axis=0), ones_bd)
            acc = sums[p * sub:(p + 1) * sub] * v_s[p, rs, :]
            for s in range(p):
                acc = acc + sums[s * sub:(s + 1) * sub] * v_s[s, rs, :]
            oi_s[p, rs, :] = acc
        return carry

    lax.fori_loop(0, nch // sub, intra, 0)

    gs = HGRN_SCAN_UNROLL
    blk_rows = gs * cc
    chunk_of_row = _iota2((blk_rows, w), 0) >> 4
    chunk_masks = [jnp.where(chunk_of_row == j, 1.0, 0.0).astype(BF16) for j in range(gs)]

    def expand(t):
        return jnp.concatenate([t * chunk_masks[j] for j in range(gs)], axis=1)

    def increments(g):
        rows = pl.ds(pl.multiple_of(g * blk_rows, blk_rows), blk_rows)
        return _dot_tn(_bf(both(i0_ref, i1_ref, rows)), expand(_bf(both(kd0, kd1, rows))))

    nsteps = nch // gs
    incr_s[...] = increments(0)

    def scan(g, st):
        incr_next = increments(jnp.minimum(g + 1, nsteps - 1))
        rows = pl.ds(pl.multiple_of(g * blk_rows, blk_rows), blk_rows)
        states = []
        for j in range(gs):
            states.append(_bf(st))
            st = st * dec_s[pl.ds(g * gs + j, 1), :] + jnp.where(same_head, incr_s[:, j * w:(j + 1) * w], 0.0)
        put(oc0, oc1, rows, _dot_nt(expand(_bf(both(qd0, qd1, rows))), jnp.concatenate(states, axis=1)))
        incr_s[...] = incr_next
        return st

    lax.fori_loop(0, nsteps, scan, jnp.zeros((w, w), F32))

    split(qd0, oc0)
    split(qd1, oc1)
    split(kd0, g0_ref)
    split(kd1, g1_ref)
    for p in range(cc):
        o = oi_s[p] + both(qd0, qd1, rows_of4(p))
        ms = _group_sum(o * o, ones_bd) * (1.0 / HEAD_DIM)
        put(oc0, oc1, rows_of(p), o * lax.rsqrt(ms + NORM_EPS) * ng * _silu(both(kd0, kd1, rows_of4(p))))
    o_ref[:, 0:LANES] = oc0[...]
    o_ref[:, LANES:2 * LANES] = oc1[...]


def _hgrn(h, batch, seq, lb, ng):
    base = C_HGRN // LANES
    half = lambda j: pl.BlockSpec((seq, LANES), lambda b: (b, base + j))
    full = lambda a: pl.BlockSpec(a.shape, lambda b: (0, 0))
    nch = seq // HGRN_CHUNK
    tiles = pltpu.VMEM((HGRN_CHUNK, nch, GROUP_WIDTH), F32)
    nat = pltpu.VMEM((seq, LANES), F32)
    return pl.pallas_call(
        _hgrn_kernel,
        grid=(batch,),
        in_specs=[half(j) for j in range(8)] + [full(lb), full(ng)],
        out_specs=pl.BlockSpec((seq, GROUP_WIDTH), lambda b: (b, 0)),
        out_shape=jax.ShapeDtypeStruct((batch * seq, GROUP_WIDTH), F32),
        scratch_shapes=[tiles] * 5 + [nat] * 6 + [pltpu.VMEM((nch, GROUP_WIDTH), F32),
                                                  pltpu.VMEM((GROUP_WIDTH, HGRN_SCAN_UNROLL * GROUP_WIDTH), F32)],
        compiler_params=_params("parallel"),
        name="hgrn",
    )(h, h, h, h, h, h, h, h, lb, ng)


def _dsa_kernel(q_ref, k_ref, v_ref, o_ref, num_ref, m_ref, l_ref):
    seq = q_ref.shape[0]
    blk = DSA_BLOCK
    pair = pl.program_id(1)
    qi = _iota2((blk, 2 * blk), 0)
    ki = _iota2((blk, 2 * blk), 1)
    steps = blk + qi - ki
    lane = _iota2((blk, LANES), 1)
    first = lane < HEAD_DIM
    qscale = HEAD_DIM ** -0.5 * LOG2E
    slopes = [jnp.where(pair == 0, ALIBI_SLOPES[hh], ALIBI_SLOPES[2 + hh]) * LOG2E for hh in range(2)]

    for bi, (window, dil) in enumerate(DSA_BRANCHES):
        nblk = seq // (dil * blk)
        in_window = (steps >= 0) & (steps <= window // dil)
        dist = (steps * dil).astype(F32)
        two_blocks = nblk > 1
        if two_blocks:
            bias = [jnp.where(in_window, -slopes[hh] * dist, MASK_VALUE) for hh in range(2)]
            bias_first = [jnp.where(ki >= blk, bias[hh], MASK_VALUE) for hh in range(2)]
        else:
            bias_first = [jnp.where(in_window, -slopes[hh] * dist, MASK_VALUE)[:, blk:] for hh in range(2)]
            bias = bias_first

        def attend_group(blocks, bi=bi, dil=dil, bias=bias, bias_first=bias_first, two_blocks=two_blocks):
            if dil == 1:
                ld = lambda ref, s0: ref[pl.ds(s0, blk), :]
                dst = lambda s0: pl.ds(bi * seq + s0, blk)
            else:
                ld = lambda ref, s0: ref[pl.ds(s0, blk, stride=dil), :]
                dst = lambda s0: pl.ds(bi * seq + s0, blk, stride=dil)
            tiles, vbs = [], []
            for start, is_first in blocks:
                qb = ld(q_ref, start) * qscale
                if two_blocks:
                    prev = start if is_first else start - dil * blk
                    kb = _bf(jnp.concatenate([ld(k_ref, prev), ld(k_ref, start)], axis=0))
                    vb = _bf(jnp.concatenate([ld(v_ref, prev), ld(v_ref, start)], axis=0))
                else:
                    kb = _bf(ld(k_ref, start))
                    vb = _bf(ld(v_ref, start))
                vbs.append(jnp.concatenate([vb, jnp.ones(vb.shape, BF16)], axis=1))
                for hh in range(2):
                    qm = _bf(jnp.where(first if hh == 0 else jnp.logical_not(first), qb, 0.0))
                    tiles.append(_dot_nt(qm, kb) + (bias_first if is_first else bias)[hh])
            s = jnp.concatenate(tiles, axis=0)
            m = jnp.max(s, axis=-1, keepdims=True)
            pb = _bf(jnp.exp2(s - m))
            for i, (start, _) in enumerate(blocks):
                r0, r1, r2 = 2 * i * blk, (2 * i + 1) * blk, (2 * i + 2) * blk
                pv = jnp.where(jnp.concatenate([first, first], axis=1), _dot(pb[r0:r1], vbs[i]), _dot(pb[r1:r2], vbs[i]))
                num_ref[dst(start), :] = pv[:, 0:LANES]
                l_ref[dst(start), :] = pv[:, LANES:2 * LANES]
                m_ref[dst(start), :] = jnp.where(first, m[r0:r1], m[r1:r2])

        group = DSA_GROUP
        if nblk == 1:
            def body(g, carry, attend_group=attend_group):
                attend_group([(g * group + j, True) for j in range(group)])
                return carry
            lax.fori_loop(0, dil // group, body, 0)
        elif nblk < group:
            per = group // nblk

            def body(g, carry, attend_group=attend_group, dil=dil, nblk=nblk, per=per):
                attend_group([(g * per + j + dil * blk * n, n == 0) for j in range(per) for n in range(nblk)])
                return carry
            lax.fori_loop(0, dil // per, body, 0)
        else:
            assert dil == 1 and nblk % group == 0
            attend_group([(n * blk, n == 0) for n in range(group)])

            def body(g, carry, attend_group=attend_group):
                attend_group([(pl.multiple_of((g * group + j) * blk, blk), False) for j in range(group)])
                return carry
            lax.fori_loop(1, nblk // group, body, 0)

    def merge(n, carry):
        r0 = pl.multiple_of(n * blk, blk)
        rows = [pl.ds(bi * seq + r0, blk) for bi in range(len(DSA_BRANCHES))]
        m0, m1, m2 = [m_ref[r, :] for r in rows]
        mm = jnp.maximum(jnp.maximum(m0, m1), m2)
        w0, w1, w2 = jnp.exp2(m0 - mm), jnp.exp2(m1 - mm), jnp.exp2(m2 - mm)
        num = w0 * num_ref[rows[0], :] + w1 * num_ref[rows[1], :] + w2 * num_ref[rows[2], :]
        den = w0 * l_ref[rows[0], :] + w1 * l_ref[rows[1], :] + w2 * l_ref[rows[2], :]
        o_ref[pl.ds(r0, blk), :] = num / den
        return carry

    lax.fori_loop(0, seq // blk, merge, 0)


def _dsa(h, batch, seq):
    base = C_DSA // LANES
    blk = lambda j: pl.BlockSpec((seq, LANES), lambda b, p: (b, base + 2 * j + p))
    nb = len(DSA_BRANCHES)
    return pl.pallas_call(
        _dsa_kernel,
        grid=(batch, 2),
        in_specs=[blk(0), blk(1), blk(2)],
        out_specs=pl.BlockSpec((seq, LANES), lambda b, p: (b, p)),
        out_shape=jax.ShapeDtypeStruct((batch * seq, GROUP_WIDTH), F32),
        scratch_shapes=[pltpu.VMEM((nb * seq, LANES), F32), pltpu.VMEM((nb * seq, LANES), F32),
                        pltpu.VMEM((nb * seq, LANES), F32)],
        compiler_params=_params("parallel", "parallel"),
        name="dsa",
    )(h, h, h)


FF_CHUNK = 1024
POST_ROW_TILE = 512


def _post_kernel(oa_ref, ob_ref, oc_ref, od_ref, x_ref, wo_ref, g1_ref, b1_ref, w1_ref, w2_ref, g2_ref, b2_ref,
                 y_ref):
    gw = GROUP_WIDTH
    mixed = _dot(_bf(oa_ref[...]), wo_ref[0:gw, :])
    mixed = mixed + _dot(_bf(ob_ref[...]), wo_ref[gw:2 * gw, :])
    mixed = mixed + _dot(_bf(oc_ref[...]), wo_ref[2 * gw:3 * gw, :])
    mixed = mixed + _dot(_bf(od_ref[...]), wo_ref[3 * gw:4 * gw, :])
    x = _layer_norm_rows(DEEPNORM_ALPHA * x_ref[...] + mixed, g1_ref[...], b1_ref[...])
    xb = _bf(x)
    acc = jnp.zeros(x.shape, F32)
    for c in range(D_FF // FF_CHUNK):
        hmid = _dot(xb, w1_ref[:, c * FF_CHUNK:(c + 1) * FF_CHUNK])
        hmid = jnp.square(jnp.maximum(hmid, 0.0))
        acc = acc + _dot(_bf(hmid), w2_ref[c * FF_CHUNK:(c + 1) * FF_CHUNK, :])
    y_ref[...] = _layer_norm_rows(DEEPNORM_ALPHA * x + acc, g2_ref[...], b2_ref[...])


def _post(oa, ob, oc, od, x2d, wo, g1, b1, w1, w2, g2, b2, layer):
    n = x2d.shape[0]
    tm = POST_ROW_TILE
    grp = pl.BlockSpec((tm, GROUP_WIDTH), lambda i: (i, 0))
    row = pl.BlockSpec((tm, D_MODEL), lambda i: (i, 0))
    resident = lambda a: _layer_block(a, layer, pipeline_mode=pl.Buffered(1))
    return pl.pallas_call(
        _post_kernel,
        grid=(n // tm,),
        in_specs=[grp, grp, grp, grp, row, resident(wo), resident(g1), resident(b1),
                  resident(w1), resident(w2), resident(g2), resident(b2)],
        out_specs=row,
        out_shape=jax.ShapeDtypeStruct((n, D_MODEL), F32),
        compiler_params=_params("parallel"),
        name="post",
    )(oa, ob, oc, od, x2d, wo, g1, b1, w1, w2, g2, b2)


def _expand_heads(wcols):
    return jnp.repeat(wcols, HEAD_DIM, axis=-1)


def _arrange_w_in(w):
    pts = np.cumsum(IN_SIZES)[:-1].tolist()
    (a_qkv, a_a, a_b, a_z, b_cq, b_ckv, b_kr, c_q, c_f, c_i, c_g, d_qkv) = jnp.split(_bf(w), pts, axis=-1)
    z = lambda n: jnp.zeros(w.shape[:-1] + (n,), BF16)
    half = MLA_ROPE // 2
    kr_sw = jnp.concatenate([b_kr[..., half:], b_kr[..., :half]], axis=-1)
    cols = [a_qkv, _expand_heads(a_a), _expand_heads(a_b), a_z,
            b_cq, z(256 - MLA_Q_RANK), b_ckv,
            z(MLA_NOPE), b_kr, z(LANES - MLA_NOPE - MLA_ROPE),
            z(MLA_NOPE), kr_sw, z(LANES - MLA_NOPE - MLA_ROPE), z(C_HGRN - C_MLA_KRS - LANES),
            c_q, c_f, c_i, c_g, d_qkv]
    return jnp.concatenate(cols, axis=-1)


def _arrange_mla_weights(w_uq, w_ukv):
    lead = w_uq.shape[:-2]
    rq = w_uq.shape[-2]
    wq = _bf(w_uq).reshape(lead + (rq, N_HEADS, MLA_NOPE + MLA_ROPE))
    half = MLA_ROPE // 2
    zq = jnp.zeros(lead + (rq, N_HEADS, LANES - MLA_NOPE - MLA_ROPE), BF16)
    q_main = jnp.concatenate([wq, zq], axis=-1)
    rope = wq[..., MLA_NOPE:]
    rope_sw = jnp.concatenate([rope[..., half:], rope[..., :half]], axis=-1)
    q_swap = jnp.concatenate([jnp.zeros(lead + (rq, N_HEADS, MLA_NOPE), BF16), rope_sw, zq], axis=-1)
    pad_rows = lambda m: jnp.concatenate([m, jnp.zeros(lead + (256 - rq, m.shape[-1]), BF16)], axis=-2)
    q_main = pad_rows(q_main.reshape(lead + (rq, N_HEADS * LANES)))
    q_swap = pad_rows(q_swap.reshape(lead + (rq, N_HEADS * LANES)))
    rkv = w_ukv.shape[-2]
    wkv = _bf(w_ukv).reshape(lead + (rkv, N_HEADS, MLA_NOPE + HEAD_DIM))
    k_w = jnp.concatenate([wkv[..., :MLA_NOPE], jnp.zeros(lead + (rkv, N_HEADS, LANES - MLA_NOPE), BF16)], axis=-1)
    v_w = wkv[..., MLA_NOPE:]
    return (q_main, q_swap, k_w.reshape(lead + (rkv, N_HEADS * LANES)),
            v_w.reshape(lead + (rkv, N_HEADS * HEAD_DIM)))


def _rope_tables(seq):
    half = MLA_ROPE // 2
    pos = jnp.arange(seq, dtype=F32)
    inv_freq = ROPE_BASE ** (-jnp.arange(half, dtype=F32) / half)
    ang = pos[:, None] * inv_freq[None, :]
    cos, sin = jnp.cos(ang), jnp.sin(ang)
    ones = jnp.ones((seq, MLA_NOPE), F32)
    zeros = jnp.zeros((seq, MLA_NOPE), F32)
    tail1 = jnp.ones((seq, LANES - MLA_NOPE - MLA_ROPE), F32)
    tail0 = jnp.zeros((seq, LANES - MLA_NOPE - MLA_ROPE), F32)
    cos_t = jnp.concatenate([ones, cos, cos, tail1], axis=1)
    sin_t = jnp.concatenate([zeros, -sin, sin, tail0], axis=1)
    return cos_t, sin_t


def _tile_heads(vec):
    return jnp.tile(vec.astype(F32), N_HEADS)[..., None, :]


def kernel(x, w_in, gdn_conv_w, gdn_a_log, gdn_dt_bias, gdn_norm_g, mla_q_norm_g, mla_kv_norm_g,
           mla_w_uq, mla_w_ukv, hgrn_lb_logits, hgrn_norm_g, w_out, ln1_g, ln1_b, w_ff1, w_ff2,
           ln2_g, ln2_b):
    batch, seq, d_model = x.shape
    assert d_model == D_MODEL and seq % (16 * DSA_BLOCK) == 0 and (batch * seq) % ROW_TILE == 0
    x2d = x.reshape(batch * seq, d_model)
    p_lb = jax.nn.softmax(hgrn_lb_logits.astype(F32), axis=0)
    lower_bounds = (jnp.cumsum(p_lb, axis=0) - p_lb[:1])[:, None, :]
    cos_t, sin_t = _rope_tables(seq)
    w_in_a = _arrange_w_in(w_in)
    avec = jnp.repeat(-jnp.exp(gdn_a_log.astype(F32)), HEAD_DIM, axis=-1)[:, None, :]
    dtb = jnp.repeat(gdn_dt_bias.astype(F32), HEAD_DIM, axis=-1)[:, None, :]
    conv_w = gdn_conv_w.astype(F32)
    gdn_g = _tile_heads(gdn_norm_g)
    hgrn_g = _tile_heads(hgrn_norm_g)
    wq, wqs, wk, wv = _arrange_mla_weights(mla_w_uq, mla_w_ukv)
    qg = jnp.concatenate([mla_q_norm_g.astype(F32), jnp.zeros((DEPTH, 256 - MLA_Q_RANK), F32)], axis=-1)[:, None, :]
    kvg = mla_kv_norm_g.astype(F32)[:, None, :]
    wo, w1, w2 = _bf(w_out), _bf(w_ff1), _bf(w_ff2)
    g1, b1, g2, b2 = (t.astype(F32)[:, None, :] for t in (ln1_g, ln1_b, ln2_g, ln2_b))
    for l in range(DEPTH):
        h = _inproj(x2d, w_in_a, l)
        o_a = _gdn(h, batch, seq, conv_w[l], avec[l], dtb[l], gdn_g[l])
        o_b = _mla(h, batch, seq, cos_t, sin_t, qg[l], kvg[l], wq[l], wqs[l], wk[l], wv[l])
        o_c = _hgrn(h, batch, seq, lower_bounds[l], hgrn_g[l])
        o_d = _dsa(h, batch, seq)
        x2d = _post(o_a, o_b, o_c, o_d, x2d, wo, g1, b1, w1, w2, g2, b2, l)
    return x2d.reshape(batch, seq, d_model)
```

```python
import functools
import math

import numpy as np
import jax
import jax.numpy as jnp
from jax import lax
from jax.experimental import pallas as pl
from jax.experimental.pallas import tpu as pltpu

F32 = jnp.float32
BF16 = jnp.bfloat16

D_MODEL = 1024
DEPTH = 2
GROUP_WIDTH = 256
N_HEADS = 4
HEAD_DIM = 64
GDN_CONV = 4
GDN_CHUNK = 64
MLA_NOPE = 64
MLA_ROPE = 32
MLA_Q_RANK = 192
MLA_KV_RANK = 128
ROPE_BASE = 10000.0
HGRN_CHUNK = 16
DSA_BRANCHES = ((128, 1), (512, 4), (2048, 16))
DSA_BLOCK = 128
DSA_GROUP = 8
ALIBI_SLOPES = tuple(2.0 ** (-8.0 * (j + 1) / N_HEADS) for j in range(N_HEADS))
D_FF = 4 * D_MODEL
DEEPNORM_ALPHA = (2 * DEPTH) ** 0.25
NORM_EPS = 1e-6
MASK_VALUE = -1e30
LOG2E = 1.4426950408889634

LANES = 128
SUBLANES = 8
VMEM_LIMIT_BYTES = 58 * 1024 * 1024

C_GDN_QKV = 0
C_GDN_A = 768
C_GDN_B = 1024
C_GDN_Z = 1280
C_MLA_CQ = 1536
C_MLA_CKV = 1792
C_MLA_KR = 1920
C_MLA_KRS = 2048
C_HGRN = 2304
C_DSA = 3328
C_TOTAL = 4096
IN_SIZES = (768, 4, 4, 256, 192, 128, 32, 256, 256, 256, 256, 768)
ROW_TILE = 512


def _bf(x):
    return x.astype(BF16)


def _dot(a, b):
    return jnp.dot(a, b, preferred_element_type=F32)


def _dot_nt(a, b):
    return lax.dot_general(a, b, (((1,), (1,)), ((), ())), preferred_element_type=F32)


def _dot_tn(a, b):
    return lax.dot_general(a, b, (((0,), (0,)), ((), ())), preferred_element_type=F32)


def _split3(x):
    hi = _bf(x)
    r1 = x - hi.astype(F32)
    mid = _bf(r1)
    lo = _bf(r1 - mid.astype(F32))
    return hi, mid, lo


def _dot_sel_r(x, sel):
    hi, mid, lo = _split3(x)
    return _dot(hi, sel) + _dot(mid, sel) + _dot(lo, sel)


def _dot_sel_l(sel, x):
    hi, mid, lo = _split3(x)
    return _dot(sel, hi) + _dot(sel, mid) + _dot(sel, lo)


def _sigmoid(x):
    return 0.5 + 0.5 * jnp.tanh(0.5 * x)


def _silu(x):
    return x * _sigmoid(x)


def _softplus(x):
    return jnp.maximum(x, 0.0) + jnp.log(1.0 + jnp.exp(-jnp.abs(x)))


def _iota2(shape, axis):
    return lax.broadcasted_iota(jnp.int32, shape, axis)


def _head_block_ones(n):
    r = _iota2((n, n), 0) >> 6
    c = _iota2((n, n), 1) >> 6
    return jnp.where(r == c, 1.0, 0.0).astype(BF16)


def _group_sum(x, ones_bd):
    return _dot(_bf(x), ones_bd)


def _layer_norm_rows(y, g, b):
    mu = jnp.mean(y, axis=-1, keepdims=True)
    d = y - mu
    var = jnp.mean(d * d, axis=-1, keepdims=True)
    return d * lax.rsqrt(var + NORM_EPS) * g + b


def _params(*sem):
    return pltpu.CompilerParams(dimension_semantics=sem, vmem_limit_bytes=VMEM_LIMIT_BYTES)


def _inproj_kernel(x_ref, w_ref, o_ref):
    xb = _bf(x_ref[...])
    for c in range(C_TOTAL // 512):
        o_ref[:, c * 512:(c + 1) * 512] = _dot(xb, w_ref[:, c * 512:(c + 1) * 512])


def _layer_block(stacked, layer, **kw):
    return pl.BlockSpec((None,) + stacked.shape[1:], lambda *_: (layer, 0, 0), **kw)


def _inproj(x2d, w_layers, layer):
    n = x2d.shape[0]
    return pl.pallas_call(
        _inproj_kernel,
        grid=(n // ROW_TILE,),
        in_specs=[pl.BlockSpec((ROW_TILE, D_MODEL), lambda i: (i, 0)), _layer_block(w_layers, layer)],
        out_specs=pl.BlockSpec((ROW_TILE, C_TOTAL), lambda i: (i, 0)),
        out_shape=jax.ShapeDtypeStruct((n, C_TOTAL), F32),
        compiler_params=_params("parallel"),
        name="inproj",
    )(x2d, w_layers)


GDN_GROUP = 8


def _gdn_kernel(qkv_ref, a_ref, b_ref, z_ref, convw_ref, avec_ref, dtb_ref, ng_ref, o_ref,
                s_ref, u_ref, egl_ref, ku_ref, w_ref, qd_ref, qk_ref, kw_ref):
    seq = qkv_ref.shape[0]
    c = GDN_CHUNK
    w = GROUP_WIDTH
    rows_per_step = GDN_GROUP * c
    s_ref[...] = jnp.zeros_like(s_ref)

    same_head = (_iota2((w, w), 0) >> 6) == (_iota2((w, w), 1) >> 6)
    ones_bd = jnp.where(same_head, 1.0, 0.0).astype(BF16)
    rr = _iota2((rows_per_step, rows_per_step), 0)
    rc = _iota2((rows_per_step, rows_per_step), 1)
    same_chunk = (rr >> 6) == (rc >> 6)
    tril_chunks = jnp.where(same_chunk & (rc <= rr), 1.0, 0.0).astype(BF16)
    ones_chunks = jnp.where(same_chunk, 1.0, 0.0).astype(BF16)
    li = _iota2((rows_per_step, w), 0) & (c - 1)
    lj = _iota2((rows_per_step, w), 1) & (c - 1)
    causal = lj <= li
    strict = lj < li
    eye = lj == li
    lane_head = _iota2((c, w), 1) >> 6
    first_rows = _iota2((SUBLANES, 3 * w), 0)

    def stack_heads(t):
        return jnp.concatenate([jnp.where(lane_head == h, t, jnp.zeros_like(t)) for h in range(N_HEADS)], axis=0)

    def block_diag(t):
        return jnp.where(same_head, jnp.concatenate([t] * N_HEADS, axis=0), jnp.zeros((w, w), t.dtype))

    convw = convw_ref[...]
    avec = avec_ref[...]
    dtb = dtb_ref[...]
    ng = ng_ref[...]
    chunks = [slice(i * c, (i + 1) * c) for i in range(GDN_GROUP)]

    def front(n):
        r0 = pl.multiple_of(n * rows_per_step, rows_per_step)
        rows = pl.ds(r0, rows_per_step)
        cur = qkv_ref[rows, :]
        prev = qkv_ref[pl.ds(pl.multiple_of(jnp.maximum(r0 - SUBLANES, 0), SUBLANES), SUBLANES), :]
        prev = jnp.where(n > 0, prev, 0.0)
        y = cur * convw[GDN_CONV - 1:GDN_CONV, :]
        for j in range(1, GDN_CONV):
            shifted = pltpu.roll(cur, j, 0)
            head = jnp.where(first_rows < j, pltpu.roll(prev, j, 0), shifted[0:SUBLANES, :])
            shifted = jnp.concatenate([head, shifted[SUBLANES:, :]], axis=0)
            y = y + shifted * convw[GDN_CONV - 1 - j:GDN_CONV - j, :]
        y = _silu(y)
        q = y[:, 0:w]
        k = y[:, w:2 * w]
        v = y[:, 2 * w:3 * w]
        q = q * lax.rsqrt(_group_sum(q * q, ones_bd) + NORM_EPS) * (HEAD_DIM ** -0.5)
        k = k * lax.rsqrt(_group_sum(k * k, ones_bd) + NORM_EPS)
        beta = _sigmoid(b_ref[rows, :])
        gstep = avec * _softplus(a_ref[rows, :] + dtb)
        g = _dot_sel_l(tril_chunks, gstep)
        g_last = jnp.concatenate([jnp.broadcast_to(g[sl][c - 1:c, :], (c, w)) for sl in chunks], axis=0)
        gr = _dot_sel_l(ones_chunks, jnp.where(eye, g, 0.0))
        decay = jnp.where(causal, jnp.exp(jnp.where(causal, g - gr, 0.0)), 0.0)
        eg = jnp.exp(g)
        kb = k * beta
        kbb = _bf(kb)
        qb = _bf(q)
        prods = [_dot_nt(jnp.concatenate([kbb[sl], qb[sl]], axis=0), _bf(stack_heads(k[sl]))) for sl in chunks]
        lower = jnp.where(strict, jnp.concatenate([p[0:c] for p in prods], axis=0) * decay, 0.0)
        qk = jnp.concatenate([p[c:2 * c] for p in prods], axis=0) * decay
        qd_ref[rows, :] = _bf(q * eg)
        qk_ref[rows, :] = _bf(qk)
        egl_ref[rows, :] = jnp.exp(g_last)
        return lower, v * beta, kb * eg, _bf(k * jnp.exp(g_last - g))

    def back(n, lower, vb, kbg, k_dec):
        rows = pl.ds(pl.multiple_of(n * rows_per_step, rows_per_step), rows_per_step)
        m = -lower
        t = jnp.where(eye, 1.0, 0.0) + m
        for level in range(6):
            mb = _bf(m)
            tb = _bf(t)
            new_m, t_m = [], []
            for sl in chunks:
                m_bd = block_diag(mb[sl])
                if level == 0:
                    new_m.append(_dot(mb[sl], m_bd))
                elif level < 5:
                    both = _dot(jnp.concatenate([mb[sl], tb[sl]], axis=0), m_bd)
                    new_m.append(both[0:c])
                    t_m.append(both[c:2 * c])
                else:
                    t_m.append(_dot(tb[sl], m_bd))
            if level < 5:
                m = jnp.concatenate(new_m, axis=0)
            if level > 0:
                t = t + jnp.concatenate(t_m, axis=0)
        tb = _bf(t)
        uw = [_dot(tb[sl], _bf(jnp.concatenate([stack_heads(vb[sl]), stack_heads(kbg[sl])], axis=1)))
              for sl in chunks]
        u_ref[rows, :] = jnp.concatenate([x[:, 0:w] for x in uw], axis=0)
        w_ref[rows, :] = _bf(jnp.concatenate([x[:, w:2 * w] for x in uw], axis=0))
        for ci, sl in enumerate(chunks):
            kw_ku = _dot_tn(k_dec[sl], _bf(uw[ci]))
            mat_rows = pl.ds(pl.multiple_of((n * GDN_GROUP + ci) * w, w), w)
            ku_ref[mat_rows, :] = kw_ku[:, 0:w]
            kw_ref[mat_rows, :] = _bf(kw_ku[:, w:2 * w])

    def prep(n, carry):
        back(n, *front(n))
        return carry

    lax.fori_loop(0, seq // rows_per_step, prep, 0)

    def scan(n, carry):
        step0 = pl.multiple_of(n * rows_per_step, rows_per_step)
        for ci in range(GDN_GROUP):
            rows = pl.ds(step0 + ci * c, c)
            mat_rows = pl.ds(pl.multiple_of((n * GDN_GROUP + ci) * w, w), w)
            s_bd = s_ref[...]
            lhs = jnp.concatenate([kw_ref[mat_rows, :], w_ref[rows, :], qd_ref[rows, :]], axis=0)
            r = _dot(lhs, _bf(s_bd))
            s_ref[...] = s_bd * egl_ref[pl.ds(step0 + ci * c, 1), :] + jnp.where(
                same_head, ku_ref[mat_rows, :] - r[0:w], 0.0)
            v_new = u_ref[rows, :] - r[w:w + c]
            o_ref[rows, :] = r[w + c:w + 2 * c] + _dot(qk_ref[rows, :], _bf(stack_heads(v_new)))
        return carry

    lax.fori_loop(0, seq // rows_per_step, scan, 0)

    def readout(n, carry):
        rows = pl.ds(pl.multiple_of(n * rows_per_step, rows_per_step), rows_per_step)
        o = o_ref[rows, :]
        ms = _group_sum(o * o, ones_bd) * (1.0 / HEAD_DIM)
        o_ref[rows, :] = o * lax.rsqrt(ms + NORM_EPS) * ng * _silu(z_ref[rows, :])
        return carry

    lax.fori_loop(0, seq // rows_per_step, readout, 0)


def _gdn(h, batch, seq, convw, avec, dtb, ng):
    blk = lambda width, cblk: pl.BlockSpec((seq, width), lambda b: (b, cblk))
    full = lambda a: pl.BlockSpec(a.shape, lambda b: (0, 0))
    return pl.pallas_call(
        _gdn_kernel,
        grid=(batch,),
        in_specs=[blk(768, C_GDN_QKV // 768), blk(256, C_GDN_A // 256), blk(256, C_GDN_B // 256),
                  blk(256, C_GDN_Z // 256), full(convw), full(avec), full(dtb), full(ng)],
        out_specs=pl.BlockSpec((seq, GROUP_WIDTH), lambda b: (b, 0)),
        out_shape=jax.ShapeDtypeStruct((batch * seq, GROUP_WIDTH), F32),
        scratch_shapes=[pltpu.VMEM((GROUP_WIDTH, GROUP_WIDTH), F32),
                        pltpu.VMEM((seq, GROUP_WIDTH), F32), pltpu.VMEM((seq, GROUP_WIDTH), F32),
                        pltpu.VMEM((seq // GDN_CHUNK * GROUP_WIDTH, GROUP_WIDTH), F32),
                        pltpu.VMEM((seq, GROUP_WIDTH), BF16), pltpu.VMEM((seq, GROUP_WIDTH), BF16),
                        pltpu.VMEM((seq, GROUP_WIDTH), BF16),
                        pltpu.VMEM((seq // GDN_CHUNK * GROUP_WIDTH, GROUP_WIDTH), BF16)],
        compiler_params=_params("parallel"),
        name="gdn",
    )(h, h, h, h, convw, avec, dtb, ng)


MLA_Q_BLOCK = 256


def _mla_kernel(cq_ref, ckv_ref, kr_ref, krs_ref, cos_ref, sin_ref, qg_ref, kvg_ref,
                wq_ref, wqs_ref, wk_ref, wv_ref, o_ref, q_ref, k_ref, v_ref):
    seq = cq_ref.shape[0]
    tm = ROW_TILE
    scale = (MLA_NOPE + MLA_ROPE) ** -0.5 * LOG2E

    def prep(i, carry):
        rows = pl.ds(pl.multiple_of(i * tm, tm), tm)
        cq = cq_ref[rows, :]
        nq = cq * lax.rsqrt(jnp.sum(cq * cq, axis=-1, keepdims=True) * (1.0 / MLA_Q_RANK) + NORM_EPS) * qg_ref[...]
        nqb = _bf(nq)
        cos1 = cos_ref[rows, :]
        sin1 = sin_ref[rows, :]
        cos2 = jnp.concatenate([cos1, cos1], axis=1)
        sin2 = jnp.concatenate([sin1, sin1], axis=1)
        q_ref[rows, :] = _bf((_dot(nqb, wq_ref[...]) * cos2 + _dot(nqb, wqs_ref[...]) * sin2) * scale)
        ckv = ckv_ref[rows, :]
        nkv = ckv * lax.rsqrt(jnp.mean(ckv * ckv, axis=-1, keepdims=True) + NORM_EPS) * kvg_ref[...]
        nkvb = _bf(nkv)
        kr = kr_ref[rows, :] * cos1 + krs_ref[rows, :] * sin1
        k_ref[rows, :] = _bf(_dot(nkvb, wk_ref[...]) + jnp.concatenate([kr, kr], axis=1))
        v = _bf(_dot(nkvb, wv_ref[...]))
        v_ref[rows, :] = jnp.concatenate([v, jnp.ones((tm, LANES), BF16)], axis=1)
        return carry

    lax.fori_loop(0, seq // tm, prep, 0)

    tq = MLA_Q_BLOCK
    ri = _iota2((tq, tq), 0)
    ci = _iota2((tq, tq), 1)
    diag_ok = ci <= ri
    lane = _iota2((tq, LANES), 1)
    for qi in range(seq // tq):
        q0 = qi * tq
        v_off = v_ref[0:q0, :] if qi else None
        v_diag = v_ref[q0:q0 + tq, :]
        outs = []
        for hh in range(2):
            qh = q_ref[q0:q0 + tq, hh * LANES:(hh + 1) * LANES]
            s_diag = _dot_nt(qh, k_ref[q0:q0 + tq, hh * LANES:(hh + 1) * LANES])
            s_diag = jnp.where(diag_ok, s_diag, MASK_VALUE)
            m = jnp.max(s_diag, axis=-1, keepdims=True)
            if qi:
                s_off = _dot_nt(qh, k_ref[0:q0, hh * LANES:(hh + 1) * LANES])
                m = jnp.maximum(m, jnp.max(s_off, axis=-1, keepdims=True))
            acc = _dot(_bf(jnp.exp2(s_diag - m)), v_diag)
            if qi:
                acc = acc + _dot(_bf(jnp.exp2(s_off - m)), v_off)
            outs.append(acc[:, 0:LANES] / acc[:, LANES:2 * LANES])
        o_ref[q0:q0 + tq, :] = jnp.where(lane < HEAD_DIM, outs[0], outs[1])


def _mla(h, batch, seq, cos_t, sin_t, qg, kvg, wq, wqs, wk, wv):
    col = lambda width, off: pl.BlockSpec((seq, width), lambda b, p: (b, off // width))
    table = pl.BlockSpec((seq, LANES), lambda b, p: (0, 0))
    full = lambda a: pl.BlockSpec(a.shape, lambda b, p: (0, 0))
    pair_cols = lambda a, width: pl.BlockSpec((a.shape[0], width), lambda b, p: (0, p))
    return pl.pallas_call(
        _mla_kernel,
        grid=(batch, 2),
        in_specs=[col(256, C_MLA_CQ), col(LANES, C_MLA_CKV), col(LANES, C_MLA_KR), col(LANES, C_MLA_KRS),
                  table, table, full(qg), full(kvg),
                  pair_cols(wq, 256), pair_cols(wqs, 256), pair_cols(wk, 256), pair_cols(wv, LANES)],
        out_specs=pl.BlockSpec((seq, LANES), lambda b, p: (b, p)),
        out_shape=jax.ShapeDtypeStruct((batch * seq, GROUP_WIDTH), F32),
        scratch_shapes=[pltpu.VMEM((seq, 256), BF16)] * 3,
        compiler_params=_params("parallel", "parallel"),
        name="mla",
    )(h, h, h, h, cos_t, sin_t, qg, kvg, wq, wqs, wk, wv)


HGRN_SCAN_UNROLL = 8
HGRN_SUBTILE = 32


def _hgrn_kernel(q0_ref, q1_ref, f0_ref, f1_ref, i0_ref, i1_ref, g0_ref, g1_ref, lb_ref, ng_ref, o_ref,
                 q_s, kk_s, cum_s, v_s, oi_s, qd0, qd1, kd0, kd1, oc0, oc1, dec_s, incr_s):
    seq = q0_ref.shape[0]
    cc = HGRN_CHUNK
    nch = seq // cc
    w = GROUP_WIDTH
    same_head = (_iota2((w, w), 0) >> 6) == (_iota2((w, w), 1) >> 6)
    ones_bd = jnp.where(same_head, 1.0, 0.0).astype(BF16)
    lb = lb_ref[...]
    ng = ng_ref[...]
    one_m_lb = 1.0 - lb

    def rows_of(p):
        return pl.ds(p, nch, stride=cc)

    def both(ref0, ref1, rows):
        return jnp.concatenate([ref0[rows, :], ref1[rows, :]], axis=1)

    def put(ref0, ref1, rows, val):
        ref0[rows, :] = val[:, 0:LANES]
        ref1[rows, :] = val[:, LANES:2 * LANES]

    quarter = seq // 4

    def split(dst, src):
        for r in range(4):
            dst[r * quarter:(r + 1) * quarter, :] = src[pl.ds(r, quarter, stride=4), :]

    def rows_of4(p):
        return pl.ds((p % 4) * quarter + p // 4, nch, stride=4)

    split(oc0, f0_ref)
    split(oc1, f1_ref)
    split(qd0, q0_ref)
    split(qd1, q1_ref)
    split(kd0, i0_ref)
    split(kd1, i1_ref)

    cum = None
    for p in range(cc):
        fl = both(oc0, oc1, rows_of4(p))
        z = jnp.exp(-jnp.abs(fl))
        r = 1.0 / (1.0 + z)
        zr = z * r
        pos = fl >= 0.0
        log_f = jnp.log(lb + one_m_lb * jnp.where(pos, r, zr))
        cum = log_f if p == 0 else cum + log_f
        q_s[p] = both(qd0, qd1, rows_of4(p))
        kk_s[p] = one_m_lb * jnp.where(pos, zr, r)
        cum_s[p] = cum
        v_s[p] = both(kd0, kd1, rows_of4(p))
    dec_s[...] = jnp.exp(cum)
    for p in range(cc):
        c_p = cum_s[p]
        put(qd0, qd1, rows_of(p), q_s[p] * jnp.exp(c_p))
        put(kd0, kd1, rows_of(p), kk_s[p] * jnp.exp(cum_s[cc - 1] - c_p))

    sub = HGRN_SUBTILE

    def intra(t, carry):
        rs = pl.ds(pl.multiple_of(t * sub, sub), sub)
        for p in range(cc):
            q_p = q_s[p, rs, :]
            c_p = cum_s[p, rs, :]
            xs = [_bf(q_p * kk_s[s, rs, :] * jnp.exp(c_p - cum_s[s, rs, :])) for s in range(p)]
            xs.append(_bf(q_p * kk_s[p, rs, :]))
            sums = _dot(jnp.concatenate(xs, axis=0), ones_bd)
            acc = sums[p * sub:(p + 1) * sub] * v_s[p, rs, :]
            for s in range(p):
                acc = acc + sums[s * sub:(s + 1) * sub] * v_s[s, rs, :]
            oi_s[p, rs, :] = acc
        return carry

    lax.fori_loop(0, nch // sub, intra, 0)

    gs = HGRN_SCAN_UNROLL
    blk_rows = gs * cc
    hw = LANES
    chunk_of_row = _iota2((blk_rows, hw), 0) >> 4
    chunk_masks = [jnp.where(chunk_of_row == j, 1.0, 0.0).astype(BF16) for j in range(gs)]
    same_head_pair = (_iota2((hw, hw), 0) >> 6) == (_iota2((hw, hw), 1) >> 6)
    pairs = ((i0_ref, kd0, qd0, oc0), (i1_ref, kd1, qd1, oc1))

    def expand(t):
        return jnp.concatenate([t * chunk_masks[j] for j in range(gs)], axis=1)

    def increments(g):
        rows = pl.ds(pl.multiple_of(g * blk_rows, blk_rows), blk_rows)
        return jnp.concatenate([_dot_tn(_bf(v_ref[rows, :]), expand(_bf(kd_ref[rows, :])))
                                for v_ref, kd_ref, _, _ in pairs], axis=1)

    nsteps = nch // gs
    incr_s[...] = increments(0)

    def scan(g, sts):
        incr_next = increments(jnp.minimum(g + 1, nsteps - 1))
        rows = pl.ds(pl.multiple_of(g * blk_rows, blk_rows), blk_rows)
        new_sts = []
        for pi, (_, _, qd_ref, oc_ref) in enumerate(pairs):
            st = sts[pi]
            states = []
            for j in range(gs):
                states.append(_bf(st))
                col = (pi * gs + j) * hw
                dec = dec_s[pl.ds(g * gs + j, 1), :][:, pi * hw:(pi + 1) * hw]
                st = st * dec + jnp.where(same_head_pair, incr_s[:, col:col + hw], 0.0)
            oc_ref[rows, :] = _dot_nt(expand(_bf(qd_ref[rows, :])), jnp.concatenate(states, axis=1))
            new_sts.append(st)
        incr_s[...] = incr_next
        return tuple(new_sts)

    lax.fori_loop(0, nsteps, scan, (jnp.zeros((hw, hw), F32), jnp.zeros((hw, hw), F32)))

    split(qd0, oc0)
    split(qd1, oc1)
    split(kd0, g0_ref)
    split(kd1, g1_ref)
    for p in range(cc):
        o = oi_s[p] + both(qd0, qd1, rows_of4(p))
        ms = _group_sum(o * o, ones_bd) * (1.0 / HEAD_DIM)
        put(oc0, oc1, rows_of(p), o * lax.rsqrt(ms + NORM_EPS) * ng * _silu(both(kd0, kd1, rows_of4(p))))
    o_ref[:, 0:LANES] = oc0[...]
    o_ref[:, LANES:2 * LANES] = oc1[...]


def _hgrn(h, batch, seq, lb, ng):
    base = C_HGRN // LANES
    half = lambda j: pl.BlockSpec((seq, LANES), lambda b: (b, base + j))
    full = lambda a: pl.BlockSpec(a.shape, lambda b: (0, 0))
    nch = seq // HGRN_CHUNK
    tiles = pltpu.VMEM((HGRN_CHUNK, nch, GROUP_WIDTH), F32)
    nat = pltpu.VMEM((seq, LANES), F32)
    return pl.pallas_call(
        _hgrn_kernel,
        grid=(batch,),
        in_specs=[half(j) for j in range(8)] + [full(lb), full(ng)],
        out_specs=pl.BlockSpec((seq, GROUP_WIDTH), lambda b: (b, 0)),
        out_shape=jax.ShapeDtypeStruct((batch * seq, GROUP_WIDTH), F32),
        scratch_shapes=[tiles] * 5 + [nat] * 6 + [pltpu.VMEM((nch, GROUP_WIDTH), F32),
                                                  pltpu.VMEM((LANES, 2 * HGRN_SCAN_UNROLL * LANES), F32)],
        compiler_params=_params("parallel"),
        name="hgrn",
    )(h, h, h, h, h, h, h, h, lb, ng)


def _dsa_kernel(q_ref, k_ref, v_ref, o_ref, num_ref, m_ref, l_ref):
    seq = q_ref.shape[0]
    blk = DSA_BLOCK
    pair = pl.program_id(1)
    qi = _iota2((blk, 2 * blk), 0)
    ki = _iota2((blk, 2 * blk), 1)
    steps = blk + qi - ki
    lane = _iota2((blk, LANES), 1)
    first = lane < HEAD_DIM
    qscale = HEAD_DIM ** -0.5 * LOG2E
    slopes = [jnp.where(pair == 0, ALIBI_SLOPES[hh], ALIBI_SLOPES[2 + hh]) * LOG2E for hh in range(2)]

    for bi, (window, dil) in enumerate(DSA_BRANCHES):
        nblk = seq // (dil * blk)
        in_window = (steps >= 0) & (steps <= window // dil)
        dist = (steps * dil).astype(F32)
        two_blocks = nblk > 1
        if two_blocks:
            bias = [jnp.where(in_window, -slopes[hh] * dist, MASK_VALUE) for hh in range(2)]
            bias_first = [jnp.where(ki >= blk, bias[hh], MASK_VALUE) for hh in range(2)]
        else:
            bias_first = [jnp.where(in_window, -slopes[hh] * dist, MASK_VALUE)[:, blk:] for hh in range(2)]
            bias = bias_first

        def attend_group(blocks, bi=bi, dil=dil, bias=bias, bias_first=bias_first, two_blocks=two_blocks):
            if dil == 1:
                ld = lambda ref, s0: ref[pl.ds(s0, blk), :]
                dst = lambda s0: pl.ds(bi * seq + s0, blk)
            else:
                ld = lambda ref, s0: ref[pl.ds(s0, blk, stride=dil), :]
                dst = lambda s0: pl.ds(bi * seq + s0, blk, stride=dil)
            tiles, vbs = [], []
            for start, is_first in blocks:
                qb = ld(q_ref, start) * qscale
                if two_blocks:
                    prev = start if is_first else start - dil * blk
                    kb = _bf(jnp.concatenate([ld(k_ref, prev), ld(k_ref, start)], axis=0))
                    vb = _bf(jnp.concatenate([ld(v_ref, prev), ld(v_ref, start)], axis=0))
                else:
                    kb = _bf(ld(k_ref, start))
                    vb = _bf(ld(v_ref, start))
                vbs.append(jnp.concatenate([vb, jnp.ones(vb.shape, BF16)], axis=1))
                for hh in range(2):
                    qm = _bf(jnp.where(first if hh == 0 else jnp.logical_not(first), qb, 0.0))
                    tiles.append(_dot_nt(qm, kb) + (bias_first if is_first else bias)[hh])
            s = jnp.concatenate(tiles, axis=0)
            m = jnp.max(s, axis=-1, keepdims=True)
            pb = _bf(jnp.exp2(s - m))
            for i, (start, _) in enumerate(blocks):
                r0, r1, r2 = 2 * i * blk, (2 * i + 1) * blk, (2 * i + 2) * blk
                pv = jnp.where(jnp.concatenate([first, first], axis=1), _dot(pb[r0:r1], vbs[i]), _dot(pb[r1:r2], vbs[i]))
                num_ref[dst(start), :] = pv[:, 0:LANES]
                l_ref[dst(start), :] = pv[:, LANES:2 * LANES]
                m_ref[dst(start), :] = jnp.where(first, m[r0:r1], m[r1:r2])

        group = DSA_GROUP
        if nblk == 1:
            def body(g, carry, attend_group=attend_group):
                attend_group([(g * group + j, True) for j in range(group)])
                return carry
            lax.fori_loop(0, dil // group, body, 0)
        elif nblk < group:
            per = group // nblk

            def body(g, carry, attend_group=attend_group, dil=dil, nblk=nblk, per=per):
                attend_group([(g * per + j + dil * blk * n, n == 0) for j in range(per) for n in range(nblk)])
                return carry
            lax.fori_loop(0, dil // per, body, 0)
        else:
            assert dil == 1 and nblk % group == 0
            attend_group([(n * blk, n == 0) for n in range(group)])

            def body(g, carry, attend_group=attend_group):
                attend_group([(pl.multiple_of((g * group + j) * blk, blk), False) for j in range(group)])
                return carry
            lax.fori_loop(1, nblk // group, body, 0)

    def merge(n, carry):
        r0 = pl.multiple_of(n * blk, blk)
        rows = [pl.ds(bi * seq + r0, blk) for bi in range(len(DSA_BRANCHES))]
        m0, m1, m2 = [m_ref[r, :] for r in rows]
        mm = jnp.maximum(jnp.maximum(m0, m1), m2)
        w0, w1, w2 = jnp.exp2(m0 - mm), jnp.exp2(m1 - mm), jnp.exp2(m2 - mm)
        num = w0 * num_ref[rows[0], :] + w1 * num_ref[rows[1], :] + w2 * num_ref[rows[2], :]
        den = w0 * l_ref[rows[0], :] + w1 * l_ref[rows[1], :] + w2 * l_ref[rows[2], :]
        o_ref[pl.ds(r0, blk), :] = num / den
        return carry

    lax.fori_loop(0, seq // blk, merge, 0)


def _dsa(h, batch, seq):
    base = C_DSA // LANES
    blk = lambda j: pl.BlockSpec((seq, LANES), lambda b, p: (b, base + 2 * j + p))
    nb = len(DSA_BRANCHES)
    return pl.pallas_call(
        _dsa_kernel,
        grid=(batch, 2),
        in_specs=[blk(0), blk(1), blk(2)],
        out_specs=pl.BlockSpec((seq, LANES), lambda b, p: (b, p)),
        out_shape=jax.ShapeDtypeStruct((batch * seq, GROUP_WIDTH), F32),
        scratch_shapes=[pltpu.VMEM((nb * seq, LANES), F32), pltpu.VMEM((nb * seq, LANES), F32),
                        pltpu.VMEM((nb * seq, LANES), F32)],
        compiler_params=_params("parallel", "parallel"),
        name="dsa",
    )(h, h, h)


FF_CHUNK = 1024
POST_ROW_TILE = 512


def _post_kernel(oa_ref, ob_ref, oc_ref, od_ref, x_ref, wo_ref, g1_ref, b1_ref, w1_ref, w2_ref, g2_ref, b2_ref,
                 y_ref):
    gw = GROUP_WIDTH
    mixed = _dot(_bf(oa_ref[...]), wo_ref[0:gw, :])
    mixed = mixed + _dot(_bf(ob_ref[...]), wo_ref[gw:2 * gw, :])
    mixed = mixed + _dot(_bf(oc_ref[...]), wo_ref[2 * gw:3 * gw, :])
    mixed = mixed + _dot(_bf(od_ref[...]), wo_ref[3 * gw:4 * gw, :])
    x = _layer_norm_rows(DEEPNORM_ALPHA * x_ref[...] + mixed, g1_ref[...], b1_ref[...])
    xb = _bf(x)
    acc = jnp.zeros(x.shape, F32)
    for c in range(D_FF // FF_CHUNK):
        hmid = _dot(xb, w1_ref[:, c * FF_CHUNK:(c + 1) * FF_CHUNK])
        hmid = jnp.square(jnp.maximum(hmid, 0.0))
        acc = acc + _dot(_bf(hmid), w2_ref[c * FF_CHUNK:(c + 1) * FF_CHUNK, :])
    y_ref[...] = _layer_norm_rows(DEEPNORM_ALPHA * x + acc, g2_ref[...], b2_ref[...])


def _post(oa, ob, oc, od, x2d, wo, g1, b1, w1, w2, g2, b2, layer):
    n = x2d.shape[0]
    tm = POST_ROW_TILE
    grp = pl.BlockSpec((tm, GROUP_WIDTH), lambda i: (i, 0))
    row = pl.BlockSpec((tm, D_MODEL), lambda i: (i, 0))
    resident = lambda a: _layer_block(a, layer, pipeline_mode=pl.Buffered(1))
    return pl.pallas_call(
        _post_kernel,
        grid=(n // tm,),
        in_specs=[grp, grp, grp, grp, row, resident(wo), resident(g1), resident(b1),
                  resident(w1), resident(w2), resident(g2), resident(b2)],
        out_specs=row,
        out_shape=jax.ShapeDtypeStruct((n, D_MODEL), F32),
        compiler_params=_params("parallel"),
        name="post",
    )(oa, ob, oc, od, x2d, wo, g1, b1, w1, w2, g2, b2)


def _expand_heads(wcols):
    return jnp.repeat(wcols, HEAD_DIM, axis=-1)


def _arrange_w_in(w):
    pts = np.cumsum(IN_SIZES)[:-1].tolist()
    (a_qkv, a_a, a_b, a_z, b_cq, b_ckv, b_kr, c_q, c_f, c_i, c_g, d_qkv) = jnp.split(_bf(w), pts, axis=-1)
    z = lambda n: jnp.zeros(w.shape[:-1] + (n,), BF16)
    half = MLA_ROPE // 2
    kr_sw = jnp.concatenate([b_kr[..., half:], b_kr[..., :half]], axis=-1)
    cols = [a_qkv, _expand_heads(a_a), _expand_heads(a_b), a_z,
            b_cq, z(256 - MLA_Q_RANK), b_ckv,
            z(MLA_NOPE), b_kr, z(LANES - MLA_NOPE - MLA_ROPE),
            z(MLA_NOPE), kr_sw, z(LANES - MLA_NOPE - MLA_ROPE), z(C_HGRN - C_MLA_KRS - LANES),
            c_q, c_f, c_i, c_g, d_qkv]
    return jnp.concatenate(cols, axis=-1)


def _arrange_mla_weights(w_uq, w_ukv):
    lead = w_uq.shape[:-2]
    rq = w_uq.shape[-2]
    wq = _bf(w_uq).reshape(lead + (rq, N_HEADS, MLA_NOPE + MLA_ROPE))
    half = MLA_ROPE // 2
    zq = jnp.zeros(lead + (rq, N_HEADS, LANES - MLA_NOPE - MLA_ROPE), BF16)
    q_main = jnp.concatenate([wq, zq], axis=-1)
    rope = wq[..., MLA_NOPE:]
    rope_sw = jnp.concatenate([rope[..., half:], rope[..., :half]], axis=-1)
    q_swap = jnp.concatenate([jnp.zeros(lead + (rq, N_HEADS, MLA_NOPE), BF16), rope_sw, zq], axis=-1)
    pad_rows = lambda m: jnp.concatenate([m, jnp.zeros(lead + (256 - rq, m.shape[-1]), BF16)], axis=-2)
    q_main = pad_rows(q_main.reshape(lead + (rq, N_HEADS * LANES)))
    q_swap = pad_rows(q_swap.reshape(lead + (rq, N_HEADS * LANES)))
    rkv = w_ukv.shape[-2]
    wkv = _bf(w_ukv).reshape(lead + (rkv, N_HEADS, MLA_NOPE + HEAD_DIM))
    k_w = jnp.concatenate([wkv[..., :MLA_NOPE], jnp.zeros(lead + (rkv, N_HEADS, LANES - MLA_NOPE), BF16)], axis=-1)
    v_w = wkv[..., MLA_NOPE:]
    return (q_main, q_swap, k_w.reshape(lead + (rkv, N_HEADS * LANES)),
            v_w.reshape(lead + (rkv, N_HEADS * HEAD_DIM)))


def _rope_tables(seq):
    half = MLA_ROPE // 2
    pos = jnp.arange(seq, dtype=F32)
    inv_freq = ROPE_BASE ** (-jnp.arange(half, dtype=F32) / half)
    ang = pos[:, None] * inv_freq[None, :]
    cos, sin = jnp.cos(ang), jnp.sin(ang)
    ones = jnp.ones((seq, MLA_NOPE), F32)
    zeros = jnp.zeros((seq, MLA_NOPE), F32)
    tail1 = jnp.ones((seq, LANES - MLA_NOPE - MLA_ROPE), F32)
    tail0 = jnp.zeros((seq, LANES - MLA_NOPE - MLA_ROPE), F32)
    cos_t = jnp.concatenate([ones, cos, cos, tail1], axis=1)
    sin_t = jnp.concatenate([zeros, -sin, sin, tail0], axis=1)
    return cos_t, sin_t


def _tile_heads(vec):
    return jnp.tile(vec.astype(F32), N_HEADS)[..., None, :]


def kernel(x, w_in, gdn_conv_w, gdn_a_log, gdn_dt_bias, gdn_norm_g, mla_q_norm_g, mla_kv_norm_g,
           mla_w_uq, mla_w_ukv, hgrn_lb_logits, hgrn_norm_g, w_out, ln1_g, ln1_b, w_ff1, w_ff2,
           ln2_g, ln2_b):
    batch, seq, d_model = x.shape
    assert d_model == D_MODEL and seq % (16 * DSA_BLOCK) == 0 and (batch * seq) % ROW_TILE == 0
    x2d = x.reshape(batch * seq, d_model)
    p_lb = jax.nn.softmax(hgrn_lb_logits.astype(F32), axis=0)
    lower_bounds = (jnp.cumsum(p_lb, axis=0) - p_lb[:1])[:, None, :]
    cos_t, sin_t = _rope_tables(seq)
    w_in_a = _arrange_w_in(w_in)
    avec = jnp.repeat(-jnp.exp(gdn_a_log.astype(F32)), HEAD_DIM, axis=-1)[:, None, :]
    dtb = jnp.repeat(gdn_dt_bias.astype(F32), HEAD_DIM, axis=-1)[:, None, :]
    conv_w = gdn_conv_w.astype(F32)
    gdn_g = _tile_heads(gdn_norm_g)
    hgrn_g = _tile_heads(hgrn_norm_g)
    wq, wqs, wk, wv = _arrange_mla_weights(mla_w_uq, mla_w_ukv)
    qg = jnp.concatenate([mla_q_norm_g.astype(F32), jnp.zeros((DEPTH, 256 - MLA_Q_RANK), F32)], axis=-1)[:, None, :]
    kvg = mla_kv_norm_g.astype(F32)[:, None, :]
    wo, w1, w2 = _bf(w_out), _bf(w_ff1), _bf(w_ff2)
    g1, b1, g2, b2 = (t.astype(F32)[:, None, :] for t in (ln1_g, ln1_b, ln2_g, ln2_b))
    for l in range(DEPTH):
        h = _inproj(x2d, w_in_a, l)
        o_a = _gdn(h, batch, seq, conv_w[l], avec[l], dtb[l], gdn_g[l])
        o_b = _mla(h, batch, seq, cos_t, sin_t, qg[l], kvg[l], wq[l], wqs[l], wk[l], wv[l])
        o_c = _hgrn(h, batch, seq, lower_bounds[l], hgrn_g[l])
        o_d = _dsa(h, batch, seq)
        x2d = _post(o_a, o_b, o_c, o_d, x2d, wo, g1, b1, w1, w2, g2, b2, l)
    return x2d.reshape(batch, seq, d_model)
```

```python
import functools
import math

import numpy as np
import jax
import jax.numpy as jnp
from jax import lax
from jax.experimental import pallas as pl
from jax.experimental.pallas import tpu as pltpu

F32 = jnp.float32
BF16 = jnp.bfloat16

D_MODEL = 1024
DEPTH = 2
GROUP_WIDTH = 256
N_HEADS = 4
HEAD_DIM = 64
GDN_CONV = 4
GDN_CHUNK = 64
MLA_NOPE = 64
MLA_ROPE = 32
MLA_Q_RANK = 192
MLA_KV_RANK = 128
ROPE_BASE = 10000.0
HGRN_CHUNK = 16
DSA_BRANCHES = ((128, 1), (512, 4), (2048, 16))
DSA_BLOCK = 128
DSA_GROUP = 8
ALIBI_SLOPES = tuple(2.0 ** (-8.0 * (j + 1) / N_HEADS) for j in range(N_HEADS))
D_FF = 4 * D_MODEL
DEEPNORM_ALPHA = (2 * DEPTH) ** 0.25
NORM_EPS = 1e-6
MASK_VALUE = -1e30
LOG2E = 1.4426950408889634

LANES = 128
SUBLANES = 8
VMEM_LIMIT_BYTES = 58 * 1024 * 1024

C_GDN_QKV = 0
C_GDN_A = 768
C_GDN_B = 1024
C_GDN_Z = 1280
C_MLA_CQ = 1536
C_MLA_CKV = 1792
C_MLA_KR = 1920
C_MLA_KRS = 2048
C_HGRN = 2304
C_DSA = 3328
C_TOTAL = 4096
IN_SIZES = (768, 4, 4, 256, 192, 128, 32, 256, 256, 256, 256, 768)
ROW_TILE = 512


def _bf(x):
    return x.astype(BF16)


def _dot(a, b):
    return jnp.dot(a, b, preferred_element_type=F32)


def _dot_nt(a, b):
    return lax.dot_general(a, b, (((1,), (1,)), ((), ())), preferred_element_type=F32)


def _dot_tn(a, b):
    return lax.dot_general(a, b, (((0,), (0,)), ((), ())), preferred_element_type=F32)


def _split3(x):
    hi = _bf(x)
    r1 = x - hi.astype(F32)
    mid = _bf(r1)
    lo = _bf(r1 - mid.astype(F32))
    return hi, mid, lo


def _dot_sel_r(x, sel):
    hi, mid, lo = _split3(x)
    return _dot(hi, sel) + _dot(mid, sel) + _dot(lo, sel)


def _dot_sel_l(sel, x):
    hi, mid, lo = _split3(x)
    return _dot(sel, hi) + _dot(sel, mid) + _dot(sel, lo)


def _sigmoid(x):
    return 0.5 + 0.5 * jnp.tanh(0.5 * x)


def _silu(x):
    return x * _sigmoid(x)


def _softplus(x):
    return jnp.maximum(x, 0.0) - jnp.log(_sigmoid(jnp.abs(x)))


def _iota2(shape, axis):
    return lax.broadcasted_iota(jnp.int32, shape, axis)


def _head_block_ones(n):
    r = _iota2((n, n), 0) >> 6
    c = _iota2((n, n), 1) >> 6
    return jnp.where(r == c, 1.0, 0.0).astype(BF16)


def _group_sum(x, ones_bd):
    return _dot(_bf(x), ones_bd)


def _layer_norm_rows(y, g, b):
    mu = jnp.mean(y, axis=-1, keepdims=True)
    d = y - mu
    var = jnp.mean(d * d, axis=-1, keepdims=True)
    return d * lax.rsqrt(var + NORM_EPS) * g + b


def _params(*sem):
    return pltpu.CompilerParams(dimension_semantics=sem, vmem_limit_bytes=VMEM_LIMIT_BYTES)


def _inproj_kernel(x_ref, w_ref, o_ref):
    xb = _bf(x_ref[...])
    for c in range(C_TOTAL // 512):
        o_ref[:, c * 512:(c + 1) * 512] = _dot_nt(xb, w_ref[c * 512:(c + 1) * 512, :])


def _layer_block(stacked, layer, **kw):
    return pl.BlockSpec((None,) + stacked.shape[1:], lambda *_: (layer, 0, 0), **kw)


def _inproj(x2d, w_layers, layer):
    n = x2d.shape[0]
    return pl.pallas_call(
        _inproj_kernel,
        grid=(n // ROW_TILE,),
        in_specs=[pl.BlockSpec((ROW_TILE, D_MODEL), lambda i: (i, 0)), _layer_block(w_layers, layer)],
        out_specs=pl.BlockSpec((ROW_TILE, C_TOTAL), lambda i: (i, 0)),
        out_shape=jax.ShapeDtypeStruct((n, C_TOTAL), F32),
        compiler_params=_params("parallel"),
        name="inproj",
    )(x2d, w_layers)


GDN_GROUP = 8


def _gdn_kernel(qkv_ref, a_ref, b_ref, z_ref, convw_ref, avec_ref, dtb_ref, ng_ref, o_ref,
                s_ref, u_ref, egl_ref, ku_ref, w_ref, qd_ref, qk_ref, kw_ref):
    seq = qkv_ref.shape[0]
    c = GDN_CHUNK
    w = GROUP_WIDTH
    rows_per_step = GDN_GROUP * c
    s_ref[...] = jnp.zeros_like(s_ref)

    same_head = (_iota2((w, w), 0) >> 6) == (_iota2((w, w), 1) >> 6)
    ones_bd = jnp.where(same_head, 1.0, 0.0).astype(BF16)
    rr = _iota2((rows_per_step, rows_per_step), 0)
    rc = _iota2((rows_per_step, rows_per_step), 1)
    same_chunk = (rr >> 6) == (rc >> 6)
    tril_chunks = jnp.where(same_chunk & (rc <= rr), 1.0, 0.0).astype(BF16)
    ones_chunks = jnp.where(same_chunk, 1.0, 0.0).astype(BF16)
    li = _iota2((rows_per_step, w), 0) & (c - 1)
    lj = _iota2((rows_per_step, w), 1) & (c - 1)
    causal = lj <= li
    strict = lj < li
    eye = lj == li
    lane_head = _iota2((c, w), 1) >> 6
    first_rows = _iota2((SUBLANES, 3 * w), 0)

    def stack_heads(t):
        return jnp.concatenate([jnp.where(lane_head == h, t, jnp.zeros_like(t)) for h in range(N_HEADS)], axis=0)

    def block_diag(t):
        return jnp.where(same_head, jnp.concatenate([t] * N_HEADS, axis=0), jnp.zeros((w, w), t.dtype))

    convw = convw_ref[...]
    avec = avec_ref[...]
    dtb = dtb_ref[...]
    ng = ng_ref[...]
    chunks = [slice(i * c, (i + 1) * c) for i in range(GDN_GROUP)]

    def front(n):
        r0 = pl.multiple_of(n * rows_per_step, rows_per_step)
        rows = pl.ds(r0, rows_per_step)
        cur = qkv_ref[rows, :]
        prev = qkv_ref[pl.ds(pl.multiple_of(jnp.maximum(r0 - SUBLANES, 0), SUBLANES), SUBLANES), :]
        prev = jnp.where(n > 0, prev, 0.0)
        y = cur * convw[GDN_CONV - 1:GDN_CONV, :]
        for j in range(1, GDN_CONV):
            shifted = pltpu.roll(cur, j, 0)
            head = jnp.where(first_rows < j, pltpu.roll(prev, j, 0), shifted[0:SUBLANES, :])
            shifted = jnp.concatenate([head, shifted[SUBLANES:, :]], axis=0)
            y = y + shifted * convw[GDN_CONV - 1 - j:GDN_CONV - j, :]
        y = _silu(y)
        q = y[:, 0:w]
        k = y[:, w:2 * w]
        v = y[:, 2 * w:3 * w]
        q = q * lax.rsqrt(_group_sum(q * q, ones_bd) + NORM_EPS) * (HEAD_DIM ** -0.5)
        k = k * lax.rsqrt(_group_sum(k * k, ones_bd) + NORM_EPS)
        beta = _sigmoid(b_ref[rows, :])
        gstep = avec * _softplus(a_ref[rows, :] + dtb)
        g = _dot_sel_l(tril_chunks, gstep)
        g_last = jnp.concatenate([jnp.broadcast_to(g[sl][c - 1:c, :], (c, w)) for sl in chunks], axis=0)
        gr = _dot_sel_l(ones_chunks, jnp.where(eye, g, 0.0))
        decay = jnp.where(causal, jnp.exp(jnp.where(causal, g - gr, 0.0)), 0.0)
        eg = jnp.exp(g)
        kb = k * beta
        kbb = _bf(kb)
        qb = _bf(q)
        prods = [_dot_nt(jnp.concatenate([kbb[sl], qb[sl]], axis=0), _bf(stack_heads(k[sl]))) for sl in chunks]
        lower = jnp.where(strict, jnp.concatenate([p[0:c] for p in prods], axis=0) * decay, 0.0)
        qk = jnp.concatenate([p[c:2 * c] for p in prods], axis=0) * decay
        qd_ref[rows, :] = _bf(q * eg)
        qk_ref[rows, :] = _bf(qk)
        egl_ref[rows, :] = jnp.exp(g_last)
        return lower, v * beta, kb * eg, _bf(k * jnp.exp(g_last - g))

    def back(n, lower, vb, kbg, k_dec):
        rows = pl.ds(pl.multiple_of(n * rows_per_step, rows_per_step), rows_per_step)
        m = -lower
        t = jnp.where(eye, 1.0, 0.0) + m
        for level in range(6):
            mb = _bf(m)
            tb = _bf(t)
            new_m, t_m = [], []
            for sl in chunks:
                m_bd = block_diag(mb[sl])
                if level == 0:
                    new_m.append(_dot(mb[sl], m_bd))
                elif level < 5:
                    both = _dot(jnp.concatenate([mb[sl], tb[sl]], axis=0), m_bd)
                    new_m.append(both[0:c])
                    t_m.append(both[c:2 * c])
                else:
                    t_m.append(_dot(tb[sl], m_bd))
            if level < 5:
                m = jnp.concatenate(new_m, axis=0)
            if level > 0:
                t = t + jnp.concatenate(t_m, axis=0)
        tb = _bf(t)
        uw = [_dot(tb[sl], _bf(jnp.concatenate([stack_heads(vb[sl]), stack_heads(kbg[sl])], axis=1)))
              for sl in chunks]
        u_ref[rows, :] = jnp.concatenate([x[:, 0:w] for x in uw], axis=0)
        w_ref[rows, :] = _bf(jnp.concatenate([x[:, w:2 * w] for x in uw], axis=0))
        for ci, sl in enumerate(chunks):
            kw_ku = _dot_tn(k_dec[sl], _bf(uw[ci]))
            mat_rows = pl.ds(pl.multiple_of((n * GDN_GROUP + ci) * w, w), w)
            ku_ref[mat_rows, :] = kw_ku[:, 0:w]
            kw_ref[mat_rows, :] = _bf(kw_ku[:, w:2 * w])

    def prep(n, carry):
        back(n, *front(n))
        return carry

    lax.fori_loop(0, seq // rows_per_step, prep, 0)

    def scan(n, carry):
        step0 = pl.multiple_of(n * rows_per_step, rows_per_step)
        for ci in range(GDN_GROUP):
            rows = pl.ds(step0 + ci * c, c)
            mat_rows = pl.ds(pl.multiple_of((n * GDN_GROUP + ci) * w, w), w)
            s_bd = s_ref[...]
            lhs = jnp.concatenate([kw_ref[mat_rows, :], w_ref[rows, :], qd_ref[rows, :]], axis=0)
            r = _dot(lhs, _bf(s_bd))
            s_ref[...] = s_bd * egl_ref[pl.ds(step0 + ci * c, 1), :] + jnp.where(
                same_head, ku_ref[mat_rows, :] - r[0:w], 0.0)
            v_new = u_ref[rows, :] - r[w:w + c]
            o_ref[rows, :] = r[w + c:w + 2 * c] + _dot(qk_ref[rows, :], _bf(stack_heads(v_new)))
        return carry

    lax.fori_loop(0, seq // rows_per_step, scan, 0)

    def readout(n, carry):
        rows = pl.ds(pl.multiple_of(n * rows_per_step, rows_per_step), rows_per_step)
        o = o_ref[rows, :]
        ms = _group_sum(o * o, ones_bd) * (1.0 / HEAD_DIM)
        o_ref[rows, :] = o * lax.rsqrt(ms + NORM_EPS) * ng * _silu(z_ref[rows, :])
        return carry

    lax.fori_loop(0, seq // rows_per_step, readout, 0)


def _gdn(h, batch, seq, convw, avec, dtb, ng):
    blk = lambda width, cblk: pl.BlockSpec((seq, width), lambda b: (b, cblk))
    full = lambda a: pl.BlockSpec(a.shape, lambda b: (0, 0))
    return pl.pallas_call(
        _gdn_kernel,
        grid=(batch,),
        in_specs=[blk(768, C_GDN_QKV // 768), blk(256, C_GDN_A // 256), blk(256, C_GDN_B // 256),
                  blk(256, C_GDN_Z // 256), full(convw), full(avec), full(dtb), full(ng)],
        out_specs=pl.BlockSpec((seq, GROUP_WIDTH), lambda b: (b, 0)),
        out_shape=jax.ShapeDtypeStruct((batch * seq, GROUP_WIDTH), F32),
        scratch_shapes=[pltpu.VMEM((GROUP_WIDTH, GROUP_WIDTH), F32),
                        pltpu.VMEM((seq, GROUP_WIDTH), F32), pltpu.VMEM((seq, GROUP_WIDTH), F32),
                        pltpu.VMEM((seq // GDN_CHUNK * GROUP_WIDTH, GROUP_WIDTH), F32),
                        pltpu.VMEM((seq, GROUP_WIDTH), BF16), pltpu.VMEM((seq, GROUP_WIDTH), BF16),
                        pltpu.VMEM((seq, GROUP_WIDTH), BF16),
                        pltpu.VMEM((seq // GDN_CHUNK * GROUP_WIDTH, GROUP_WIDTH), BF16)],
        compiler_params=_params("parallel"),
        name="gdn",
    )(h, h, h, h, convw, avec, dtb, ng)


MLA_Q_BLOCK = 512


def _mla_kernel(cq_ref, ckv_ref, kr_ref, krs_ref, cos_ref, sin_ref, qg_ref, kvg_ref,
                wq_ref, wqs_ref, wk_ref, wv_ref, o_ref, q_ref, k_ref, v_ref):
    seq = cq_ref.shape[0]
    tm = ROW_TILE
    scale = (MLA_NOPE + MLA_ROPE) ** -0.5 * LOG2E

    def prep(i, carry):
        rows = pl.ds(pl.multiple_of(i * tm, tm), tm)
        cq = cq_ref[rows, :]
        nq = cq * lax.rsqrt(jnp.sum(cq * cq, axis=-1, keepdims=True) * (1.0 / MLA_Q_RANK) + NORM_EPS) * qg_ref[...]
        nqb = _bf(nq)
        cos1 = cos_ref[rows, :]
        sin1 = sin_ref[rows, :]
        cos2 = jnp.concatenate([cos1, cos1], axis=1)
        sin2 = jnp.concatenate([sin1, sin1], axis=1)
        q_ref[rows, :] = _bf((_dot(nqb, wq_ref[...]) * cos2 + _dot(nqb, wqs_ref[...]) * sin2) * scale)
        ckv = ckv_ref[rows, :]
        nkv = ckv * lax.rsqrt(jnp.mean(ckv * ckv, axis=-1, keepdims=True) + NORM_EPS) * kvg_ref[...]
        nkvb = _bf(nkv)
        kr = kr_ref[rows, :] * cos1 + krs_ref[rows, :] * sin1
        k_ref[rows, :] = _bf(_dot(nkvb, wk_ref[...]) + jnp.concatenate([kr, kr], axis=1))
        v = _bf(_dot(nkvb, wv_ref[...]))
        v_ref[rows, :] = jnp.concatenate([v, jnp.ones((tm, LANES), BF16)], axis=1)
        return carry

    lax.fori_loop(0, seq // tm, prep, 0)

    tq = MLA_Q_BLOCK
    ri = _iota2((tq, tq), 0)
    ci = _iota2((tq, tq), 1)
    diag_ok = ci <= ri
    lane = _iota2((tq, LANES), 1)
    for qi in range(seq // tq):
        q0 = qi * tq
        v_off = v_ref[0:q0, :] if qi else None
        v_diag = v_ref[q0:q0 + tq, :]
        outs = []
        for hh in range(2):
            qh = q_ref[q0:q0 + tq, hh * LANES:(hh + 1) * LANES]
            s_diag = _dot_nt(qh, k_ref[q0:q0 + tq, hh * LANES:(hh + 1) * LANES])
            s_diag = jnp.where(diag_ok, s_diag, MASK_VALUE)
            m = jnp.max(s_diag, axis=-1, keepdims=True)
            if qi:
                s_off = _dot_nt(qh, k_ref[0:q0, hh * LANES:(hh + 1) * LANES])
                m = jnp.maximum(m, jnp.max(s_off, axis=-1, keepdims=True))
            acc = _dot(_bf(jnp.exp2(s_diag - m)), v_diag)
            if qi:
                acc = acc + _dot(_bf(jnp.exp2(s_off - m)), v_off)
            outs.append(acc[:, 0:LANES] / acc[:, LANES:2 * LANES])
        o_ref[q0:q0 + tq, :] = jnp.where(lane < HEAD_DIM, outs[0], outs[1])


def _mla(h, batch, seq, cos_t, sin_t, qg, kvg, wq, wqs, wk, wv):
    col = lambda width, off: pl.BlockSpec((seq, width), lambda b, p: (b, off // width))
    table = pl.BlockSpec((seq, LANES), lambda b, p: (0, 0))
    full = lambda a: pl.BlockSpec(a.shape, lambda b, p: (0, 0))
    pair_cols = lambda a, width: pl.BlockSpec((a.shape[0], width), lambda b, p: (0, p))
    return pl.pallas_call(
        _mla_kernel,
        grid=(batch, 2),
        in_specs=[col(256, C_MLA_CQ), col(LANES, C_MLA_CKV), col(LANES, C_MLA_KR), col(LANES, C_MLA_KRS),
                  table, table, full(qg), full(kvg),
                  pair_cols(wq, 256), pair_cols(wqs, 256), pair_cols(wk, 256), pair_cols(wv, LANES)],
        out_specs=pl.BlockSpec((seq, LANES), lambda b, p: (b, p)),
        out_shape=jax.ShapeDtypeStruct((batch * seq, GROUP_WIDTH), F32),
        scratch_shapes=[pltpu.VMEM((seq, 256), BF16)] * 3,
        compiler_params=_params("parallel", "parallel"),
        name="mla",
    )(h, h, h, h, cos_t, sin_t, qg, kvg, wq, wqs, wk, wv)


HGRN_SCAN_UNROLL = 8
HGRN_SUBTILE = 32


def _hgrn_kernel(q0_ref, q1_ref, f0_ref, f1_ref, i0_ref, i1_ref, g0_ref, g1_ref, lb_ref, ng_ref, o_ref,
                 q_s, kk_s, cum_s, v_s, oi_s, qd0, qd1, kd0, kd1, oc0, oc1, dec_s, incr_s):
    seq = q0_ref.shape[0]
    cc = HGRN_CHUNK
    nch = seq // cc
    w = GROUP_WIDTH
    same_head = (_iota2((w, w), 0) >> 6) == (_iota2((w, w), 1) >> 6)
    ones_bd = jnp.where(same_head, 1.0, 0.0).astype(BF16)
    lb = lb_ref[...]
    ng = ng_ref[...]
    one_m_lb = 1.0 - lb

    def rows_of(p):
        return pl.ds(p, nch, stride=cc)

    def both(ref0, ref1, rows):
        return jnp.concatenate([ref0[rows, :], ref1[rows, :]], axis=1)

    def put(ref0, ref1, rows, val):
        ref0[rows, :] = val[:, 0:LANES]
        ref1[rows, :] = val[:, LANES:2 * LANES]

    quarter = seq // 4

    def split(dst, src):
        for r in range(4):
            dst[r * quarter:(r + 1) * quarter, :] = src[pl.ds(r, quarter, stride=4), :]

    def rows_of4(p):
        return pl.ds((p % 4) * quarter + p // 4, nch, stride=4)

    split(oc0, f0_ref)
    split(oc1, f1_ref)
    split(qd0, q0_ref)
    split(qd1, q1_ref)
    split(kd0, i0_ref)
    split(kd1, i1_ref)

    cum = None
    for p in range(cc):
        fl = both(oc0, oc1, rows_of4(p))
        z = jnp.exp(-jnp.abs(fl))
        r = 1.0 / (1.0 + z)
        zr = z * r
        pos = fl >= 0.0
        log_f = jnp.log(lb + one_m_lb * jnp.where(pos, r, zr))
        cum = log_f if p == 0 else cum + log_f
        q_s[p] = both(qd0, qd1, rows_of4(p))
        kk_s[p] = one_m_lb * jnp.where(pos, zr, r)
        cum_s[p] = cum
        v_s[p] = both(kd0, kd1, rows_of4(p))
    dec_s[...] = jnp.exp(cum)
    for p in range(cc):
        c_p = cum_s[p]
        put(qd0, qd1, rows_of(p), q_s[p] * jnp.exp(c_p))
        put(kd0, kd1, rows_of(p), kk_s[p] * jnp.exp(cum_s[cc - 1] - c_p))

    sub = HGRN_SUBTILE

    def intra(t, carry):
        rs = pl.ds(pl.multiple_of(t * sub, sub), sub)
        for p in range(cc):
            q_p = q_s[p, rs, :]
            c_p = cum_s[p, rs, :]
            xs = [_bf(q_p * kk_s[s, rs, :] * jnp.exp(c_p - cum_s[s, rs, :])) for s in range(p)]
            xs.append(_bf(q_p * kk_s[p, rs, :]))
            sums = _dot(jnp.concatenate(xs, axis=0), ones_bd)
            acc = sums[p * sub:(p + 1) * sub] * v_s[p, rs, :]
            for s in range(p):
                acc = acc + sums[s * sub:(s + 1) * sub] * v_s[s, rs, :]
            oi_s[p, rs, :] = acc
        return carry

    lax.fori_loop(0, nch // sub, intra, 0)

    gs = HGRN_SCAN_UNROLL
    blk_rows = gs * cc
    hw = LANES
    chunk_of_row = _iota2((blk_rows, hw), 0) >> 4
    chunk_masks = [jnp.where(chunk_of_row == j, 1.0, 0.0).astype(BF16) for j in range(gs)]
    same_head_pair = (_iota2((hw, hw), 0) >> 6) == (_iota2((hw, hw), 1) >> 6)
    pairs = ((i0_ref, kd0, qd0, oc0), (i1_ref, kd1, qd1, oc1))

    def expand(t):
        return jnp.concatenate([t * chunk_masks[j] for j in range(gs)], axis=1)

    def increments(g):
        rows = pl.ds(pl.multiple_of(g * blk_rows, blk_rows), blk_rows)
        return jnp.concatenate([_dot_tn(_bf(v_ref[rows, :]), expand(_bf(kd_ref[rows, :])))
                                for v_ref, kd_ref, _, _ in pairs], axis=1)

    nsteps = nch // gs
    incr_s[...] = increments(0)

    def scan(g, sts):
        incr_next = increments(jnp.minimum(g + 1, nsteps - 1))
        rows = pl.ds(pl.multiple_of(g * blk_rows, blk_rows), blk_rows)
        new_sts = []
        for pi, (_, _, qd_ref, oc_ref) in enumerate(pairs):
            st = sts[pi]
            states = []
            for j in range(gs):
                states.append(_bf(st))
                col = (pi * gs + j) * hw
                dec = dec_s[pl.ds(g * gs + j, 1), :][:, pi * hw:(pi + 1) * hw]
                st = st * dec + jnp.where(same_head_pair, incr_s[:, col:col + hw], 0.0)
            oc_ref[rows, :] = _dot_nt(expand(_bf(qd_ref[rows, :])), jnp.concatenate(states, axis=1))
            new_sts.append(st)
        incr_s[...] = incr_next
        return tuple(new_sts)

    lax.fori_loop(0, nsteps, scan, (jnp.zeros((hw, hw), F32), jnp.zeros((hw, hw), F32)))

    split(qd0, oc0)
    split(qd1, oc1)
    split(kd0, g0_ref)
    split(kd1, g1_ref)
    for p in range(cc):
        o = oi_s[p] + both(qd0, qd1, rows_of4(p))
        ms = _group_sum(o * o, ones_bd) * (1.0 / HEAD_DIM)
        put(oc0, oc1, rows_of(p), o * lax.rsqrt(ms + NORM_EPS) * ng * _silu(both(kd0, kd1, rows_of4(p))))
    o_ref[:, 0:LANES] = oc0[...]
    o_ref[:, LANES:2 * LANES] = oc1[...]


def _hgrn(h, batch, seq, lb, ng):
    base = C_HGRN // LANES
    half = lambda j: pl.BlockSpec((seq, LANES), lambda b: (b, base + j))
    full = lambda a: pl.BlockSpec(a.shape, lambda b: (0, 0))
    nch = seq // HGRN_CHUNK
    tiles = pltpu.VMEM((HGRN_CHUNK, nch, GROUP_WIDTH), F32)
    nat = pltpu.VMEM((seq, LANES), F32)
    return pl.pallas_call(
        _hgrn_kernel,
        grid=(batch,),
        in_specs=[half(j) for j in range(8)] + [full(lb), full(ng)],
        out_specs=pl.BlockSpec((seq, GROUP_WIDTH), lambda b: (b, 0)),
        out_shape=jax.ShapeDtypeStruct((batch * seq, GROUP_WIDTH), F32),
        scratch_shapes=[tiles] * 5 + [nat] * 6 + [pltpu.VMEM((nch, GROUP_WIDTH), F32),
                                                  pltpu.VMEM((LANES, 2 * HGRN_SCAN_UNROLL * LANES), F32)],
        compiler_params=_params("parallel"),
        name="hgrn",
    )(h, h, h, h, h, h, h, h, lb, ng)


def _dsa_kernel(q_ref, k_ref, v_ref, o_ref, num_ref, m_ref, l_ref):
    seq = q_ref.shape[0]
    blk = DSA_BLOCK
    pair = pl.program_id(1)
    qi = _iota2((blk, 2 * blk), 0)
    ki = _iota2((blk, 2 * blk), 1)
    steps = blk + qi - ki
    lane = _iota2((blk, LANES), 1)
    first = lane < HEAD_DIM
    qscale = HEAD_DIM ** -0.5 * LOG2E
    slopes = [jnp.where(pair == 0, ALIBI_SLOPES[hh], ALIBI_SLOPES[2 + hh]) * LOG2E for hh in range(2)]

    for bi, (window, dil) in enumerate(DSA_BRANCHES):
        nblk = seq // (dil * blk)
        in_window = (steps >= 0) & (steps <= window // dil)
        dist = (steps * dil).astype(F32)
        two_blocks = nblk > 1
        if two_blocks:
            bias = [jnp.where(in_window, -slopes[hh] * dist, MASK_VALUE) for hh in range(2)]
            bias_first = [jnp.where(ki >= blk, bias[hh], MASK_VALUE) for hh in range(2)]
        else:
            bias_first = [jnp.where(in_window, -slopes[hh] * dist, MASK_VALUE)[:, blk:] for hh in range(2)]
            bias = bias_first

        def attend_group(blocks, bi=bi, dil=dil, bias=bias, bias_first=bias_first, two_blocks=two_blocks):
            if dil == 1:
                ld = lambda ref, s0: ref[pl.ds(s0, blk), :]
                dst = lambda s0: pl.ds(bi * seq + s0, blk)
            else:
                ld = lambda ref, s0: ref[pl.ds(s0, blk, stride=dil), :]
                dst = lambda s0: pl.ds(bi * seq + s0, blk, stride=dil)
            tiles, vbs = [], []
            for start, is_first in blocks:
                qb = ld(q_ref, start) * qscale
                if two_blocks:
                    prev = start if is_first else start - dil * blk
                    kb = _bf(jnp.concatenate([ld(k_ref, prev), ld(k_ref, start)], axis=0))
                    vb = _bf(jnp.concatenate([ld(v_ref, prev), ld(v_ref, start)], axis=0))
                else:
                    kb = _bf(ld(k_ref, start))
                    vb = _bf(ld(v_ref, start))
                vbs.append(jnp.concatenate([vb, jnp.ones(vb.shape, BF16)], axis=1))
                qm = _bf(jnp.concatenate([jnp.where(first, qb, 0.0), jnp.where(first, 0.0, qb)], axis=0))
                b2 = bias_first if is_first else bias
                tiles.append(_dot_nt(qm, kb) + jnp.concatenate([b2[0], b2[1]], axis=0))
            s = jnp.concatenate(tiles, axis=0)
            m = jnp.max(s, axis=-1, keepdims=True)
            pb = _bf(jnp.exp2(s - m))
            for i, (start, _) in enumerate(blocks):
                r0, r1, r2 = 2 * i * blk, (2 * i + 1) * blk, (2 * i + 2) * blk
                pv2 = _dot(pb[r0:r2], vbs[i])
                pv = jnp.where(jnp.concatenate([first, first], axis=1), pv2[0:blk], pv2[blk:2 * blk])
                num_ref[dst(start), :] = pv[:, 0:LANES]
                l_ref[dst(start), :] = pv[:, LANES:2 * LANES]
                m_ref[dst(start), :] = jnp.where(first, m[r0:r1], m[r1:r2])

        group = DSA_GROUP
        if nblk == 1:
            def body(g, carry, attend_group=attend_group):
                attend_group([(g * group + j, True) for j in range(group)])
                return carry
            lax.fori_loop(0, dil // group, body, 0)
        elif nblk < group:
            per = group // nblk

            def body(g, carry, attend_group=attend_group, dil=dil, nblk=nblk, per=per):
                attend_group([(g * per + j + dil * blk * n, n == 0) for j in range(per) for n in range(nblk)])
                return carry
            lax.fori_loop(0, dil // per, body, 0)
        else:
            assert dil == 1 and nblk % group == 0
            attend_group([(n * blk, n == 0) for n in range(group)])

            def body(g, carry, attend_group=attend_group):
                attend_group([(pl.multiple_of((g * group + j) * blk, blk), False) for j in range(group)])
                return carry
            lax.fori_loop(1, nblk // group, body, 0)

    def merge(n, carry):
        r0 = pl.multiple_of(n * blk, blk)
        rows = [pl.ds(bi * seq + r0, blk) for bi in range(len(DSA_BRANCHES))]
        m0, m1, m2 = [m_ref[r, :] for r in rows]
        mm = jnp.maximum(jnp.maximum(m0, m1), m2)
        w0, w1, w2 = jnp.exp2(m0 - mm), jnp.exp2(m1 - mm), jnp.exp2(m2 - mm)
        num = w0 * num_ref[rows[0], :] + w1 * num_ref[rows[1], :] + w2 * num_ref[rows[2], :]
        den = w0 * l_ref[rows[0], :] + w1 * l_ref[rows[1], :] + w2 * l_ref[rows[2], :]
        o_ref[pl.ds(r0, blk), :] = num / den
        return carry

    lax.fori_loop(0, seq // blk, merge, 0)


def _dsa(h, batch, seq):
    base = C_DSA // LANES
    blk = lambda j: pl.BlockSpec((seq, LANES), lambda b, p: (b, base + 2 * j + p))
    nb = len(DSA_BRANCHES)
    return pl.pallas_call(
        _dsa_kernel,
        grid=(batch, 2),
        in_specs=[blk(0), blk(1), blk(2)],
        out_specs=pl.BlockSpec((seq, LANES), lambda b, p: (b, p)),
        out_shape=jax.ShapeDtypeStruct((batch * seq, GROUP_WIDTH), F32),
        scratch_shapes=[pltpu.VMEM((nb * seq, LANES), F32), pltpu.VMEM((nb * seq, LANES), F32),
                        pltpu.VMEM((nb * seq, LANES), F32)],
        compiler_params=_params("parallel", "parallel"),
        name="dsa",
    )(h, h, h)


FF_CHUNK = 1024
POST_ROW_TILE = 512


def _post_kernel(oa_ref, ob_ref, oc_ref, od_ref, x_ref, wo_ref, g1_ref, b1_ref, w1_ref, w2_ref, g2_ref, b2_ref,
                 y_ref):
    mixer_out = jnp.concatenate([_bf(oa_ref[...]), _bf(ob_ref[...]), _bf(oc_ref[...]), _bf(od_ref[...])], axis=1)
    mixed = _dot(mixer_out, wo_ref[...])
    x = _layer_norm_rows(DEEPNORM_ALPHA * x_ref[...] + mixed, g1_ref[...], b1_ref[...])
    xb = _bf(x)
    acc = jnp.zeros(x.shape, F32)
    for c in range(D_FF // FF_CHUNK):
        hmid = _dot(xb, w1_ref[:, c * FF_CHUNK:(c + 1) * FF_CHUNK])
        hmid = jnp.square(jnp.maximum(hmid, 0.0))
        acc = acc + _dot(_bf(hmid), w2_ref[c * FF_CHUNK:(c + 1) * FF_CHUNK, :])
    y_ref[...] = _layer_norm_rows(DEEPNORM_ALPHA * x + acc, g2_ref[...], b2_ref[...])


def _post(oa, ob, oc, od, x2d, wo, g1, b1, w1, w2, g2, b2, layer):
    n = x2d.shape[0]
    tm = POST_ROW_TILE
    grp = pl.BlockSpec((tm, GROUP_WIDTH), lambda i: (i, 0))
    row = pl.BlockSpec((tm, D_MODEL), lambda i: (i, 0))
    resident = lambda a: _layer_block(a, layer, pipeline_mode=pl.Buffered(1))
    return pl.pallas_call(
        _post_kernel,
        grid=(n // tm,),
        in_specs=[grp, grp, grp, grp, row, resident(wo), resident(g1), resident(b1),
                  resident(w1), resident(w2), resident(g2), resident(b2)],
        out_specs=row,
        out_shape=jax.ShapeDtypeStruct((n, D_MODEL), F32),
        compiler_params=_params("parallel"),
        name="post",
    )(oa, ob, oc, od, x2d, wo, g1, b1, w1, w2, g2, b2)


def _expand_heads(wrows):
    return jnp.repeat(wrows, HEAD_DIM, axis=-2)


def _arrange_w_in(w):
    wt = jnp.swapaxes(_bf(w), -1, -2)
    pts = np.cumsum(IN_SIZES)[:-1].tolist()
    (a_qkv, a_a, a_b, a_z, b_cq, b_ckv, b_kr, c_q, c_f, c_i, c_g, d_qkv) = jnp.split(wt, pts, axis=-2)
    z = lambda n: jnp.zeros(wt.shape[:-2] + (n, wt.shape[-1]), BF16)
    half = MLA_ROPE // 2
    kr_sw = jnp.concatenate([b_kr[..., half:, :], b_kr[..., :half, :]], axis=-2)
    rows = [a_qkv, _expand_heads(a_a), _expand_heads(a_b), a_z,
            b_cq, z(256 - MLA_Q_RANK), b_ckv,
            z(MLA_NOPE), b_kr, z(LANES - MLA_NOPE - MLA_ROPE),
            z(MLA_NOPE), kr_sw, z(LANES - MLA_NOPE - MLA_ROPE), z(C_HGRN - C_MLA_KRS - LANES),
            c_q, c_f, c_i, c_g, d_qkv]
    return jnp.concatenate(rows, axis=-2)


def _arrange_mla_weights(w_uq, w_ukv):
    lead = w_uq.shape[:-2]
    rq = w_uq.shape[-2]
    wq = _bf(w_uq).reshape(lead + (rq, N_HEADS, MLA_NOPE + MLA_ROPE))
    half = MLA_ROPE // 2
    zq = jnp.zeros(lead + (rq, N_HEADS, LANES - MLA_NOPE - MLA_ROPE), BF16)
    q_main = jnp.concatenate([wq, zq], axis=-1)
    rope = wq[..., MLA_NOPE:]
    rope_sw = jnp.concatenate([rope[..., half:], rope[..., :half]], axis=-1)
    q_swap = jnp.concatenate([jnp.zeros(lead + (rq, N_HEADS, MLA_NOPE), BF16), rope_sw, zq], axis=-1)
    pad_rows = lambda m: jnp.concatenate([m, jnp.zeros(lead + (256 - rq, m.shape[-1]), BF16)], axis=-2)
    q_main = pad_rows(q_main.reshape(lead + (rq, N_HEADS * LANES)))
    q_swap = pad_rows(q_swap.reshape(lead + (rq, N_HEADS * LANES)))
    rkv = w_ukv.shape[-2]
    wkv = _bf(w_ukv).reshape(lead + (rkv, N_HEADS, MLA_NOPE + HEAD_DIM))
    k_w = jnp.concatenate([wkv[..., :MLA_NOPE], jnp.zeros(lead + (rkv, N_HEADS, LANES - MLA_NOPE), BF16)], axis=-1)
    v_w = wkv[..., MLA_NOPE:]
    return (q_main, q_swap, k_w.reshape(lead + (rkv, N_HEADS * LANES)),
            v_w.reshape(lead + (rkv, N_HEADS * HEAD_DIM)))


def _rope_tables(seq):
    half = MLA_ROPE // 2
    pos = jnp.arange(seq, dtype=F32)
    inv_freq = ROPE_BASE ** (-jnp.arange(half, dtype=F32) / half)
    ang = pos[:, None] * inv_freq[None, :]
    cos, sin = jnp.cos(ang), jnp.sin(ang)
    ones = jnp.ones((seq, MLA_NOPE), F32)
    zeros = jnp.zeros((seq, MLA_NOPE), F32)
    tail1 = jnp.ones((seq, LANES - MLA_NOPE - MLA_ROPE), F32)
    tail0 = jnp.zeros((seq, LANES - MLA_NOPE - MLA_ROPE), F32)
    cos_t = jnp.concatenate([ones, cos, cos, tail1], axis=1)
    sin_t = jnp.concatenate([zeros, -sin, sin, tail0], axis=1)
    return cos_t, sin_t


def _tile_heads(vec):
    return jnp.tile(vec.astype(F32), N_HEADS)[..., None, :]


def kernel(x, w_in, gdn_conv_w, gdn_a_log, gdn_dt_bias, gdn_norm_g, mla_q_norm_g, mla_kv_norm_g,
           mla_w_uq, mla_w_ukv, hgrn_lb_logits, hgrn_norm_g, w_out, ln1_g, ln1_b, w_ff1, w_ff2,
           ln2_g, ln2_b):
    batch, seq, d_model = x.shape
    assert d_model == D_MODEL and seq % (16 * DSA_BLOCK) == 0 and (batch * seq) % ROW_TILE == 0
    x2d = x.reshape(batch * seq, d_model)
    p_lb = jax.nn.softmax(hgrn_lb_logits.astype(F32), axis=0)
    lower_bounds = (jnp.cumsum(p_lb, axis=0) - p_lb[:1])[:, None, :]
    cos_t, sin_t = _rope_tables(seq)
    w_in_a = _arrange_w_in(w_in)
    avec = jnp.repeat(-jnp.exp(gdn_a_log.astype(F32)), HEAD_DIM, axis=-1)[:, None, :]
    dtb = jnp.repeat(gdn_dt_bias.astype(F32), HEAD_DIM, axis=-1)[:, None, :]
    conv_w = gdn_conv_w.astype(F32)
    gdn_g = _tile_heads(gdn_norm_g)
    hgrn_g = _tile_heads(hgrn_norm_g)
    wq, wqs, wk, wv = _arrange_mla_weights(mla_w_uq, mla_w_ukv)
    qg = jnp.concatenate([mla_q_norm_g.astype(F32), jnp.zeros((DEPTH, 256 - MLA_Q_RANK), F32)], axis=-1)[:, None, :]
    kvg = mla_kv_norm_g.astype(F32)[:, None, :]
    wo, w1, w2 = _bf(w_out), _bf(w_ff1), _bf(w_ff2)
    g1, b1, g2, b2 = (t.astype(F32)[:, None, :] for t in (ln1_g, ln1_b, ln2_g, ln2_b))
    for l in range(DEPTH):
        h = _inproj(x2d, w_in_a, l)
        o_a = _gdn(h, batch, seq, conv_w[l], avec[l], dtb[l], gdn_g[l])
        o_b = _mla(h, batch, seq, cos_t, sin_t, qg[l], kvg[l], wq[l], wqs[l], wk[l], wv[l])
        o_c = _hgrn(h, batch, seq, lower_bounds[l], hgrn_g[l])
        o_d = _dsa(h, batch, seq)
        x2d = _post(o_a, o_b, o_c, o_d, x2d, wo, g1, b1, w1, w2, g2, b2, l)
    return x2d.reshape(batch, seq, d_model)
```

```python
import functools
import math

import numpy as np
import jax
import jax.numpy as jnp
from jax import lax
from jax.experimental import pallas as pl
from jax.experimental.pallas import tpu as pltpu

F32 = jnp.float32
BF16 = jnp.bfloat16

D_MODEL = 1024
DEPTH = 2
GROUP_WIDTH = 256
N_HEADS = 4
HEAD_DIM = 64
GDN_CONV = 4
GDN_CHUNK = 64
MLA_NOPE = 64
MLA_ROPE = 32
MLA_Q_RANK = 192
MLA_KV_RANK = 128
ROPE_BASE = 10000.0
HGRN_CHUNK = 16
DSA_BRANCHES = ((128, 1), (512, 4), (2048, 16))
DSA_BLOCK = 128
DSA_GROUP = 8
ALIBI_SLOPES = tuple(2.0 ** (-8.0 * (j + 1) / N_HEADS) for j in range(N_HEADS))
D_FF = 4 * D_MODEL
DEEPNORM_ALPHA = (2 * DEPTH) ** 0.25
NORM_EPS = 1e-6
MASK_VALUE = -1e30
LOG2E = 1.4426950408889634

LANES = 128
SUBLANES = 8
VMEM_LIMIT_BYTES = 58 * 1024 * 1024

C_GDN_QKV = 0
C_GDN_A = 768
C_GDN_B = 1024
C_GDN_Z = 1280
C_MLA_CQ = 1536
C_MLA_CKV = 1792
C_MLA_KR = 1920
C_MLA_KRS = 2048
C_HGRN = 2304
C_DSA = 3328
C_TOTAL = 4096
IN_SIZES = (768, 4, 4, 256, 192, 128, 32, 256, 256, 256, 256, 768)
ROW_TILE = 512


def _bf(x):
    return x.astype(BF16)


def _dot(a, b):
    return jnp.dot(a, b, preferred_element_type=F32)


def _dot_nt(a, b):
    return lax.dot_general(a, b, (((1,), (1,)), ((), ())), preferred_element_type=F32)


def _dot_tn(a, b):
    return lax.dot_general(a, b, (((0,), (0,)), ((), ())), preferred_element_type=F32)


def _split3(x):
    hi = _bf(x)
    r1 = x - hi.astype(F32)
    mid = _bf(r1)
    lo = _bf(r1 - mid.astype(F32))
    return hi, mid, lo


def _dot_sel_r(x, sel):
    hi, mid, lo = _split3(x)
    return _dot(hi, sel) + _dot(mid, sel) + _dot(lo, sel)


def _dot_sel_l(sel, x):
    n = x.shape[1]
    r = _dot(sel, jnp.concatenate(_split3(x), axis=1))
    return r[:, 0:n] + r[:, n:2 * n] + r[:, 2 * n:3 * n]


def _sigmoid(x):
    return 0.5 + 0.5 * jnp.tanh(0.5 * x)


def _silu(x):
    return x * _sigmoid(x)


def _softplus(x):
    return jnp.maximum(x, 0.0) - jnp.log(_sigmoid(jnp.abs(x)))


def _iota2(shape, axis):
    return lax.broadcasted_iota(jnp.int32, shape, axis)


def _head_block_ones(n):
    r = _iota2((n, n), 0) >> 6
    c = _iota2((n, n), 1) >> 6
    return jnp.where(r == c, 1.0, 0.0).astype(BF16)


def _group_sum(x, ones_bd):
    return _dot(_bf(x), ones_bd)


def _layer_norm_rows(y, g, b):
    mu = jnp.mean(y, axis=-1, keepdims=True)
    d = y - mu
    var = jnp.mean(d * d, axis=-1, keepdims=True)
    return d * lax.rsqrt(var + NORM_EPS) * g + b


def _params(*sem):
    return pltpu.CompilerParams(dimension_semantics=sem, vmem_limit_bytes=VMEM_LIMIT_BYTES)


def _inproj_kernel(x_ref, w_ref, o_ref):
    xb = _bf(x_ref[...])
    for c in range(C_TOTAL // 512):
        o_ref[:, c * 512:(c + 1) * 512] = _dot_nt(xb, w_ref[c * 512:(c + 1) * 512, :])


def _layer_block(stacked, layer, **kw):
    return pl.BlockSpec((None,) + stacked.shape[1:], lambda *_: (layer, 0, 0), **kw)


def _inproj(x2d, w_layers, layer):
    n = x2d.shape[0]
    return pl.pallas_call(
        _inproj_kernel,
        grid=(n // ROW_TILE,),
        in_specs=[pl.BlockSpec((ROW_TILE, D_MODEL), lambda i: (i, 0)), _layer_block(w_layers, layer)],
        out_specs=pl.BlockSpec((ROW_TILE, C_TOTAL), lambda i: (i, 0)),
        out_shape=jax.ShapeDtypeStruct((n, C_TOTAL), F32),
        compiler_params=_params("parallel"),
        name="inproj",
    )(x2d, w_layers)


GDN_GROUP = 8


def _gdn_kernel(qkv_ref, a_ref, b_ref, z_ref, convw_ref, avec_ref, dtb_ref, ng_ref, o_ref,
                s_ref, u_ref, egl_ref, ku_ref, w_ref, qd_ref, qk_ref, kw_ref):
    seq = qkv_ref.shape[0]
    c = GDN_CHUNK
    w = GROUP_WIDTH
    rows_per_step = GDN_GROUP * c
    s_ref[...] = jnp.zeros_like(s_ref)

    same_head = (_iota2((w, w), 0) >> 6) == (_iota2((w, w), 1) >> 6)
    ones_bd = jnp.where(same_head, 1.0, 0.0).astype(BF16)
    rr = _iota2((rows_per_step, rows_per_step), 0)
    rc = _iota2((rows_per_step, rows_per_step), 1)
    same_chunk = (rr >> 6) == (rc >> 6)
    tril_chunks = jnp.where(same_chunk & (rc <= rr), 1.0, 0.0).astype(BF16)
    ones_chunks = jnp.where(same_chunk, 1.0, 0.0).astype(BF16)
    li = _iota2((rows_per_step, w), 0) & (c - 1)
    lj = _iota2((rows_per_step, w), 1) & (c - 1)
    causal = lj <= li
    strict = lj < li
    eye = lj == li
    lane_head = _iota2((c, w), 1) >> 6
    first_rows = _iota2((SUBLANES, 3 * w), 0)

    def stack_heads(t):
        return jnp.concatenate([jnp.where(lane_head == h, t, jnp.zeros_like(t)) for h in range(N_HEADS)], axis=0)

    def block_diag(t):
        return jnp.where(same_head, jnp.concatenate([t] * N_HEADS, axis=0), jnp.zeros((w, w), t.dtype))

    convw = convw_ref[...]
    avec = avec_ref[...]
    dtb = dtb_ref[...]
    ng = ng_ref[...]
    chunks = [slice(i * c, (i + 1) * c) for i in range(GDN_GROUP)]

    def front(n):
        r0 = pl.multiple_of(n * rows_per_step, rows_per_step)
        rows = pl.ds(r0, rows_per_step)
        cur = qkv_ref[rows, :]
        prev = qkv_ref[pl.ds(pl.multiple_of(jnp.maximum(r0 - SUBLANES, 0), SUBLANES), SUBLANES), :]
        prev = jnp.where(n > 0, prev, 0.0)
        y = cur * convw[GDN_CONV - 1:GDN_CONV, :]
        for j in range(1, GDN_CONV):
            shifted = pltpu.roll(cur, j, 0)
            head = jnp.where(first_rows < j, pltpu.roll(prev, j, 0), shifted[0:SUBLANES, :])
            shifted = jnp.concatenate([head, shifted[SUBLANES:, :]], axis=0)
            y = y + shifted * convw[GDN_CONV - 1 - j:GDN_CONV - j, :]
        y = _silu(y)
        q = y[:, 0:w]
        k = y[:, w:2 * w]
        v = y[:, 2 * w:3 * w]
        q = q * lax.rsqrt(_group_sum(q * q, ones_bd) + NORM_EPS) * (HEAD_DIM ** -0.5)
        k = k * lax.rsqrt(_group_sum(k * k, ones_bd) + NORM_EPS)
        beta = _sigmoid(b_ref[rows, :])
        gstep = avec * _softplus(a_ref[rows, :] + dtb)
        g = _dot_sel_l(tril_chunks, gstep)
        g_last = jnp.concatenate([jnp.broadcast_to(g[sl][c - 1:c, :], (c, w)) for sl in chunks], axis=0)
        gr = _dot_sel_l(ones_chunks, jnp.where(eye, g, 0.0))
        decay = jnp.where(causal, jnp.exp(jnp.where(causal, g - gr, 0.0)), 0.0)
        eg = jnp.exp(g)
        kb = k * beta
        kbb = _bf(kb)
        qb = _bf(q)
        prods = [_dot_nt(jnp.concatenate([kbb[sl], qb[sl]], axis=0), _bf(stack_heads(k[sl]))) for sl in chunks]
        lower = jnp.where(strict, jnp.concatenate([p[0:c] for p in prods], axis=0) * decay, 0.0)
        qk = jnp.concatenate([p[c:2 * c] for p in prods], axis=0) * decay
        qd_ref[rows, :] = _bf(q * eg)
        qk_ref[rows, :] = _bf(qk)
        egl_ref[rows, :] = jnp.exp(g_last)
        return lower, v * beta, kb * eg, _bf(k * jnp.exp(g_last - g))

    def back(n, lower, vb, kbg, k_dec):
        rows = pl.ds(pl.multiple_of(n * rows_per_step, rows_per_step), rows_per_step)
        m = -lower
        t = jnp.where(eye, 1.0, 0.0) + m
        for level in range(6):
            mb = _bf(m)
            tb = _bf(t)
            new_m, t_m = [], []
            for sl in chunks:
                m_bd = block_diag(mb[sl])
                if level == 0:
                    new_m.append(_dot(mb[sl], m_bd))
                elif level < 5:
                    both = _dot(jnp.concatenate([mb[sl], tb[sl]], axis=0), m_bd)
                    new_m.append(both[0:c])
                    t_m.append(both[c:2 * c])
                else:
                    t_m.append(_dot(tb[sl], m_bd))
            if level < 5:
                m = jnp.concatenate(new_m, axis=0)
            if level > 0:
                t = t + jnp.concatenate(t_m, axis=0)
        tb = _bf(t)
        uw = [_dot(tb[sl], _bf(jnp.concatenate([stack_heads(vb[sl]), stack_heads(kbg[sl])], axis=1)))
              for sl in chunks]
        u_ref[rows, :] = jnp.concatenate([x[:, 0:w] for x in uw], axis=0)
        w_ref[rows, :] = _bf(jnp.concatenate([x[:, w:2 * w] for x in uw], axis=0))
        for ci, sl in enumerate(chunks):
            kw_ku = _dot_tn(k_dec[sl], _bf(uw[ci]))
            mat_rows = pl.ds(pl.multiple_of((n * GDN_GROUP + ci) * w, w), w)
            ku_ref[mat_rows, :] = kw_ku[:, 0:w]
            kw_ref[mat_rows, :] = _bf(kw_ku[:, w:2 * w])

    def prep(n, carry):
        back(n, *front(n))
        return carry

    lax.fori_loop(0, seq // rows_per_step, prep, 0)

    def scan(n, carry):
        step0 = pl.multiple_of(n * rows_per_step, rows_per_step)
        for ci in range(GDN_GROUP):
            rows = pl.ds(step0 + ci * c, c)
            mat_rows = pl.ds(pl.multiple_of((n * GDN_GROUP + ci) * w, w), w)
            s_bd = s_ref[...]
            lhs = jnp.concatenate([kw_ref[mat_rows, :], w_ref[rows, :], qd_ref[rows, :]], axis=0)
            r = _dot(lhs, _bf(s_bd))
            s_ref[...] = s_bd * egl_ref[pl.ds(step0 + ci * c, 1), :] + jnp.where(
                same_head, ku_ref[mat_rows, :] - r[0:w], 0.0)
            v_new = u_ref[rows, :] - r[w:w + c]
            o_ref[rows, :] = r[w + c:w + 2 * c] + _dot(qk_ref[rows, :], _bf(stack_heads(v_new)))
        return carry

    lax.fori_loop(0, seq // rows_per_step, scan, 0)

    def readout(n, carry):
        rows = pl.ds(pl.multiple_of(n * rows_per_step, rows_per_step), rows_per_step)
        o = o_ref[rows, :]
        ms = _group_sum(o * o, ones_bd) * (1.0 / HEAD_DIM)
        o_ref[rows, :] = o * lax.rsqrt(ms + NORM_EPS) * ng * _silu(z_ref[rows, :])
        return carry

    lax.fori_loop(0, seq // rows_per_step, readout, 0)


def _gdn(h, batch, seq, convw, avec, dtb, ng):
    blk = lambda width, cblk: pl.BlockSpec((seq, width), lambda b: (b, cblk))
    full = lambda a: pl.BlockSpec(a.shape, lambda b: (0, 0))
    return pl.pallas_call(
        _gdn_kernel,
        grid=(batch,),
        in_specs=[blk(768, C_GDN_QKV // 768), blk(256, C_GDN_A // 256), blk(256, C_GDN_B // 256),
                  blk(256, C_GDN_Z // 256), full(convw), full(avec), full(dtb), full(ng)],
        out_specs=pl.BlockSpec((seq, GROUP_WIDTH), lambda b: (b, 0)),
        out_shape=jax.ShapeDtypeStruct((batch * seq, GROUP_WIDTH), F32),
        scratch_shapes=[pltpu.VMEM((GROUP_WIDTH, GROUP_WIDTH), F32),
                        pltpu.VMEM((seq, GROUP_WIDTH), F32), pltpu.VMEM((seq, GROUP_WIDTH), F32),
                        pltpu.VMEM((seq // GDN_CHUNK * GROUP_WIDTH, GROUP_WIDTH), F32),
                        pltpu.VMEM((seq, GROUP_WIDTH), BF16), pltpu.VMEM((seq, GROUP_WIDTH), BF16),
                        pltpu.VMEM((seq, GROUP_WIDTH), BF16),
                        pltpu.VMEM((seq // GDN_CHUNK * GROUP_WIDTH, GROUP_WIDTH), BF16)],
        compiler_params=_params("parallel"),
        name="gdn",
    )(h, h, h, h, convw, avec, dtb, ng)


MLA_Q_BLOCK = 512


def _mla_kernel(cq_ref, ckv_ref, kr_ref, krs_ref, cos_ref, sin_ref, qg_ref, kvg_ref,
                wq_ref, wqs_ref, wk_ref, wv_ref, o_ref, q_ref, k_ref, v_ref):
    seq = cq_ref.shape[0]
    tm = ROW_TILE
    scale = (MLA_NOPE + MLA_ROPE) ** -0.5 * LOG2E

    def prep(i, carry):
        rows = pl.ds(pl.multiple_of(i * tm, tm), tm)
        cq = cq_ref[rows, :]
        nq = cq * lax.rsqrt(jnp.sum(cq * cq, axis=-1, keepdims=True) * (1.0 / MLA_Q_RANK) + NORM_EPS) * qg_ref[...]
        nqb = _bf(nq)
        cos1 = cos_ref[rows, :]
        sin1 = sin_ref[rows, :]
        cos4 = jnp.concatenate([cos1] * N_HEADS, axis=1)
        sin4 = jnp.concatenate([sin1] * N_HEADS, axis=1)
        q_ref[rows, :] = _bf((_dot(nqb, wq_ref[...]) * cos4 + _dot(nqb, wqs_ref[...]) * sin4) * scale)
        ckv = ckv_ref[rows, :]
        nkv = ckv * lax.rsqrt(jnp.mean(ckv * ckv, axis=-1, keepdims=True) + NORM_EPS) * kvg_ref[...]
        nkvb = _bf(nkv)
        kr = kr_ref[rows, :] * cos1 + krs_ref[rows, :] * sin1
        k_ref[rows, :] = _bf(_dot(nkvb, wk_ref[...]) + jnp.concatenate([kr] * N_HEADS, axis=1))
        v = _bf(_dot(nkvb, wv_ref[...]))
        ones = jnp.ones((tm, LANES), BF16)
        v_ref[rows, :] = jnp.concatenate([v[:, 0:LANES], ones, v[:, LANES:2 * LANES], ones], axis=1)
        return carry

    lax.fori_loop(0, seq // tm, prep, 0)

    tq = MLA_Q_BLOCK
    ri = _iota2((tq, tq), 0)
    ci = _iota2((tq, tq), 1)
    diag_ok = ci <= ri
    lane = _iota2((tq, LANES), 1)
    for pair in range(2):
        vc = slice(pair * 2 * LANES, (pair + 1) * 2 * LANES)
        for qi in range(seq // tq):
            q0 = qi * tq
            v_off = v_ref[0:q0, vc] if qi else None
            v_diag = v_ref[q0:q0 + tq, vc]
            outs = []
            for hh in range(2):
                hc = slice((2 * pair + hh) * LANES, (2 * pair + hh + 1) * LANES)
                qh = q_ref[q0:q0 + tq, hc]
                s_diag = _dot_nt(qh, k_ref[q0:q0 + tq, hc])
                s_diag = jnp.where(diag_ok, s_diag, MASK_VALUE)
                m = jnp.max(s_diag, axis=-1, keepdims=True)
                if qi:
                    s_off = _dot_nt(qh, k_ref[0:q0, hc])
                    m = jnp.maximum(m, jnp.max(s_off, axis=-1, keepdims=True))
                acc = _dot(_bf(jnp.exp2(s_diag - m)), v_diag)
                if qi:
                    acc = acc + _dot(_bf(jnp.exp2(s_off - m)), v_off)
                outs.append(acc[:, 0:LANES] / acc[:, LANES:2 * LANES])
            o_ref[q0:q0 + tq, pair * LANES:(pair + 1) * LANES] = jnp.where(lane < HEAD_DIM, outs[0], outs[1])


def _mla(h, batch, seq, cos_t, sin_t, qg, kvg, wq, wqs, wk, wv):
    col = lambda width, off: pl.BlockSpec((seq, width), lambda b: (b, off // width))
    table = pl.BlockSpec((seq, LANES), lambda b: (0, 0))
    full = lambda a: pl.BlockSpec(a.shape, lambda b: (0, 0))
    return pl.pallas_call(
        _mla_kernel,
        grid=(batch,),
        in_specs=[col(256, C_MLA_CQ), col(LANES, C_MLA_CKV), col(LANES, C_MLA_KR), col(LANES, C_MLA_KRS),
                  table, table, full(qg), full(kvg), full(wq), full(wqs), full(wk), full(wv)],
        out_specs=pl.BlockSpec((seq, GROUP_WIDTH), lambda b: (b, 0)),
        out_shape=jax.ShapeDtypeStruct((batch * seq, GROUP_WIDTH), F32),
        scratch_shapes=[pltpu.VMEM((seq, N_HEADS * LANES), BF16)] * 3,
        compiler_params=_params("parallel"),
        name="mla",
    )(h, h, h, h, cos_t, sin_t, qg, kvg, wq, wqs, wk, wv)


HGRN_SCAN_UNROLL = 8
HGRN_SUBTILE = 32


def _hgrn_kernel(q0_ref, q1_ref, f0_ref, f1_ref, i0_ref, i1_ref, g0_ref, g1_ref, lb_ref, ng_ref, o_ref,
                 q_s, kk_s, cum_s, v_s, oi_s, qd0, qd1, kd0, kd1, oc0, oc1, dec_s, incr_s):
    seq = q0_ref.shape[0]
    cc = HGRN_CHUNK
    nch = seq // cc
    w = GROUP_WIDTH
    same_head = (_iota2((w, w), 0) >> 6) == (_iota2((w, w), 1) >> 6)
    ones_bd = jnp.where(same_head, 1.0, 0.0).astype(BF16)
    lb = lb_ref[...]
    ng = ng_ref[...]
    one_m_lb = 1.0 - lb

    def rows_of(p):
        return pl.ds(p, nch, stride=cc)

    def both(ref0, ref1, rows):
        return jnp.concatenate([ref0[rows, :], ref1[rows, :]], axis=1)

    def put(ref0, ref1, rows, val):
        ref0[rows, :] = val[:, 0:LANES]
        ref1[rows, :] = val[:, LANES:2 * LANES]

    quarter = seq // 4

    def split(dst, src):
        for r in range(4):
            dst[r * quarter:(r + 1) * quarter, :] = src[pl.ds(r, quarter, stride=4), :]

    def rows_of4(p):
        return pl.ds((p % 4) * quarter + p // 4, nch, stride=4)

    split(oc0, f0_ref)
    split(oc1, f1_ref)
    split(qd0, q0_ref)
    split(qd1, q1_ref)
    split(kd0, i0_ref)
    split(kd1, i1_ref)

    cum = None
    for p in range(cc):
        fl = both(oc0, oc1, rows_of4(p))
        z = jnp.exp(-jnp.abs(fl))
        r = 1.0 / (1.0 + z)
        zr = z * r
        pos = fl >= 0.0
        log_f = jnp.log(lb + one_m_lb * jnp.where(pos, r, zr))
        cum = log_f if p == 0 else cum + log_f
        q_s[p] = both(qd0, qd1, rows_of4(p))
        kk_s[p] = one_m_lb * jnp.where(pos, zr, r)
        cum_s[p] = cum
        v_s[p] = both(kd0, kd1, rows_of4(p))
    dec_s[...] = jnp.exp(cum)
    for p in range(cc):
        c_p = cum_s[p]
        put(qd0, qd1, rows_of(p), q_s[p] * jnp.exp(c_p))
        put(kd0, kd1, rows_of(p), kk_s[p] * jnp.exp(cum_s[cc - 1] - c_p))

    sub = HGRN_SUBTILE

    def intra(t, carry):
        rs = pl.ds(pl.multiple_of(t * sub, sub), sub)
        for p in range(cc):
            q_p = q_s[p, rs, :]
            c_p = cum_s[p, rs, :]
            xs = [_bf(q_p * kk_s[s, rs, :] * jnp.exp(c_p - cum_s[s, rs, :])) for s in range(p)]
            xs.append(_bf(q_p * kk_s[p, rs, :]))
            sums = _dot(jnp.concatenate(xs, axis=0), ones_bd)
            acc = sums[p * sub:(p + 1) * sub] * v_s[p, rs, :]
            for s in range(p):
                acc = acc + sums[s * sub:(s + 1) * sub] * v_s[s, rs, :]
            oi_s[p, rs, :] = acc
        return carry

    lax.fori_loop(0, nch // sub, intra, 0)

    gs = HGRN_SCAN_UNROLL
    blk_rows = gs * cc
    hw = LANES
    chunk_of_row = _iota2((blk_rows, hw), 0) >> 4
    chunk_masks = [jnp.where(chunk_of_row == j, 1.0, 0.0).astype(BF16) for j in range(gs)]
    same_head_pair = (_iota2((hw, hw), 0) >> 6) == (_iota2((hw, hw), 1) >> 6)
    pairs = ((i0_ref, kd0, qd0, oc0), (i1_ref, kd1, qd1, oc1))

    def expand(t):
        return jnp.concatenate([t * chunk_masks[j] for j in range(gs)], axis=1)

    def increments(g):
        rows = pl.ds(pl.multiple_of(g * blk_rows, blk_rows), blk_rows)
        return jnp.concatenate([_dot_tn(_bf(v_ref[rows, :]), expand(_bf(kd_ref[rows, :])))
                                for v_ref, kd_ref, _, _ in pairs], axis=1)

    nsteps = nch // gs
    incr_s[...] = increments(0)

    def scan(g, sts):
        incr_next = increments(jnp.minimum(g + 1, nsteps - 1))
        rows = pl.ds(pl.multiple_of(g * blk_rows, blk_rows), blk_rows)
        new_sts = []
        for pi, (_, _, qd_ref, oc_ref) in enumerate(pairs):
            st = sts[pi]
            states = []
            for j in range(gs):
                states.append(_bf(st))
                col = (pi * gs + j) * hw
                dec = dec_s[pl.ds(g * gs + j, 1), :][:, pi * hw:(pi + 1) * hw]
                st = st * dec + jnp.where(same_head_pair, incr_s[:, col:col + hw], 0.0)
            oc_ref[rows, :] = _dot_nt(expand(_bf(qd_ref[rows, :])), jnp.concatenate(states, axis=1))
            new_sts.append(st)
        incr_s[...] = incr_next
        return tuple(new_sts)

    lax.fori_loop(0, nsteps, scan, (jnp.zeros((hw, hw), F32), jnp.zeros((hw, hw), F32)))

    split(qd0, oc0)
    split(qd1, oc1)
    split(kd0, g0_ref)
    split(kd1, g1_ref)
    for p in range(cc):
        o = oi_s[p] + both(qd0, qd1, rows_of4(p))
        ms = _group_sum(o * o, ones_bd) * (1.0 / HEAD_DIM)
        put(oc0, oc1, rows_of(p), o * lax.rsqrt(ms + NORM_EPS) * ng * _silu(both(kd0, kd1, rows_of4(p))))
    o_ref[:, 0:LANES] = oc0[...]
    o_ref[:, LANES:2 * LANES] = oc1[...]


def _hgrn(h, batch, seq, lb, ng):
    base = C_HGRN // LANES
    half = lambda j: pl.BlockSpec((seq, LANES), lambda b: (b, base + j))
    full = lambda a: pl.BlockSpec(a.shape, lambda b: (0, 0))
    nch = seq // HGRN_CHUNK
    tiles = pltpu.VMEM((HGRN_CHUNK, nch, GROUP_WIDTH), F32)
    nat = pltpu.VMEM((seq, LANES), F32)
    return pl.pallas_call(
        _hgrn_kernel,
        grid=(batch,),
        in_specs=[half(j) for j in range(8)] + [full(lb), full(ng)],
        out_specs=pl.BlockSpec((seq, GROUP_WIDTH), lambda b: (b, 0)),
        out_shape=jax.ShapeDtypeStruct((batch * seq, GROUP_WIDTH), F32),
        scratch_shapes=[tiles] * 5 + [nat] * 6 + [pltpu.VMEM((nch, GROUP_WIDTH), F32),
                                                  pltpu.VMEM((LANES, 2 * HGRN_SCAN_UNROLL * LANES), F32)],
        compiler_params=_params("parallel"),
        name="hgrn",
    )(h, h, h, h, h, h, h, h, lb, ng)


def _dsa_kernel(q_ref, k_ref, v_ref, o_ref, num_ref, m_ref, l_ref):
    seq = q_ref.shape[0]
    blk = DSA_BLOCK
    pair = pl.program_id(1)
    qi = _iota2((blk, 2 * blk), 0)
    ki = _iota2((blk, 2 * blk), 1)
    steps = blk + qi - ki
    lane = _iota2((blk, LANES), 1)
    first = lane < HEAD_DIM
    qscale = HEAD_DIM ** -0.5 * LOG2E
    slopes = [jnp.where(pair == 0, ALIBI_SLOPES[hh], ALIBI_SLOPES[2 + hh]) * LOG2E for hh in range(2)]

    for bi, (window, dil) in enumerate(DSA_BRANCHES):
        nblk = seq // (dil * blk)
        in_window = (steps >= 0) & (steps <= window // dil)
        dist = (steps * dil).astype(F32)
        two_blocks = nblk > 1
        if two_blocks:
            bias = [jnp.where(in_window, -slopes[hh] * dist, MASK_VALUE) for hh in range(2)]
            bias_first = [jnp.where(ki >= blk, bias[hh], MASK_VALUE) for hh in range(2)]
        else:
            bias_first = [jnp.where(in_window, -slopes[hh] * dist, MASK_VALUE)[:, blk:] for hh in range(2)]
            bias = bias_first

        def attend_group(blocks, bi=bi, dil=dil, bias=bias, bias_first=bias_first, two_blocks=two_blocks):
            if dil == 1:
                ld = lambda ref, s0: ref[pl.ds(s0, blk), :]
                dst = lambda s0: pl.ds(bi * seq + s0, blk)
            else:
                ld = lambda ref, s0: ref[pl.ds(s0, blk, stride=dil), :]
                dst = lambda s0: pl.ds(bi * seq + s0, blk, stride=dil)
            tiles, vbs = [], []
            for start, is_first in blocks:
                qb = ld(q_ref, start) * qscale
                if two_blocks:
                    prev = start if is_first else start - dil * blk
                    kb = _bf(jnp.concatenate([ld(k_ref, prev), ld(k_ref, start)], axis=0))
                    vb = _bf(jnp.concatenate([ld(v_ref, prev), ld(v_ref, start)], axis=0))
                else:
                    kb = _bf(ld(k_ref, start))
                    vb = _bf(ld(v_ref, start))
                vbs.append(jnp.concatenate([vb, jnp.ones(vb.shape, BF16)], axis=1))
                qm = _bf(jnp.concatenate([jnp.where(first, qb, 0.0), jnp.where(first, 0.0, qb)], axis=0))
                b2 = bias_first if is_first else bias
                tiles.append(_dot_nt(qm, kb) + jnp.concatenate([b2[0], b2[1]], axis=0))
            s = jnp.concatenate(tiles, axis=0)
            m = jnp.max(s, axis=-1, keepdims=True)
            pb = _bf(jnp.exp2(s - m))
            for i, (start, _) in enumerate(blocks):
                r0, r1, r2 = 2 * i * blk, (2 * i + 1) * blk, (2 * i + 2) * blk
                pv2 = _dot(pb[r0:r2], vbs[i])
                pv = jnp.where(jnp.concatenate([first, first], axis=1), pv2[0:blk], pv2[blk:2 * blk])
                num_ref[dst(start), :] = pv[:, 0:LANES]
                l_ref[dst(start), :] = pv[:, LANES:2 * LANES]
                m_ref[dst(start), :] = jnp.where(first, m[r0:r1], m[r1:r2])

        group = DSA_GROUP
        if nblk == 1:
            def body(g, carry, attend_group=attend_group):
                attend_group([(g * group + j, True) for j in range(group)])
                return carry
            lax.fori_loop(0, dil // group, body, 0)
        elif nblk < group:
            per = group // nblk

            def body(g, carry, attend_group=attend_group, dil=dil, nblk=nblk, per=per):
                attend_group([(g * per + j + dil * blk * n, n == 0) for j in range(per) for n in range(nblk)])
                return carry
            lax.fori_loop(0, dil // per, body, 0)
        else:
            assert dil == 1 and nblk % group == 0
            attend_group([(n * blk, n == 0) for n in range(group)])

            def body(g, carry, attend_group=attend_group):
                attend_group([(pl.multiple_of((g * group + j) * blk, blk), False) for j in range(group)])
                return carry
            lax.fori_loop(1, nblk // group, body, 0)

    def merge(n, carry):
        r0 = pl.multiple_of(n * blk, blk)
        rows = [pl.ds(bi * seq + r0, blk) for bi in range(len(DSA_BRANCHES))]
        m0, m1, m2 = [m_ref[r, :] for r in rows]
        mm = jnp.maximum(jnp.maximum(m0, m1), m2)
        w0, w1, w2 = jnp.exp2(m0 - mm), jnp.exp2(m1 - mm), jnp.exp2(m2 - mm)
        num = w0 * num_ref[rows[0], :] + w1 * num_ref[rows[1], :] + w2 * num_ref[rows[2], :]
        den = w0 * l_ref[rows[0], :] + w1 * l_ref[rows[1], :] + w2 * l_ref[rows[2], :]
        o_ref[pl.ds(r0, blk), :] = num / den
        return carry

    lax.fori_loop(0, seq // blk, merge, 0)


def _dsa(h, batch, seq):
    base = C_DSA // LANES
    blk = lambda j: pl.BlockSpec((seq, LANES), lambda b, p: (b, base + 2 * j + p))
    nb = len(DSA_BRANCHES)
    return pl.pallas_call(
        _dsa_kernel,
        grid=(batch, 2),
        in_specs=[blk(0), blk(1), blk(2)],
        out_specs=pl.BlockSpec((seq, LANES), lambda b, p: (b, p)),
        out_shape=jax.ShapeDtypeStruct((batch * seq, GROUP_WIDTH), F32),
        scratch_shapes=[pltpu.VMEM((nb * seq, LANES), F32), pltpu.VMEM((nb * seq, LANES), F32),
                        pltpu.VMEM((nb * seq, LANES), F32)],
        compiler_params=_params("parallel", "parallel"),
        name="dsa",
    )(h, h, h)


FF_CHUNK = 1024
POST_ROW_TILE = 512


def _post_kernel(oa_ref, ob_ref, oc_ref, od_ref, x_ref, wo_ref, g1_ref, b1_ref, w1_ref, w2_ref, g2_ref, b2_ref,
                 y_ref):
    mixer_out = jnp.concatenate([_bf(oa_ref[...]), _bf(ob_ref[...]), _bf(oc_ref[...]), _bf(od_ref[...])], axis=1)
    mixed = _dot(mixer_out, wo_ref[...])
    x = _layer_norm_rows(DEEPNORM_ALPHA * x_ref[...] + mixed, g1_ref[...], b1_ref[...])
    xb = _bf(x)
    acc = jnp.zeros(x.shape, F32)
    for c in range(D_FF // FF_CHUNK):
        hmid = _dot(xb, w1_ref[:, c * FF_CHUNK:(c + 1) * FF_CHUNK])
        hmid = jnp.square(jnp.maximum(hmid, 0.0))
        acc = acc + _dot(_bf(hmid), w2_ref[c * FF_CHUNK:(c + 1) * FF_CHUNK, :])
    y_ref[...] = _layer_norm_rows(DEEPNORM_ALPHA * x + acc, g2_ref[...], b2_ref[...])


def _post(oa, ob, oc, od, x2d, wo, g1, b1, w1, w2, g2, b2, layer):
    n = x2d.shape[0]
    tm = POST_ROW_TILE
    grp = pl.BlockSpec((tm, GROUP_WIDTH), lambda i: (i, 0))
    row = pl.BlockSpec((tm, D_MODEL), lambda i: (i, 0))
    resident = lambda a: _layer_block(a, layer, pipeline_mode=pl.Buffered(1))
    return pl.pallas_call(
        _post_kernel,
        grid=(n // tm,),
        in_specs=[grp, grp, grp, grp, row, resident(wo), resident(g1), resident(b1),
                  resident(w1), resident(w2), resident(g2), resident(b2)],
        out_specs=row,
        out_shape=jax.ShapeDtypeStruct((n, D_MODEL), F32),
        compiler_params=_params("parallel"),
        name="post",
    )(oa, ob, oc, od, x2d, wo, g1, b1, w1, w2, g2, b2)


def _expand_heads(wrows):
    return jnp.repeat(wrows, HEAD_DIM, axis=-2)


def _arrange_w_in(w):
    wt = jnp.swapaxes(_bf(w), -1, -2)
    pts = np.cumsum(IN_SIZES)[:-1].tolist()
    (a_qkv, a_a, a_b, a_z, b_cq, b_ckv, b_kr, c_q, c_f, c_i, c_g, d_qkv) = jnp.split(wt, pts, axis=-2)
    z = lambda n: jnp.zeros(wt.shape[:-2] + (n, wt.shape[-1]), BF16)
    half = MLA_ROPE // 2
    kr_sw = jnp.concatenate([b_kr[..., half:, :], b_kr[..., :half, :]], axis=-2)
    rows = [a_qkv, _expand_heads(a_a), _expand_heads(a_b), a_z,
            b_cq, z(256 - MLA_Q_RANK), b_ckv,
            z(MLA_NOPE), b_kr, z(LANES - MLA_NOPE - MLA_ROPE),
            z(MLA_NOPE), kr_sw, z(LANES - MLA_NOPE - MLA_ROPE), z(C_HGRN - C_MLA_KRS - LANES),
            c_q, c_f, c_i, c_g, d_qkv]
    return jnp.concatenate(rows, axis=-2)


def _arrange_mla_weights(w_uq, w_ukv):
    lead = w_uq.shape[:-2]
    rq = w_uq.shape[-2]
    wq = _bf(w_uq).reshape(lead + (rq, N_HEADS, MLA_NOPE + MLA_ROPE))
    half = MLA_ROPE // 2
    zq = jnp.zeros(lead + (rq, N_HEADS, LANES - MLA_NOPE - MLA_ROPE), BF16)
    q_main = jnp.concatenate([wq, zq], axis=-1)
    rope = wq[..., MLA_NOPE:]
    rope_sw = jnp.concatenate([rope[..., half:], rope[..., :half]], axis=-1)
    q_swap = jnp.concatenate([jnp.zeros(lead + (rq, N_HEADS, MLA_NOPE), BF16), rope_sw, zq], axis=-1)
    pad_rows = lambda m: jnp.concatenate([m, jnp.zeros(lead + (256 - rq, m.shape[-1]), BF16)], axis=-2)
    q_main = pad_rows(q_main.reshape(lead + (rq, N_HEADS * LANES)))
    q_swap = pad_rows(q_swap.reshape(lead + (rq, N_HEADS * LANES)))
    rkv = w_ukv.shape[-2]
    wkv = _bf(w_ukv).reshape(lead + (rkv, N_HEADS, MLA_NOPE + HEAD_DIM))
    k_w = jnp.concatenate([wkv[..., :MLA_NOPE], jnp.zeros(lead + (rkv, N_HEADS, LANES - MLA_NOPE), BF16)], axis=-1)
    v_w = wkv[..., MLA_NOPE:]
    return (q_main, q_swap, k_w.reshape(lead + (rkv, N_HEADS * LANES)),
            v_w.reshape(lead + (rkv, N_HEADS * HEAD_DIM)))


def _rope_tables(seq):
    half = MLA_ROPE // 2
    pos = jnp.arange(seq, dtype=F32)
    inv_freq = ROPE_BASE ** (-jnp.arange(half, dtype=F32) / half)
    ang = pos[:, None] * inv_freq[None, :]
    cos, sin = jnp.cos(ang), jnp.sin(ang)
    ones = jnp.ones((seq, MLA_NOPE), F32)
    zeros = jnp.zeros((seq, MLA_NOPE), F32)
    tail1 = jnp.ones((seq, LANES - MLA_NOPE - MLA_ROPE), F32)
    tail0 = jnp.zeros((seq, LANES - MLA_NOPE - MLA_ROPE), F32)
    cos_t = jnp.concatenate([ones, cos, cos, tail1], axis=1)
    sin_t = jnp.concatenate([zeros, -sin, sin, tail0], axis=1)
    return cos_t, sin_t


def _tile_heads(vec):
    return jnp.tile(vec.astype(F32), N_HEADS)[..., None, :]


def kernel(x, w_in, gdn_conv_w, gdn_a_log, gdn_dt_bias, gdn_norm_g, mla_q_norm_g, mla_kv_norm_g,
           mla_w_uq, mla_w_ukv, hgrn_lb_logits, hgrn_norm_g, w_out, ln1_g, ln1_b, w_ff1, w_ff2,
           ln2_g, ln2_b):
    batch, seq, d_model = x.shape
    assert d_model == D_MODEL and seq % (16 * DSA_BLOCK) == 0 and (batch * seq) % ROW_TILE == 0
    x2d = x.reshape(batch * seq, d_model)
    p_lb = jax.nn.softmax(hgrn_lb_logits.astype(F32), axis=0)
    lower_bounds = (jnp.cumsum(p_lb, axis=0) - p_lb[:1])[:, None, :]
    cos_t, sin_t = _rope_tables(seq)
    w_in_a = _arrange_w_in(w_in)
    avec = jnp.repeat(-jnp.exp(gdn_a_log.astype(F32)), HEAD_DIM, axis=-1)[:, None, :]
    dtb = jnp.repeat(gdn_dt_bias.astype(F32), HEAD_DIM, axis=-1)[:, None, :]
    conv_w = gdn_conv_w.astype(F32)
    gdn_g = _tile_heads(gdn_norm_g)
    hgrn_g = _tile_heads(hgrn_norm_g)
    wq, wqs, wk, wv = _arrange_mla_weights(mla_w_uq, mla_w_ukv)
    qg = jnp.concatenate([mla_q_norm_g.astype(F32), jnp.zeros((DEPTH, 256 - MLA_Q_RANK), F32)], axis=-1)[:, None, :]
    kvg = mla_kv_norm_g.astype(F32)[:, None, :]
    wo, w1, w2 = _bf(w_out), _bf(w_ff1), _bf(w_ff2)
    g1, b1, g2, b2 = (t.astype(F32)[:, None, :] for t in (ln1_g, ln1_b, ln2_g, ln2_b))
    for l in range(DEPTH):
        h = _inproj(x2d, w_in_a, l)
        o_a = _gdn(h, batch, seq, conv_w[l], avec[l], dtb[l], gdn_g[l])
        o_b = _mla(h, batch, seq, cos_t, sin_t, qg[l], kvg[l], wq[l], wqs[l], wk[l], wv[l])
        o_c = _hgrn(h, batch, seq, lower_bounds[l], hgrn_g[l])
        o_d = _dsa(h, batch, seq)
        x2d = _post(o_a, o_b, o_c, o_d, x2d, wo, g1, b1, w1, w2, g2, b2, l)
    return x2d.reshape(batch, seq, d_model)
```

```python
import functools
import math

import numpy as np
import jax
import jax.numpy as jnp
from jax import lax
from jax.experimental import pallas as pl
from jax.experimental.pallas import tpu as pltpu

F32 = jnp.float32
BF16 = jnp.bfloat16

D_MODEL = 1024
DEPTH = 2
GROUP_WIDTH = 256
N_HEADS = 4
HEAD_DIM = 64
GDN_CONV = 4
GDN_CHUNK = 64
MLA_NOPE = 64
MLA_ROPE = 32
MLA_Q_RANK = 192
MLA_KV_RANK = 128
ROPE_BASE = 10000.0
HGRN_CHUNK = 16
DSA_BRANCHES = ((128, 1), (512, 4), (2048, 16))
DSA_BLOCK = 128
DSA_GROUP = 8
ALIBI_SLOPES = tuple(2.0 ** (-8.0 * (j + 1) / N_HEADS) for j in range(N_HEADS))
D_FF = 4 * D_MODEL
DEEPNORM_ALPHA = (2 * DEPTH) ** 0.25
NORM_EPS = 1e-6
MASK_VALUE = -1e30
LOG2E = 1.4426950408889634

LANES = 128
SUBLANES = 8
VMEM_LIMIT_BYTES = 58 * 1024 * 1024

C_GDN_QKV = 0
C_GDN_A = 768
C_GDN_B = 1024
C_GDN_Z = 1280
C_MLA_CQ = 1536
C_MLA_CKV = 1792
C_MLA_KR = 1920
C_MLA_KRS = 2048
C_HGRN = 2304
C_DSA = 3328
C_TOTAL = 4096
IN_SIZES = (768, 4, 4, 256, 192, 128, 32, 256, 256, 256, 256, 768)
ROW_TILE = 512


def _bf(x):
    return x.astype(BF16)


def _dot(a, b):
    return jnp.dot(a, b, preferred_element_type=F32)


def _dot_nt(a, b):
    return lax.dot_general(a, b, (((1,), (1,)), ((), ())), preferred_element_type=F32)


def _dot_tn(a, b):
    return lax.dot_general(a, b, (((0,), (0,)), ((), ())), preferred_element_type=F32)


def _split3(x):
    hi = _bf(x)
    r1 = x - hi.astype(F32)
    mid = _bf(r1)
    lo = _bf(r1 - mid.astype(F32))
    return hi, mid, lo


def _dot_sel_r(x, sel):
    hi, mid, lo = _split3(x)
    return _dot(hi, sel) + _dot(mid, sel) + _dot(lo, sel)


def _dot_sel_l(sel, x):
    n = x.shape[1]
    r = _dot(sel, jnp.concatenate(_split3(x), axis=1))
    return r[:, 0:n] + r[:, n:2 * n] + r[:, 2 * n:3 * n]


def _sigmoid(x):
    return 0.5 + 0.5 * jnp.tanh(0.5 * x)


def _silu(x):
    return x * _sigmoid(x)


def _softplus(x):
    return jnp.maximum(x, 0.0) - jnp.log(_sigmoid(jnp.abs(x)))


def _iota2(shape, axis):
    return lax.broadcasted_iota(jnp.int32, shape, axis)


def _head_block_ones(n):
    r = _iota2((n, n), 0) >> 6
    c = _iota2((n, n), 1) >> 6
    return jnp.where(r == c, 1.0, 0.0).astype(BF16)


def _group_sum(x, ones_bd):
    return _dot(_bf(x), ones_bd)


def _layer_norm_rows(y, g, b):
    mu = jnp.mean(y, axis=-1, keepdims=True)
    d = y - mu
    var = jnp.mean(d * d, axis=-1, keepdims=True)
    return d * lax.rsqrt(var + NORM_EPS) * g + b


def _params(*sem):
    return pltpu.CompilerParams(dimension_semantics=sem, vmem_limit_bytes=VMEM_LIMIT_BYTES)


def _inproj_kernel(x_ref, w_ref, o_ref):
    xb = _bf(x_ref[...])
    for c in range(C_TOTAL // 512):
        o_ref[:, c * 512:(c + 1) * 512] = _dot_nt(xb, w_ref[c * 512:(c + 1) * 512, :])


def _layer_block(stacked, layer, **kw):
    return pl.BlockSpec((None,) + stacked.shape[1:], lambda *_: (layer, 0, 0), **kw)


def _inproj(x2d, w_layers, layer):
    n = x2d.shape[0]
    return pl.pallas_call(
        _inproj_kernel,
        grid=(n // ROW_TILE,),
        in_specs=[pl.BlockSpec((ROW_TILE, D_MODEL), lambda i: (i, 0)), _layer_block(w_layers, layer)],
        out_specs=pl.BlockSpec((ROW_TILE, C_TOTAL), lambda i: (i, 0)),
        out_shape=jax.ShapeDtypeStruct((n, C_TOTAL), F32),
        compiler_params=_params("parallel"),
        name="inproj",
    )(x2d, w_layers)


GDN_GROUP = 8


def _gdn_kernel(qkv_ref, a_ref, b_ref, z_ref, convw_ref, avec_ref, dtb_ref, ng_ref, o_ref,
                s_ref, u_ref, egl_ref, ku_ref, w_ref, qd_ref, qk_ref, kw_ref):
    seq = qkv_ref.shape[0]
    c = GDN_CHUNK
    w = GROUP_WIDTH
    rows_per_step = GDN_GROUP * c
    s_ref[...] = jnp.zeros_like(s_ref)

    same_head = (_iota2((w, w), 0) >> 6) == (_iota2((w, w), 1) >> 6)
    ones_bd = jnp.where(same_head, 1.0, 0.0).astype(BF16)
    rr = _iota2((rows_per_step, rows_per_step), 0)
    rc = _iota2((rows_per_step, rows_per_step), 1)
    same_chunk = (rr >> 6) == (rc >> 6)
    tril_chunks = jnp.where(same_chunk & (rc <= rr), 1.0, 0.0).astype(BF16)
    ones_chunks = jnp.where(same_chunk, 1.0, 0.0).astype(BF16)
    li = _iota2((rows_per_step, w), 0) & (c - 1)
    lj = _iota2((rows_per_step, w), 1) & (c - 1)
    causal = lj <= li
    strict = lj < li
    eye = lj == li
    lane_head = _iota2((c, w), 1) >> 6
    first_rows = _iota2((SUBLANES, 3 * w), 0)

    def stack_heads(t):
        return jnp.concatenate([jnp.where(lane_head == h, t, jnp.zeros_like(t)) for h in range(N_HEADS)], axis=0)

    def block_diag(t):
        return jnp.where(same_head, jnp.concatenate([t] * N_HEADS, axis=0), jnp.zeros((w, w), t.dtype))

    convw = convw_ref[...]
    avec = avec_ref[...]
    dtb = dtb_ref[...]
    ng = ng_ref[...]
    chunks = [slice(i * c, (i + 1) * c) for i in range(GDN_GROUP)]

    def front(n):
        r0 = pl.multiple_of(n * rows_per_step, rows_per_step)
        rows = pl.ds(r0, rows_per_step)
        cur = qkv_ref[rows, :]
        prev = qkv_ref[pl.ds(pl.multiple_of(jnp.maximum(r0 - SUBLANES, 0), SUBLANES), SUBLANES), :]
        prev = jnp.where(n > 0, prev, 0.0)
        y = cur * convw[GDN_CONV - 1:GDN_CONV, :]
        for j in range(1, GDN_CONV):
            shifted = pltpu.roll(cur, j, 0)
            head = jnp.where(first_rows < j, pltpu.roll(prev, j, 0), shifted[0:SUBLANES, :])
            shifted = jnp.concatenate([head, shifted[SUBLANES:, :]], axis=0)
            y = y + shifted * convw[GDN_CONV - 1 - j:GDN_CONV - j, :]
        y = _silu(y)
        q = y[:, 0:w]
        k = y[:, w:2 * w]
        v = y[:, 2 * w:3 * w]
        q = q * lax.rsqrt(_group_sum(q * q, ones_bd) + NORM_EPS) * (HEAD_DIM ** -0.5)
        k = k * lax.rsqrt(_group_sum(k * k, ones_bd) + NORM_EPS)
        beta = _sigmoid(b_ref[rows, :])
        gstep = avec * _softplus(a_ref[rows, :] + dtb)
        g = _dot_sel_l(tril_chunks, gstep)
        g_last = jnp.concatenate([jnp.broadcast_to(g[sl][c - 1:c, :], (c, w)) for sl in chunks], axis=0)
        gr = _dot_sel_l(ones_chunks, jnp.where(eye, g, 0.0))
        decay = jnp.where(causal, jnp.exp(jnp.where(causal, g - gr, 0.0)), 0.0)
        eg = jnp.exp(g)
        kb = k * beta
        kbb = _bf(kb)
        qb = _bf(q)
        prods = [_dot_nt(jnp.concatenate([kbb[sl], qb[sl]], axis=0), _bf(stack_heads(k[sl]))) for sl in chunks]
        lower = jnp.where(strict, jnp.concatenate([p[0:c] for p in prods], axis=0) * decay, 0.0)
        qk = jnp.concatenate([p[c:2 * c] for p in prods], axis=0) * decay
        qd_ref[rows, :] = _bf(q * eg)
        qk_ref[rows, :] = _bf(qk)
        egl_ref[rows, :] = jnp.exp(g_last)
        return lower, v * beta, kb * eg, _bf(k * jnp.exp(g_last - g))

    def back(n, lower, vb, kbg, k_dec):
        rows = pl.ds(pl.multiple_of(n * rows_per_step, rows_per_step), rows_per_step)
        m = -lower
        t = jnp.where(eye, 1.0, 0.0) + m
        for level in range(6):
            mb = _bf(m)
            tb = _bf(t)
            new_m, t_m = [], []
            for sl in chunks:
                m_bd = block_diag(mb[sl])
                if level == 0:
                    new_m.append(_dot(mb[sl], m_bd))
                elif level < 5:
                    both = _dot(jnp.concatenate([mb[sl], tb[sl]], axis=0), m_bd)
                    new_m.append(both[0:c])
                    t_m.append(both[c:2 * c])
                else:
                    t_m.append(_dot(tb[sl], m_bd))
            if level < 5:
                m = jnp.concatenate(new_m, axis=0)
            if level > 0:
                t = t + jnp.concatenate(t_m, axis=0)
        tb = _bf(t)
        uw = [_dot(tb[sl], _bf(jnp.concatenate([stack_heads(vb[sl]), stack_heads(kbg[sl])], axis=1)))
              for sl in chunks]
        u_ref[rows, :] = jnp.concatenate([x[:, 0:w] for x in uw], axis=0)
        w_ref[rows, :] = _bf(jnp.concatenate([x[:, w:2 * w] for x in uw], axis=0))
        for ci, sl in enumerate(chunks):
            kw_ku = _dot_tn(k_dec[sl], _bf(uw[ci]))
            mat_rows = pl.ds(pl.multiple_of((n * GDN_GROUP + ci) * w, w), w)
            ku_ref[mat_rows, :] = kw_ku[:, 0:w]
            kw_ref[mat_rows, :] = _bf(kw_ku[:, w:2 * w])

    def prep(n, carry):
        back(n, *front(n))
        return carry

    lax.fori_loop(0, seq // rows_per_step, prep, 0)

    def scan(n, carry):
        step0 = pl.multiple_of(n * rows_per_step, rows_per_step)
        for ci in range(GDN_GROUP):
            rows = pl.ds(step0 + ci * c, c)
            mat_rows = pl.ds(pl.multiple_of((n * GDN_GROUP + ci) * w, w), w)
            s_bd = s_ref[...]
            lhs = jnp.concatenate([kw_ref[mat_rows, :], w_ref[rows, :], qd_ref[rows, :]], axis=0)
            r = _dot(lhs, _bf(s_bd))
            s_ref[...] = s_bd * egl_ref[pl.ds(step0 + ci * c, 1), :] + jnp.where(
                same_head, ku_ref[mat_rows, :] - r[0:w], 0.0)
            v_new = u_ref[rows, :] - r[w:w + c]
            o_ref[rows, :] = r[w + c:w + 2 * c] + _dot(qk_ref[rows, :], _bf(stack_heads(v_new)))
        return carry

    lax.fori_loop(0, seq // rows_per_step, scan, 0)

    def readout(n, carry):
        rows = pl.ds(pl.multiple_of(n * rows_per_step, rows_per_step), rows_per_step)
        o = o_ref[rows, :]
        ms = _group_sum(o * o, ones_bd) * (1.0 / HEAD_DIM)
        o_ref[rows, :] = o * lax.rsqrt(ms + NORM_EPS) * ng * _silu(z_ref[rows, :])
        return carry

    lax.fori_loop(0, seq // rows_per_step, readout, 0)


def _gdn(h, batch, seq, convw, avec, dtb, ng):
    blk = lambda width, cblk: pl.BlockSpec((seq, width), lambda b: (b, cblk))
    full = lambda a: pl.BlockSpec(a.shape, lambda b: (0, 0))
    return pl.pallas_call(
        _gdn_kernel,
        grid=(batch,),
        in_specs=[blk(768, C_GDN_QKV // 768), blk(256, C_GDN_A // 256), blk(256, C_GDN_B // 256),
                  blk(256, C_GDN_Z // 256), full(convw), full(avec), full(dtb), full(ng)],
        out_specs=pl.BlockSpec((seq, GROUP_WIDTH), lambda b: (b, 0)),
        out_shape=jax.ShapeDtypeStruct((batch * seq, GROUP_WIDTH), F32),
        scratch_shapes=[pltpu.VMEM((GROUP_WIDTH, GROUP_WIDTH), F32),
                        pltpu.VMEM((seq, GROUP_WIDTH), F32), pltpu.VMEM((seq, GROUP_WIDTH), F32),
                        pltpu.VMEM((seq // GDN_CHUNK * GROUP_WIDTH, GROUP_WIDTH), F32),
                        pltpu.VMEM((seq, GROUP_WIDTH), BF16), pltpu.VMEM((seq, GROUP_WIDTH), BF16),
                        pltpu.VMEM((seq, GROUP_WIDTH), BF16),
                        pltpu.VMEM((seq // GDN_CHUNK * GROUP_WIDTH, GROUP_WIDTH), BF16)],
        compiler_params=_params("parallel"),
        name="gdn",
    )(h, h, h, h, convw, avec, dtb, ng)


MLA_Q_BLOCK = 512


def _mla_kernel(cq_ref, ckv_ref, kr_ref, krs_ref, cos_ref, sin_ref, qg_ref, kvg_ref,
                wq_ref, wqs_ref, wk_ref, wv_ref, o_ref, q_ref, k_ref, v_ref):
    seq = cq_ref.shape[0]
    tm = ROW_TILE
    scale = (MLA_NOPE + MLA_ROPE) ** -0.5 * LOG2E

    def prep(i, carry):
        rows = pl.ds(pl.multiple_of(i * tm, tm), tm)
        cq = cq_ref[rows, :]
        nq = cq * lax.rsqrt(jnp.sum(cq * cq, axis=-1, keepdims=True) * (1.0 / MLA_Q_RANK) + NORM_EPS) * qg_ref[...]
        nqb = _bf(nq)
        cos1 = cos_ref[rows, :]
        sin1 = sin_ref[rows, :]
        cos4 = jnp.concatenate([cos1] * N_HEADS, axis=1)
        sin4 = jnp.concatenate([sin1] * N_HEADS, axis=1)
        q_ref[rows, :] = _bf((_dot(nqb, wq_ref[...]) * cos4 + _dot(nqb, wqs_ref[...]) * sin4) * scale)
        ckv = ckv_ref[rows, :]
        nkv = ckv * lax.rsqrt(jnp.mean(ckv * ckv, axis=-1, keepdims=True) + NORM_EPS) * kvg_ref[...]
        nkvb = _bf(nkv)
        kr = kr_ref[rows, :] * cos1 + krs_ref[rows, :] * sin1
        k_ref[rows, :] = _bf(_dot(nkvb, wk_ref[...]) + jnp.concatenate([kr] * N_HEADS, axis=1))
        v = _bf(_dot(nkvb, wv_ref[...]))
        ones = jnp.ones((tm, LANES), BF16)
        v_ref[rows, :] = jnp.concatenate([v[:, 0:LANES], ones, v[:, LANES:2 * LANES], ones], axis=1)
        return carry

    lax.fori_loop(0, seq // tm, prep, 0)

    tq = MLA_Q_BLOCK
    ri = _iota2((tq, tq), 0)
    ci = _iota2((tq, tq), 1)
    diag_ok = ci <= ri
    lane = _iota2((tq, LANES), 1)
    for pair in range(2):
        vc = slice(pair * 2 * LANES, (pair + 1) * 2 * LANES)
        for qi in range(seq // tq):
            q0 = qi * tq
            v_off = v_ref[0:q0, vc] if qi else None
            v_diag = v_ref[q0:q0 + tq, vc]
            outs = []
            for hh in range(2):
                hc = slice((2 * pair + hh) * LANES, (2 * pair + hh + 1) * LANES)
                qh = q_ref[q0:q0 + tq, hc]
                s_diag = _dot_nt(qh, k_ref[q0:q0 + tq, hc])
                s_diag = jnp.where(diag_ok, s_diag, MASK_VALUE)
                m = jnp.max(s_diag, axis=-1, keepdims=True)
                if qi:
                    s_off = _dot_nt(qh, k_ref[0:q0, hc])
                    m = jnp.maximum(m, jnp.max(s_off, axis=-1, keepdims=True))
                acc = _dot(_bf(jnp.exp2(s_diag - m)), v_diag)
                if qi:
                    acc = acc + _dot(_bf(jnp.exp2(s_off - m)), v_off)
                outs.append(acc[:, 0:LANES] / acc[:, LANES:2 * LANES])
            o_ref[q0:q0 + tq, pair * LANES:(pair + 1) * LANES] = jnp.where(lane < HEAD_DIM, outs[0], outs[1])


def _mla(h, batch, seq, cos_t, sin_t, qg, kvg, wq, wqs, wk, wv):
    col = lambda width, off: pl.BlockSpec((seq, width), lambda b: (b, off // width))
    table = pl.BlockSpec((seq, LANES), lambda b: (0, 0))
    full = lambda a: pl.BlockSpec(a.shape, lambda b: (0, 0))
    return pl.pallas_call(
        _mla_kernel,
        grid=(batch,),
        in_specs=[col(256, C_MLA_CQ), col(LANES, C_MLA_CKV), col(LANES, C_MLA_KR), col(LANES, C_MLA_KRS),
                  table, table, full(qg), full(kvg), full(wq), full(wqs), full(wk), full(wv)],
        out_specs=pl.BlockSpec((seq, GROUP_WIDTH), lambda b: (b, 0)),
        out_shape=jax.ShapeDtypeStruct((batch * seq, GROUP_WIDTH), F32),
        scratch_shapes=[pltpu.VMEM((seq, N_HEADS * LANES), BF16)] * 3,
        compiler_params=_params("parallel"),
        name="mla",
    )(h, h, h, h, cos_t, sin_t, qg, kvg, wq, wqs, wk, wv)


HGRN_SCAN_UNROLL = 8
HGRN_SUBTILE = 32


def _hgrn_kernel(q0_ref, q1_ref, f0_ref, f1_ref, i0_ref, i1_ref, g0_ref, g1_ref, lb_ref, ng_ref, o_ref,
                 q_s, kk_s, cum_s, v_s, oi_s, qd0, qd1, kd0, kd1, oc0, oc1, dec_s, incr_s):
    seq = q0_ref.shape[0]
    cc = HGRN_CHUNK
    nch = seq // cc
    w = GROUP_WIDTH
    same_head = (_iota2((w, w), 0) >> 6) == (_iota2((w, w), 1) >> 6)
    ones_bd = jnp.where(same_head, 1.0, 0.0).astype(BF16)
    lb = lb_ref[...]
    ng = ng_ref[...]
    one_m_lb = 1.0 - lb

    def rows_of(p):
        return pl.ds(p, nch, stride=cc)

    def both(ref0, ref1, rows):
        return jnp.concatenate([ref0[rows, :], ref1[rows, :]], axis=1)

    def put(ref0, ref1, rows, val):
        ref0[rows, :] = val[:, 0:LANES]
        ref1[rows, :] = val[:, LANES:2 * LANES]

    quarter = seq // 4

    def split(dst, src):
        for r in range(4):
            dst[r * quarter:(r + 1) * quarter, :] = src[pl.ds(r, quarter, stride=4), :]

    def rows_of4(p):
        return pl.ds((p % 4) * quarter + p // 4, nch, stride=4)

    split(oc0, f0_ref)
    split(oc1, f1_ref)
    split(qd0, q0_ref)
    split(qd1, q1_ref)
    split(kd0, i0_ref)
    split(kd1, i1_ref)

    cum = None
    for p in range(cc):
        fl = both(oc0, oc1, rows_of4(p))
        z = jnp.exp(-jnp.abs(fl))
        r = 1.0 / (1.0 + z)
        zr = z * r
        pos = fl >= 0.0
        log_f = jnp.log(lb + one_m_lb * jnp.where(pos, r, zr))
        cum = log_f if p == 0 else cum + log_f
        q_s[p] = both(qd0, qd1, rows_of4(p))
        kk_s[p] = one_m_lb * jnp.where(pos, zr, r)
        cum_s[p] = cum
        v_s[p] = both(kd0, kd1, rows_of4(p))
    dec_s[...] = jnp.exp(cum)
    for p in range(cc):
        c_p = cum_s[p]
        put(qd0, qd1, rows_of(p), q_s[p] * jnp.exp(c_p))
        put(kd0, kd1, rows_of(p), kk_s[p] * jnp.exp(cum_s[cc - 1] - c_p))

    sub = HGRN_SUBTILE

    def intra(t, carry):
        rs = pl.ds(pl.multiple_of(t * sub, sub), sub)
        for p in range(cc):
            q_p = q_s[p, rs, :]
            c_p = cum_s[p, rs, :]
            xs = [_bf(q_p * kk_s[s, rs, :] * jnp.exp(c_p - cum_s[s, rs, :])) for s in range(p)]
            xs.append(_bf(q_p * kk_s[p, rs, :]))
            sums = _dot(jnp.concatenate(xs, axis=0), ones_bd)
            acc = sums[p * sub:(p + 1) * sub] * v_s[p, rs, :]
            for s in range(p):
                acc = acc + sums[s * sub:(s + 1) * sub] * v_s[s, rs, :]
            oi_s[p, rs, :] = acc
        return carry

    lax.fori_loop(0, nch // sub, intra, 0)

    gs = HGRN_SCAN_UNROLL
    blk_rows = gs * cc
    hw = LANES
    chunk_of_row = _iota2((blk_rows, hw), 0) >> 4
    chunk_masks = [jnp.where(chunk_of_row == j, 1.0, 0.0).astype(BF16) for j in range(gs)]
    same_head_pair = (_iota2((hw, hw), 0) >> 6) == (_iota2((hw, hw), 1) >> 6)
    pairs = ((i0_ref, kd0, qd0, oc0), (i1_ref, kd1, qd1, oc1))

    def expand(t):
        return jnp.concatenate([t * chunk_masks[j] for j in range(gs)], axis=1)

    def increments(g):
        rows = pl.ds(pl.multiple_of(g * blk_rows, blk_rows), blk_rows)
        return jnp.concatenate([_dot_tn(_bf(v_ref[rows, :]), expand(_bf(kd_ref[rows, :])))
                                for v_ref, kd_ref, _, _ in pairs], axis=1)

    nsteps = nch // gs
    incr_s[...] = increments(0)

    def scan(g, sts):
        incr_next = increments(jnp.minimum(g + 1, nsteps - 1))
        rows = pl.ds(pl.multiple_of(g * blk_rows, blk_rows), blk_rows)
        new_sts = []
        for pi, (_, _, qd_ref, oc_ref) in enumerate(pairs):
            st = sts[pi]
            states = []
            for j in range(gs):
                states.append(_bf(st))
                col = (pi * gs + j) * hw
                dec = dec_s[pl.ds(g * gs + j, 1), :][:, pi * hw:(pi + 1) * hw]
                st = st * dec + jnp.where(same_head_pair, incr_s[:, col:col + hw], 0.0)
            oc_ref[rows, :] = _dot_nt(expand(_bf(qd_ref[rows, :])), jnp.concatenate(states, axis=1))
            new_sts.append(st)
        incr_s[...] = incr_next
        return tuple(new_sts)

    lax.fori_loop(0, nsteps, scan, (jnp.zeros((hw, hw), F32), jnp.zeros((hw, hw), F32)))

    split(qd0, oc0)
    split(qd1, oc1)
    split(kd0, g0_ref)
    split(kd1, g1_ref)
    for p in range(cc):
        o = oi_s[p] + both(qd0, qd1, rows_of4(p))
        ms = _group_sum(o * o, ones_bd) * (1.0 / HEAD_DIM)
        put(oc0, oc1, rows_of(p), o * lax.rsqrt(ms + NORM_EPS) * ng * _silu(both(kd0, kd1, rows_of4(p))))
    o_ref[:, 0:LANES] = oc0[...]
    o_ref[:, LANES:2 * LANES] = oc1[...]


def _hgrn(h, batch, seq, lb, ng):
    base = C_HGRN // LANES
    half = lambda j: pl.BlockSpec((seq, LANES), lambda b: (b, base + j))
    full = lambda a: pl.BlockSpec(a.shape, lambda b: (0, 0))
    nch = seq // HGRN_CHUNK
    tiles = pltpu.VMEM((HGRN_CHUNK, nch, GROUP_WIDTH), F32)
    nat = pltpu.VMEM((seq, LANES), F32)
    return pl.pallas_call(
        _hgrn_kernel,
        grid=(batch,),
        in_specs=[half(j) for j in range(8)] + [full(lb), full(ng)],
        out_specs=pl.BlockSpec((seq, GROUP_WIDTH), lambda b: (b, 0)),
        out_shape=jax.ShapeDtypeStruct((batch * seq, GROUP_WIDTH), F32),
        scratch_shapes=[tiles] * 5 + [nat] * 6 + [pltpu.VMEM((nch, GROUP_WIDTH), F32),
                                                  pltpu.VMEM((LANES, 2 * HGRN_SCAN_UNROLL * LANES), F32)],
        compiler_params=_params("parallel"),
        name="hgrn",
    )(h, h, h, h, h, h, h, h, lb, ng)


def _dsa_kernel(q_ref, k_ref, v_ref, o_ref, num_ref, m_ref, l_ref):
    seq = q_ref.shape[0]
    blk = DSA_BLOCK
    pair = pl.program_id(1)
    qi = _iota2((blk, 2 * blk), 0)
    ki = _iota2((blk, 2 * blk), 1)
    steps = blk + qi - ki
    lane = _iota2((blk, LANES), 1)
    first = lane < HEAD_DIM
    qscale = HEAD_DIM ** -0.5 * LOG2E
    slopes = [jnp.where(pair == 0, ALIBI_SLOPES[hh], ALIBI_SLOPES[2 + hh]) * LOG2E for hh in range(2)]

    for bi, (window, dil) in enumerate(DSA_BRANCHES):
        nblk = seq // (dil * blk)
        in_window = (steps >= 0) & (steps <= window // dil)
        dist = (steps * dil).astype(F32)
        two_blocks = nblk > 1
        if two_blocks:
            bias = [jnp.where(in_window, -slopes[hh] * dist, MASK_VALUE) for hh in range(2)]
            bias_first = [jnp.where(ki >= blk, bias[hh], MASK_VALUE) for hh in range(2)]
        else:
            bias_first = [jnp.where(in_window, -slopes[hh] * dist, MASK_VALUE)[:, blk:] for hh in range(2)]
            bias = bias_first

        def attend_group(blocks, bi=bi, dil=dil, bias=bias, bias_first=bias_first, two_blocks=two_blocks):
            if dil == 1:
                ld = lambda ref, s0: ref[pl.ds(s0, blk), :]
                dst = lambda s0: pl.ds(bi * seq + s0, blk)
            else:
                ld = lambda ref, s0: ref[pl.ds(s0, blk, stride=dil), :]
                dst = lambda s0: pl.ds(bi * seq + s0, blk, stride=dil)
            tiles, vbs = [], []
            for start, is_first in blocks:
                qb = ld(q_ref, start) * qscale
                if two_blocks:
                    prev = start if is_first else start - dil * blk
                    kb = _bf(jnp.concatenate([ld(k_ref, prev), ld(k_ref, start)], axis=0))
                    vb = _bf(jnp.concatenate([ld(v_ref, prev), ld(v_ref, start)], axis=0))
                else:
                    kb = _bf(ld(k_ref, start))
                    vb = _bf(ld(v_ref, start))
                vbs.append(jnp.concatenate([vb, jnp.ones(vb.shape, BF16)], axis=1))
                qm = _bf(jnp.concatenate([jnp.where(first, qb, 0.0), jnp.where(first, 0.0, qb)], axis=0))
                b2 = bias_first if is_first else bias
                tiles.append(_dot_nt(qm, kb) + jnp.concatenate([b2[0], b2[1]], axis=0))
            s = jnp.concatenate(tiles, axis=0)
            m = jnp.max(s, axis=-1, keepdims=True)
            pb = _bf(jnp.exp2(s - m))
            for i, (start, _) in enumerate(blocks):
                r0, r1, r2 = 2 * i * blk, (2 * i + 1) * blk, (2 * i + 2) * blk
                pv2 = _dot(pb[r0:r2], vbs[i])
                pv = jnp.where(jnp.concatenate([first, first], axis=1), pv2[0:blk], pv2[blk:2 * blk])
                num_ref[dst(start), :] = pv[:, 0:LANES]
                l_ref[dst(start), :] = pv[:, LANES:2 * LANES]
                m_ref[dst(start), :] = jnp.where(first, m[r0:r1], m[r1:r2])

        group = DSA_GROUP
        if nblk == 1:
            def body(g, carry, attend_group=attend_group):
                attend_group([(g * group + j, True) for j in range(group)])
                return carry
            lax.fori_loop(0, dil // group, body, 0)
        elif nblk < group:
            per = group // nblk

            def body(g, carry, attend_group=attend_group, dil=dil, nblk=nblk, per=per):
                attend_group([(g * per + j + dil * blk * n, n == 0) for j in range(per) for n in range(nblk)])
                return carry
            lax.fori_loop(0, dil // per, body, 0)
        else:
            assert dil == 1 and nblk % group == 0
            attend_group([(n * blk, n == 0) for n in range(group)])

            def body(g, carry, attend_group=attend_group):
                attend_group([(pl.multiple_of((g * group + j) * blk, blk), False) for j in range(group)])
                return carry
            lax.fori_loop(1, nblk // group, body, 0)

    def merge(n, carry):
        r0 = pl.multiple_of(n * blk, blk)
        rows = [pl.ds(bi * seq + r0, blk) for bi in range(len(DSA_BRANCHES))]
        m0, m1, m2 = [m_ref[r, :] for r in rows]
        mm = jnp.maximum(jnp.maximum(m0, m1), m2)
        w0, w1, w2 = jnp.exp2(m0 - mm), jnp.exp2(m1 - mm), jnp.exp2(m2 - mm)
        num = w0 * num_ref[rows[0], :] + w1 * num_ref[rows[1], :] + w2 * num_ref[rows[2], :]
        den = w0 * l_ref[rows[0], :] + w1 * l_ref[rows[1], :] + w2 * l_ref[rows[2], :]
        o_ref[pl.ds(r0, blk), :] = num / den
        return carry

    lax.fori_loop(0, seq // blk, merge, 0)


def _dsa(h, batch, seq):
    base = C_DSA // LANES
    blk = lambda j: pl.BlockSpec((seq, LANES), lambda b, p: (b, base + 2 * j + p))
    nb = len(DSA_BRANCHES)
    return pl.pallas_call(
        _dsa_kernel,
        grid=(batch, 2),
        in_specs=[blk(0), blk(1), blk(2)],
        out_specs=pl.BlockSpec((seq, LANES), lambda b, p: (b, p)),
        out_shape=jax.ShapeDtypeStruct((batch * seq, GROUP_WIDTH), F32),
        scratch_shapes=[pltpu.VMEM((nb * seq, LANES), F32), pltpu.VMEM((nb * seq, LANES), F32),
                        pltpu.VMEM((nb * seq, LANES), F32)],
        compiler_params=_params("parallel", "parallel"),
        name="dsa",
    )(h, h, h)


FF_CHUNK = 1024
POST_ROW_TILE = 1024


def _post_kernel(oa_ref, ob_ref, oc_ref, od_ref, x_ref, wo_ref, g1_ref, b1_ref, w1_ref, w2_ref, g2_ref, b2_ref,
                 y_ref):
    mixer_out = jnp.concatenate([_bf(oa_ref[...]), _bf(ob_ref[...]), _bf(oc_ref[...]), _bf(od_ref[...])], axis=1)
    mixed = _dot(mixer_out, wo_ref[...])
    x = _layer_norm_rows(DEEPNORM_ALPHA * x_ref[...] + mixed, g1_ref[...], b1_ref[...])
    xb = _bf(x)
    acc = jnp.zeros(x.shape, F32)
    for c in range(D_FF // FF_CHUNK):
        hmid = _dot(xb, w1_ref[:, c * FF_CHUNK:(c + 1) * FF_CHUNK])
        hmid = jnp.square(jnp.maximum(hmid, 0.0))
        acc = acc + _dot(_bf(hmid), w2_ref[c * FF_CHUNK:(c + 1) * FF_CHUNK, :])
    y_ref[...] = _layer_norm_rows(DEEPNORM_ALPHA * x + acc, g2_ref[...], b2_ref[...])


def _post(oa, ob, oc, od, x2d, wo, g1, b1, w1, w2, g2, b2, layer):
    n = x2d.shape[0]
    tm = POST_ROW_TILE
    grp = pl.BlockSpec((tm, GROUP_WIDTH), lambda i: (i, 0))
    row = pl.BlockSpec((tm, D_MODEL), lambda i: (i, 0))
    resident = lambda a: _layer_block(a, layer, pipeline_mode=pl.Buffered(1))
    return pl.pallas_call(
        _post_kernel,
        grid=(n // tm,),
        in_specs=[grp, grp, grp, grp, row, resident(wo), resident(g1), resident(b1),
                  resident(w1), resident(w2), resident(g2), resident(b2)],
        out_specs=row,
        out_shape=jax.ShapeDtypeStruct((n, D_MODEL), F32),
        compiler_params=_params("parallel"),
        name="post",
    )(oa, ob, oc, od, x2d, wo, g1, b1, w1, w2, g2, b2)


def _expand_heads(wrows):
    return jnp.repeat(wrows, HEAD_DIM, axis=-2)


def _arrange_w_in(w):
    wt = jnp.swapaxes(_bf(w), -1, -2)
    pts = np.cumsum(IN_SIZES)[:-1].tolist()
    (a_qkv, a_a, a_b, a_z, b_cq, b_ckv, b_kr, c_q, c_f, c_i, c_g, d_qkv) = jnp.split(wt, pts, axis=-2)
    z = lambda n: jnp.zeros(wt.shape[:-2] + (n, wt.shape[-1]), BF16)
    half = MLA_ROPE // 2
    kr_sw = jnp.concatenate([b_kr[..., half:, :], b_kr[..., :half, :]], axis=-2)
    rows = [a_qkv, _expand_heads(a_a), _expand_heads(a_b), a_z,
            b_cq, z(256 - MLA_Q_RANK), b_ckv,
            z(MLA_NOPE), b_kr, z(LANES - MLA_NOPE - MLA_ROPE),
            z(MLA_NOPE), kr_sw, z(LANES - MLA_NOPE - MLA_ROPE), z(C_HGRN - C_MLA_KRS - LANES),
            c_q, c_f, c_i, c_g, d_qkv]
    return jnp.concatenate(rows, axis=-2)


def _arrange_mla_weights(w_uq, w_ukv):
    lead = w_uq.shape[:-2]
    rq = w_uq.shape[-2]
    wq = _bf(w_uq).reshape(lead + (rq, N_HEADS, MLA_NOPE + MLA_ROPE))
    half = MLA_ROPE // 2
    zq = jnp.zeros(lead + (rq, N_HEADS, LANES - MLA_NOPE - MLA_ROPE), BF16)
    q_main = jnp.concatenate([wq, zq], axis=-1)
    rope = wq[..., MLA_NOPE:]
    rope_sw = jnp.concatenate([rope[..., half:], rope[..., :half]], axis=-1)
    q_swap = jnp.concatenate([jnp.zeros(lead + (rq, N_HEADS, MLA_NOPE), BF16), rope_sw, zq], axis=-1)
    pad_rows = lambda m: jnp.concatenate([m, jnp.zeros(lead + (256 - rq, m.shape[-1]), BF16)], axis=-2)
    q_main = pad_rows(q_main.reshape(lead + (rq, N_HEADS * LANES)))
    q_swap = pad_rows(q_swap.reshape(lead + (rq, N_HEADS * LANES)))
    rkv = w_ukv.shape[-2]
    wkv = _bf(w_ukv).reshape(lead + (rkv, N_HEADS, MLA_NOPE + HEAD_DIM))
    k_w = jnp.concatenate([wkv[..., :MLA_NOPE], jnp.zeros(lead + (rkv, N_HEADS, LANES - MLA_NOPE), BF16)], axis=-1)
    v_w = wkv[..., MLA_NOPE:]
    return (q_main, q_swap, k_w.reshape(lead + (rkv, N_HEADS * LANES)),
            v_w.reshape(lead + (rkv, N_HEADS * HEAD_DIM)))


def _rope_tables(seq):
    half = MLA_ROPE // 2
    pos = jnp.arange(seq, dtype=F32)
    inv_freq = ROPE_BASE ** (-jnp.arange(half, dtype=F32) / half)
    ang = pos[:, None] * inv_freq[None, :]
    cos, sin = jnp.cos(ang), jnp.sin(ang)
    ones = jnp.ones((seq, MLA_NOPE), F32)
    zeros = jnp.zeros((seq, MLA_NOPE), F32)
    tail1 = jnp.ones((seq, LANES - MLA_NOPE - MLA_ROPE), F32)
    tail0 = jnp.zeros((seq, LANES - MLA_NOPE - MLA_ROPE), F32)
    cos_t = jnp.concatenate([ones, cos, cos, tail1], axis=1)
    sin_t = jnp.concatenate([zeros, -sin, sin, tail0], axis=1)
    return cos_t, sin_t


def _tile_heads(vec):
    return jnp.tile(vec.astype(F32), N_HEADS)[..., None, :]


def kernel(x, w_in, gdn_conv_w, gdn_a_log, gdn_dt_bias, gdn_norm_g, mla_q_norm_g, mla_kv_norm_g,
           mla_w_uq, mla_w_ukv, hgrn_lb_logits, hgrn_norm_g, w_out, ln1_g, ln1_b, w_ff1, w_ff2,
           ln2_g, ln2_b):
    batch, seq, d_model = x.shape
    assert d_model == D_MODEL and seq % (16 * DSA_BLOCK) == 0 and (batch * seq) % ROW_TILE == 0
    x2d = x.reshape(batch * seq, d_model)
    p_lb = jax.nn.softmax(hgrn_lb_logits.astype(F32), axis=0)
    lower_bounds = (jnp.cumsum(p_lb, axis=0) - p_lb[:1])[:, None, :]
    cos_t, sin_t = _rope_tables(seq)
    w_in_a = _arrange_w_in(w_in)
    avec = jnp.repeat(-jnp.exp(gdn_a_log.astype(F32)), HEAD_DIM, axis=-1)[:, None, :]
    dtb = jnp.repeat(gdn_dt_bias.astype(F32), HEAD_DIM, axis=-1)[:, None, :]
    conv_w = gdn_conv_w.astype(F32)
    gdn_g = _tile_heads(gdn_norm_g)
    hgrn_g = _tile_heads(hgrn_norm_g)
    wq, wqs, wk, wv = _arrange_mla_weights(mla_w_uq, mla_w_ukv)
    qg = jnp.concatenate([mla_q_norm_g.astype(F32), jnp.zeros((DEPTH, 256 - MLA_Q_RANK), F32)], axis=-1)[:, None, :]
    kvg = mla_kv_norm_g.astype(F32)[:, None, :]
    wo, w1, w2 = _bf(w_out), _bf(w_ff1), _bf(w_ff2)
    g1, b1, g2, b2 = (t.astype(F32)[:, None, :] for t in (ln1_g, ln1_b, ln2_g, ln2_b))
    for l in range(DEPTH):
        h = _inproj(x2d, w_in_a, l)
        o_a = _gdn(h, batch, seq, conv_w[l], avec[l], dtb[l], gdn_g[l])
        o_b = _mla(h, batch, seq, cos_t, sin_t, qg[l], kvg[l], wq[l], wqs[l], wk[l], wv[l])
        o_c = _hgrn(h, batch, seq, lower_bounds[l], hgrn_g[l])
        o_d = _dsa(h, batch, seq)
        x2d = _post(o_a, o_b, o_c, o_d, x2d, wo, g1, b1, w1, w2, g2, b2, l)
    return x2d.reshape(batch, seq, d_model)
```
